```python
import math
import jax, jax.numpy as jnp
from jax import lax
import numpy as np

D_MODEL = 2048
BATCH = 8
SEQ = 4096
DEPTH = 2
DEC_BATCH = 8
DEC_SEQ = 16
PAST_LEN = 1024

CHUNK = 64
SCAN_BLOCK = CHUNK
N_EVEN = (DEPTH + 1) // 2
N_ODD = DEPTH // 2
MIX_W = D_MODEL
HALF_W = MIX_W // 2
A_CHUNK = 128
H_A = 4
DA = HALF_W // H_A
H_B = 8
K_B = 128
V_B = HALF_W // H_B
H_C = 4
V_C = HALF_W // H_C
K_C = V_C // 2
GLA_RANK = 16
GLA_TAU = 16.0
P_D = 64
H_D = HALF_W // P_D
G_D = 2
HPG_D = H_D // G_D
N_D = 128
D_CONV = 4
CONV_DIM = HALF_W + 2 * G_D * N_D
N_EXPERTS = 64
N_EXPERT_GROUPS = 8
TOPK_GROUPS = 4
TOP_K = 8
D_EXPERT = 512
D_SHARED = 512
ROUTE_SCALE = 2.5
MOE_BLOCK = 128
DEEPNORM_ALPHA = (2 * DEPTH) ** 0.25
DEEPNORM_BETA = (8 * DEPTH) ** -0.25
EPS = 1e-5
EV_SIZES = (HALF_W, HALF_W, H_B * K_B, H_B * K_B, H_B * V_B, H_B * V_B)
OD_SIZES = (H_C * K_C, H_C * K_C, H_C * V_C, GLA_RANK, H_C * V_C, HALF_W, CONV_DIM, H_D)
EV_IN = sum(EV_SIZES)
OD_IN = sum(OD_SIZES)

kernel_name = 'hybrid_stream_gmlp_hgrn2_gla_ssd_moe'


def layer_norm(x, g, b):
    xf = x.astype(jnp.float32)
    mu = jnp.mean(xf, -1, keepdims=True)
    var = jnp.mean(jnp.square(xf - mu), -1, keepdims=True)
    return ((xf - mu) * lax.rsqrt(var + EPS) * g + b).astype(x.dtype)


def rms_norm(x, g):
    xf = x.astype(jnp.float32)
    return (xf * lax.rsqrt(jnp.mean(jnp.square(xf), -1, keepdims=True) + EPS) * g).astype(x.dtype)


def split_last(a, sizes):
    return jnp.split(a, [int(s) for s in np.cumsum(sizes)[:-1]], axis=-1)


def block_len(L):
    return SCAN_BLOCK if L % SCAN_BLOCK == 0 else L


def to_blocks(a, blk):
    a = a.reshape(a.shape[0], a.shape[1] // blk, blk, *a.shape[2:])
    return jnp.moveaxis(a, 1, 0)


def from_blocks(a):
    a = jnp.moveaxis(a, 0, 1)
    return a.reshape(a.shape[0], -1, *a.shape[3:])


def chunked_gla(q, k, v, log_f, s0):
    out_dtype = q.dtype
    blk = block_len(q.shape[1])
    mask = jnp.tril(jnp.ones((blk, blk), bool))
    xs = tuple(to_blocks(a.astype(jnp.float32), blk) for a in (q, k, v, log_f))

    def step(S, inp):
        qc, kc, vc, gc = inp
        cum = jnp.cumsum(gc, axis=1)
        diff = cum[:, :, None] - cum[:, None, :]
        decay = jnp.exp(jnp.where(mask[None, :, :, None, None], diff, -jnp.inf))
        att = jnp.einsum('bihk,bjhk,bijhk->bhij', qc, kc, decay)
        o = (jnp.einsum('bhij,bjhv->bihv', att, vc)
             + jnp.einsum('bihk,bhkv->bihv', qc * jnp.exp(cum), S))
        last = cum[:, -1]
        S = (jnp.exp(last)[..., None] * S
             + jnp.einsum('bjhk,bjhv->bhkv', kc * jnp.exp(last[:, None] - cum), vc))
        return S, o

    S, o = lax.scan(step, s0.astype(jnp.float32), xs)
    return from_blocks(o).astype(out_dtype), S


def chunked_ssd(x, la, bm, cm, s0):
    out_dtype = x.dtype
    blk = block_len(x.shape[1])
    mask = jnp.tril(jnp.ones((blk, blk), bool))
    xs = tuple(to_blocks(a.astype(jnp.float32), blk) for a in (x, la, bm, cm))

    def step(S, inp):
        xc, ac, bc, cc = inp
        cum = jnp.cumsum(ac, axis=1)
        diff = cum[:, :, None] - cum[:, None, :]
        decay = jnp.exp(jnp.where(mask[None, :, :, None, None], diff, -jnp.inf))
        cb = jnp.einsum('bign,bjgn->bijg', cc, bc)
        y = (jnp.einsum('bijg,bijgh,bjghp->bighp', cb, decay, xc)
             + jnp.einsum('bign,bghnp->bighp', cc, S) * jnp.exp(cum)[..., None])
        last = cum[:, -1]
        S = (jnp.exp(last)[..., None, None] * S
             + jnp.einsum('bjgn,bjgh,bjghp->bghnp', bc, jnp.exp(last[:, None] - cum), xc))
        return S, y

    S, y = lax.scan(step, s0.astype(jnp.float32), xs)
    return from_blocks(y).astype(out_dtype), S


def spatial_gating(h_u, h_v, ln_g, ln_b, ws, bs):
    bsz, L, _ = h_u.shape
    v = layer_norm(h_v.reshape(bsz, L, H_A, DA), ln_g, ln_b)
    blk = min(A_CHUNK, L)
    pos = jnp.arange(A_CHUNK)
    mask = (pos[None, :] // CHUNK) <= (pos[:, None] // CHUNK)
    w = (ws * mask)[:, :blk, :blk]
    vb = v.reshape(bsz, L // blk, blk, H_A, DA)
    s = jnp.einsum('hij,bnjhd->bnihd', w, vb) + bs[:, :blk].T[None, None, :, :, None]
    y = h_u.reshape(bsz, L // blk, blk, H_A, DA) * s
    return y.reshape(bsz, L, HALF_W).astype(h_u.dtype), v.reshape(bsz, L, HALF_W)


def even_mixer(x, s_hgrn, lb, w_in, a_ln_g, a_ln_b, a_ws, a_bs, b_norm_g, w_out):
    bsz, L, _ = x.shape
    a_u, a_v, b_q, b_f, b_i, b_g = split_last(x @ w_in, EV_SIZES)
    ya, v_rows = spatial_gating(jax.nn.gelu(a_u), jax.nn.gelu(a_v), a_ln_g, a_ln_b, a_ws, a_bs)
    f = (lb + (1.0 - lb) * jax.nn.sigmoid(b_f.astype(jnp.float32))).reshape(bsz, L, H_B, K_B)
    q = jax.nn.silu(b_q).reshape(bsz, L, H_B, K_B)
    o, s_new = chunked_gla(q, 1.0 - f, b_i.reshape(bsz, L, H_B, V_B), jnp.log(f), s_hgrn)
    yb = rms_norm(o, b_norm_g) * jax.nn.sigmoid(b_g).reshape(bsz, L, H_B, V_B)
    mix = jnp.concatenate([ya, yb.reshape(bsz, L, HALF_W).astype(ya.dtype)], -1) @ w_out
    return mix.astype(x.dtype), v_rows.astype(x.dtype), s_new.astype(x.dtype)


def odd_mixer(x, s_gla, s_ssm, conv_buf, w_in, c_gate_w2, c_gate_b, c_norm_g,
              d_conv_w, d_conv_b, d_dt_bias, d_a_log, d_skip, d_norm_g, w_out):
    bsz, L, _ = x.shape
    c_q, c_k, c_v, c_lr, c_r, d_z, d_xbc, d_dt = split_last(x @ w_in, OD_SIZES)
    q = c_q.reshape(bsz, L, H_C, K_C) * (K_C ** -0.5)
    k = c_k.reshape(bsz, L, H_C, K_C)
    v = c_v.reshape(bsz, L, H_C, V_C)
    log_a = jax.nn.log_sigmoid((c_lr @ c_gate_w2 + c_gate_b).astype(jnp.float32)) / GLA_TAU
    o, s_c = chunked_gla(q, k, v, log_a.reshape(bsz, L, H_C, K_C), s_gla)
    yc = rms_norm(o, c_norm_g).reshape(bsz, L, HALF_W) * jax.nn.silu(c_r)
    xp = jnp.concatenate([conv_buf.astype(d_xbc.dtype), d_xbc], axis=1)
    new_conv = xp[:, -(D_CONV - 1):]
    conv = lax.conv_general_dilated(xp, d_conv_w[:, None, :].astype(xp.dtype), window_strides=(1,),
                                    padding='VALID', dimension_numbers=('NWC', 'WIO', 'NWC'),
                                    feature_group_count=CONV_DIM)
    xbc = jax.nn.silu(conv + d_conv_b)
    xs, bm, cm = split_last(xbc, (HALF_W, G_D * N_D, G_D * N_D))
    dt = jax.nn.softplus(d_dt.astype(jnp.float32) + d_dt_bias).reshape(bsz, L, G_D, HPG_D)
    a = -jnp.exp(d_a_log.astype(jnp.float32)).reshape(G_D, HPG_D)
    xh = xs.reshape(bsz, L, G_D, HPG_D, P_D)
    y, s_d = chunked_ssd(xh * dt[..., None], dt * a, bm.reshape(bsz, L, G_D, N_D),
                         cm.reshape(bsz, L, G_D, N_D), s_ssm.reshape(bsz, G_D, HPG_D, N_D, P_D))
    y = y + xh * d_skip.reshape(G_D, HPG_D)[..., None]
    gate = jax.nn.silu(d_z.astype(jnp.float32)).reshape(bsz, L, G_D, HPG_D * P_D)
    yd = rms_norm(y.reshape(bsz, L, G_D, HPG_D * P_D) * gate, d_norm_g.reshape(G_D, HPG_D * P_D))
    mix = jnp.concatenate([yc.astype(yd.dtype), yd.reshape(bsz, L, HALF_W)], -1) @ w_out
    return (mix.astype(x.dtype), s_c.astype(x.dtype),
            s_d.reshape(bsz, H_D, N_D, P_D).astype(x.dtype), new_conv.astype(x.dtype))


def routed_experts(xt, eidx, gw, w_gate, w_up, w_down):
    T, D = xt.shape
    tk = T * TOP_K
    e_flat = eidx.reshape(-1).astype(jnp.int32)
    w_flat = gw.reshape(-1)
    tok_flat = jnp.repeat(jnp.arange(T, dtype=jnp.int32), TOP_K)
    order = jnp.argsort(e_flat)
    se, stok, sw = e_flat[order], tok_flat[order], w_flat[order]
    counts = jnp.bincount(e_flat, length=N_EXPERTS).astype(jnp.int32)
    padded = (counts + MOE_BLOCK - 1) // MOE_BLOCK * MOE_BLOCK
    pad_end = jnp.cumsum(padded)
    pad_start = pad_end - padded
    cnt_start = jnp.cumsum(counts) - counts
    dest = pad_start[se] + jnp.arange(tk, dtype=jnp.int32) - cnt_start[se]
    n_blocks = -(-(tk + N_EXPERTS * (MOE_BLOCK - 1)) // MOE_BLOCK)
    slot_tok = jnp.zeros((n_blocks * MOE_BLOCK,), jnp.int32).at[dest].set(stok)
    slot_w = jnp.zeros((n_blocks * MOE_BLOCK,), jnp.float32).at[dest].set(sw)
    blk_e = jnp.minimum(jnp.searchsorted(pad_end, jnp.arange(n_blocks, dtype=jnp.int32) * MOE_BLOCK,
                                         side='right'), N_EXPERTS - 1)

    def step(y, inp):
        tok, w, e = inp
        xb = xt[tok]
        hb = jax.nn.silu(xb @ w_gate[e]) * (xb @ w_up[e])
        return y.at[tok].add((hb @ w_down[e]).astype(jnp.float32) * w[:, None]), None

    y, _ = lax.scan(step, jnp.zeros((T, D), jnp.float32),
                    (slot_tok.reshape(n_blocks, MOE_BLOCK), slot_w.reshape(n_blocks, MOE_BLOCK), blk_e))
    return y


def moe(x, w_router, router_bias, w_gate, w_up, w_down, ws_gate, ws_up, ws_down):
    bsz, L, D = x.shape
    xt = x.reshape(-1, D)
    T = xt.shape[0]
    scores = jax.nn.sigmoid((xt @ w_router).astype(jnp.float32))
    sel = scores + router_bias.astype(jnp.float32)
    grp_score = lax.top_k(sel.reshape(T, N_EXPERT_GROUPS, -1), 2)[0].sum(-1)
    _, top_groups = lax.top_k(grp_score, TOPK_GROUPS)
    gmask = jnp.any(top_groups[..., None] == jnp.arange(N_EXPERT_GROUPS), axis=1)
    emask = jnp.repeat(gmask, N_EXPERTS // N_EXPERT_GROUPS, axis=1)
    _, eidx = lax.top_k(jnp.where(emask, sel, -jnp.inf), TOP_K)
    gw = jnp.take_along_axis(scores, eidx, axis=1)
    gw = gw / jnp.sum(gw, -1, keepdims=True) * ROUTE_SCALE
    routed = routed_experts(xt, eidx, gw, w_gate, w_up, w_down)
    shared = (jax.nn.silu(xt @ ws_gate) * (xt @ ws_up)) @ ws_down
    return (routed + shared.astype(jnp.float32)).astype(x.dtype).reshape(bsz, L, D)


def setup_inputs(seed: int = 0) -> dict:
    key = jax.random.key(seed)
    ks = iter(jax.random.split(key, 64))

    def nrm(shape, scale):
        return jax.random.normal(next(ks), shape, jnp.float32) * scale

    def gain(shape):
        return 1.0 + nrm(shape, 0.1)

    dt0 = jnp.exp(jax.random.uniform(next(ks), (N_ODD, H_D), jnp.float32)
                  * (math.log(0.1) - math.log(0.001)) + math.log(0.001))
    return {
        'x_prompt': nrm((BATCH, SEQ, D_MODEL), 1.0),
        'x_sample': nrm((DEC_BATCH, DEC_SEQ, D_MODEL), 1.0),
        'state_b_hgrn': nrm((N_EVEN, DEC_BATCH, H_B, K_B, V_B), 0.5),
        'state_c_gla': nrm((N_ODD, DEC_BATCH, H_C, K_C, V_C), 1.0),
        'state_d_ssm': nrm((N_ODD, DEC_BATCH, H_D, N_D, P_D), 0.1),
        'state_d_conv': nrm((N_ODD, DEC_BATCH, D_CONV - 1, CONV_DIM), 1.0),
        'ev_w_in': nrm((N_EVEN, D_MODEL, EV_IN), D_MODEL ** -0.5),
        'ev_a_ln_g': gain((N_EVEN, H_A, DA)),
        'ev_a_ln_b': nrm((N_EVEN, H_A, DA), 0.1),
        'ev_a_ws': nrm((N_EVEN, H_A, A_CHUNK, A_CHUNK), A_CHUNK ** -0.5),
        'ev_a_bs': gain((N_EVEN, H_A, A_CHUNK)),
        'ev_b_norm_g': gain((N_EVEN, H_B, V_B)),
        'ev_w_out': nrm((N_EVEN, MIX_W, D_MODEL), MIX_W ** -0.5 * DEEPNORM_BETA),
        'hgrn_lb_logits': nrm((DEPTH + 1, H_B * K_B), 0.5),
        'od_w_in': nrm((N_ODD, D_MODEL, OD_IN), D_MODEL ** -0.5),
        'od_c_gate_w2': nrm((N_ODD, GLA_RANK, H_C * K_C), GLA_RANK ** -0.5),
        'od_c_gate_b': nrm((N_ODD, H_C * K_C), 0.1),
        'od_c_norm_g': gain((N_ODD, H_C, V_C)),
        'od_d_conv_w': nrm((N_ODD, D_CONV, CONV_DIM), D_CONV ** -0.5),
        'od_d_conv_b': nrm((N_ODD, CONV_DIM), 0.1),
        'od_d_dt_bias': dt0 + jnp.log(-jnp.expm1(-dt0)),
        'od_d_a_log': jnp.log(jax.random.uniform(next(ks), (N_ODD, H_D), jnp.float32, 1.0, 16.0)),
        'od_d_skip': gain((N_ODD, H_D)),
        'od_d_norm_g': gain((N_ODD, HALF_W)),
        'od_w_out': nrm((N_ODD, MIX_W, D_MODEL), MIX_W ** -0.5 * DEEPNORM_BETA),
        'ln1_g': gain((DEPTH, D_MODEL)),
        'ln1_b': nrm((DEPTH, D_MODEL), 0.1),
        'ln2_g': gain((DEPTH, D_MODEL)),
        'ln2_b': nrm((DEPTH, D_MODEL), 0.1),
        'moe_w_router': nrm((DEPTH, D_MODEL, N_EXPERTS), D_MODEL ** -0.5),
        'moe_router_bias': nrm((DEPTH, N_EXPERTS), 0.01),
        'moe_w_gate': nrm((DEPTH, N_EXPERTS, D_MODEL, D_EXPERT), D_MODEL ** -0.5),
        'moe_w_up': nrm((DEPTH, N_EXPERTS, D_MODEL, D_EXPERT), D_MODEL ** -0.5),
        'moe_w_down': nrm((DEPTH, N_EXPERTS, D_EXPERT, D_MODEL), D_EXPERT ** -0.5 * DEEPNORM_BETA),
        'moe_ws_gate': nrm((DEPTH, D_MODEL, D_SHARED), D_MODEL ** -0.5),
        'moe_ws_up': nrm((DEPTH, D_MODEL, D_SHARED), D_MODEL ** -0.5),
        'moe_ws_down': nrm((DEPTH, D_SHARED, D_MODEL), D_SHARED ** -0.5 * DEEPNORM_BETA),
    }


def reference(x_prompt, x_sample, state_b_hgrn, state_c_gla, state_d_ssm, state_d_conv,
              ev_w_in, ev_a_ln_g, ev_a_ln_b, ev_a_ws, ev_a_bs, ev_b_norm_g, ev_w_out, hgrn_lb_logits,
              od_w_in, od_c_gate_w2, od_c_gate_b, od_c_norm_g, od_d_conv_w, od_d_conv_b,
              od_d_dt_bias, od_d_a_log, od_d_skip, od_d_norm_g, od_w_out,
              ln1_g, ln1_b, ln2_g, ln2_b, moe_w_router, moe_router_bias,
              moe_w_gate, moe_w_up, moe_w_down, moe_ws_gate, moe_ws_up, moe_ws_down):
    lower_bounds = jnp.cumsum(jax.nn.softmax(hgrn_lb_logits.astype(jnp.float32), axis=0), axis=0)

    def trunk(x, hgrn0, gla0, ssm0, conv0):
        v_rows, hgrn1, gla1, ssm1, conv1 = [], [], [], [], []
        for l in range(DEPTH):
            j = l // 2
            if l % 2 == 0:
                mix, v, s_b = even_mixer(x, hgrn0[j], lower_bounds[l], ev_w_in[j], ev_a_ln_g[j],
                                         ev_a_ln_b[j], ev_a_ws[j], ev_a_bs[j], ev_b_norm_g[j], ev_w_out[j])
                v_rows.append(v)
                hgrn1.append(s_b)
            else:
                mix, s_c, s_d, cv = odd_mixer(x, gla0[j], ssm0[j], conv0[j], od_w_in[j], od_c_gate_w2[j],
                                              od_c_gate_b[j], od_c_norm_g[j], od_d_conv_w[j], od_d_conv_b[j],
                                              od_d_dt_bias[j], od_d_a_log[j], od_d_skip[j], od_d_norm_g[j],
                                              od_w_out[j])
                gla1.append(s_c)
                ssm1.append(s_d)
                conv1.append(cv)
            x = layer_norm(DEEPNORM_ALPHA * x + mix, ln1_g[l], ln1_b[l])
            ffn = moe(x, moe_w_router[l], moe_router_bias[l], moe_w_gate[l], moe_w_up[l], moe_w_down[l],
                      moe_ws_gate[l], moe_ws_up[l], moe_ws_down[l])
            x = layer_norm(DEEPNORM_ALPHA * x + ffn, ln2_g[l], ln2_b[l])
        return x, jnp.stack(v_rows), jnp.stack(hgrn1), jnp.stack(gla1), jnp.stack(ssm1), jnp.stack(conv1)

    bp = x_prompt.shape[0]
    dtp = x_prompt.dtype
    y_prompt, _, hgrn_p, gla_p, ssm_p, conv_p = trunk(
        x_prompt,
        jnp.zeros((N_EVEN, bp, H_B, K_B, V_B), dtp),
        jnp.zeros((N_ODD, bp, H_C, K_C, V_C), dtp),
        jnp.zeros((N_ODD, bp, H_D, N_D, P_D), dtp),
        jnp.zeros((N_ODD, bp, D_CONV - 1, CONV_DIM), dtp))
    y_sample, a_v_sample, hgrn_s, gla_s, ssm_s, conv_s = trunk(
        x_sample, state_b_hgrn, state_c_gla, state_d_ssm, state_d_conv)
    return (y_prompt, y_sample, a_v_sample, hgrn_p, hgrn_s, gla_p, gla_s, ssm_p, ssm_s, conv_p, conv_s)
```

```python
import functools
import math

import jax
import jax.numpy as jnp
import numpy as np
from jax import lax
from jax.experimental import pallas as pl
from jax.experimental.pallas import tpu as pltpu

F32 = jnp.float32
BF16 = jnp.bfloat16
I32 = jnp.int32

D_MODEL = 2048
DEPTH = 2
CHUNK = 64
SUB = 16
HALF_W = D_MODEL // 2
A_CHUNK = 128
H_A = 4
DA = HALF_W // H_A
H_B = 8
K_B = 128
V_B = HALF_W // H_B
H_C = 4
V_C = HALF_W // H_C
K_C = V_C // 2
GLA_RANK = 16
GLA_TAU = 16.0
P_D = 64
H_D = HALF_W // P_D
G_D = 2
HPG_D = H_D // G_D
N_D = 128
D_CONV = 4
CONV_DIM = HALF_W + 2 * G_D * N_D
N_EXPERTS = 64
N_GROUPS = 8
GROUP_SIZE = N_EXPERTS // N_GROUPS
TOPK_GROUPS = 4
TOP_K = 8
D_EXPERT = 512
ROUTE_SCALE = 2.5
ALPHA = (2 * DEPTH) ** 0.25
EPS = 1e-5
LANE = 128
VMEM_LIMIT = 56 * 1024 * 1024

OD_Q, OD_K, OD_V, OD_R, OD_Z, OD_LRDT, OD_XBC = 0, 512, 1024, 2048, 3072, 4096, 4608
OD_PAD = 6144
DT_OFF = GLA_RANK


def _cparams(*sem):
    return pltpu.CompilerParams(dimension_semantics=sem, vmem_limit_bytes=VMEM_LIMIT)


def _split3(x):
    hi = x.astype(BF16)
    r = x - hi.astype(F32)
    mid = r.astype(BF16)
    lo = (r - mid.astype(F32)).astype(BF16)
    return hi, mid, lo


def _split2(x):
    hi = x.astype(BF16)
    return hi, (x - hi.astype(F32)).astype(BF16)


def _dot_exact_l(a_bf16, x):
    return sum(jnp.dot(a_bf16, p, preferred_element_type=F32) for p in _split3(x))


def _dot_exact_r(x, b_bf16):
    return sum(jnp.dot(p, b_bf16, preferred_element_type=F32) for p in _split3(x))


def _dot_nt(a, b):
    return lax.dot_general(a, b, (((1,), (1,)), ((), ())), preferred_element_type=F32)


def _softplus(x):
    return jnp.maximum(x, 0.0) + jnp.log1p(jnp.exp(-jnp.abs(x)))


def _proj_kernel(x_ref, w_ref, o_ref, *scratch):
    if scratch:
        xb_ref, = scratch

        @pl.when(pl.program_id(1) == 0)
        def _():
            xb_ref[...] = x_ref[...].astype(BF16)
        x = xb_ref[...]
    else:
        x = x_ref[...]
    o_ref[...] = jnp.dot(x, w_ref[...], preferred_element_type=F32)


def _proj(x, w, tm, tn):
    t, k = x.shape
    n = w.shape[1]
    scratch = [] if x.dtype == BF16 else [pltpu.VMEM((tm, k), BF16)]
    return pl.pallas_call(
        _proj_kernel, grid=(t // tm, n // tn),
        in_specs=[pl.BlockSpec((tm, k), lambda i, j: (i, 0)),
                  pl.BlockSpec((k, tn), lambda i, j: (0, j))],
        out_specs=pl.BlockSpec((tm, tn), lambda i, j: (i, j)),
        out_shape=jax.ShapeDtypeStruct((t, n), F32),
        scratch_shapes=scratch, compiler_params=_cparams("parallel", "arbitrary"),
        name="in_proj")(x, w)


def _mixa_kernel(u_ref, v_ref, lng_ref, lnb_ref, w_ref, bs_ref, ya_ref, *vrows, ca):
    gu = jax.nn.gelu(u_ref[...])
    gv = jax.nn.gelu(v_ref[...])
    rows = gu.shape[0]
    for h in range(H_A):
        sl = slice(h * DA, (h + 1) * DA)
        vh = gv[:, sl]
        mu = jnp.mean(vh, -1, keepdims=True)
        d = vh - mu
        var = jnp.mean(d * d, -1, keepdims=True)
        vn = d * lax.rsqrt(var + EPS) * lng_ref[h] + lnb_ref[h]
        if vrows:
            vrows[0][:, sl] = vn
        vnb = vn.astype(BF16)
        for c in range(rows // ca):
            rs = slice(c * ca, (c + 1) * ca)
            s = jnp.dot(w_ref[h], vnb[rs], preferred_element_type=F32) + bs_ref[h]
            ya_ref[rs, sl] = (gu[rs, sl] * s).astype(BF16)


def _mixer_a(proj, lng, lnb, ws, bs, seq_len, rows, emit_v):
    t = proj.shape[0]
    ca = min(A_CHUNK, seq_len)
    pos = np.arange(A_CHUNK)
    mask = (pos[None, :] // CHUNK) <= (pos[:, None] // CHUNK)
    w = (ws * mask)[:, :ca, :ca].astype(BF16)
    bsb = jnp.broadcast_to(bs[:, :ca, None], (H_A, ca, DA)).astype(F32)
    nb = HALF_W // HALF_W
    out_shape = [jax.ShapeDtypeStruct((t, HALF_W), BF16)]
    out_specs = [pl.BlockSpec((rows, HALF_W), lambda i: (i, 0))]
    if emit_v:
        out_shape.append(jax.ShapeDtypeStruct((t, HALF_W), F32))
        out_specs.append(pl.BlockSpec((rows, HALF_W), lambda i: (i, 0)))
    del nb
    res = pl.pallas_call(
        functools.partial(_mixa_kernel, ca=ca), grid=(t // rows,),
        in_specs=[pl.BlockSpec((rows, HALF_W), lambda i: (i, 0)),
                  pl.BlockSpec((rows, HALF_W), lambda i: (i, 1)),
                  pl.BlockSpec((H_A, 1, DA), lambda i: (0, 0, 0)),
                  pl.BlockSpec((H_A, 1, DA), lambda i: (0, 0, 0)),
                  pl.BlockSpec((H_A, ca, ca), lambda i: (0, 0, 0)),
                  pl.BlockSpec((H_A, ca, DA), lambda i: (0, 0, 0))],
        out_specs=out_specs, out_shape=out_shape,
        compiler_params=_cparams("parallel"), name="mixer_a")(
            proj, proj, lng[:, None, :], lnb[:, None, :], w, bsb)
    return res if emit_v else (res[0], None)


def _scan_mats(c):
    i = np.arange(c)[:, None]
    j = np.arange(c)[None, :]
    tril = (j <= i)
    local = tril & ((i // SUB) == (j // SUB))
    ones = np.ones((c, c), bool)
    return jnp.asarray(np.concatenate([local, tril, ones], 0), BF16)


def _gla_head(q, k, v, lc, cum, last, s_ref, h, c):
    ns = c // SUB
    q_hi, q_lo = _split2(q * jnp.exp(lc))
    pre = cum - lc
    vb = v.astype(BF16)
    parts = []
    for blk in range(ns):
        n = SUB * (blk + 1)
        r0 = SUB * blk
        k_hi, k_lo = _split2(k[:n] * jnp.exp(pre[r0:r0 + 1] - cum[:n]))
        rq = slice(r0, r0 + SUB)
        att = _dot_nt(q_hi[rq], k_hi) + (_dot_nt(q_hi[rq], k_lo) + _dot_nt(q_lo[rq], k_hi))
        row = lax.broadcasted_iota(I32, (SUB, n), 0)
        col = lax.broadcasted_iota(I32, (SUB, n), 1)
        att = jnp.where(col - r0 <= row, att, 0.0)
        parts.append(jnp.dot(att.astype(BF16), vb[:n], preferred_element_type=F32))
    o = parts[0] if ns == 1 else jnp.concatenate(parts, 0)
    st = s_ref[h]
    o = o + _dot_nt((q * jnp.exp(cum)).astype(BF16), st.astype(BF16))
    kc = (k * jnp.exp(last - cum)).astype(BF16)
    upd = jnp.dot(v.T.astype(BF16), kc, preferred_element_type=F32)
    s_ref[h] = st * jnp.exp(last[0:1]) + upd
    return o


def _hgrn_kernel(q_ref, f_ref, i_ref, g_ref, lbl_ref, ng_ref, s0_ref, cm_ref, y_ref, so_ref, s_ref,
                 *, c, layer):
    step = pl.program_id(1)

    @pl.when(step == 0)
    def _():
        s_ref[...] = s0_ref[0]

    lg = lbl_ref[...]
    ex = jnp.exp(lg - jnp.max(lg, 0, keepdims=True))
    sm = ex / jnp.sum(ex, 0, keepdims=True)
    lb = jnp.sum(sm[:layer + 1], 0, keepdims=True)
    rows = q_ref.shape[0]
    for ci in range(rows // c):
        rs = slice(ci * c, (ci + 1) * c)
        f = lb + (1.0 - lb) * jax.nn.sigmoid(f_ref[rs])
        g = jnp.log(f)
        k = 1.0 - f
        q = jax.nn.silu(q_ref[rs])
        v = i_ref[rs]
        sc = _dot_exact_l(cm_ref[...], g)
        for h in range(H_B):
            hs = slice(h * K_B, (h + 1) * K_B)
            o = _gla_head(q[:, hs], k[:, hs], v[:, hs], sc[0:c, hs], sc[c:2 * c, hs],
                          sc[2 * c:3 * c, hs], s_ref, h, c)
            ms = jnp.mean(o * o, -1, keepdims=True)
            y = o * lax.rsqrt(ms + EPS) * ng_ref[:, hs] * jax.nn.sigmoid(g_ref[rs, hs])
            y_ref[rs, hs] = y.astype(BF16)

    @pl.when(step == pl.num_programs(1) - 1)
    def _():
        so_ref[0] = s_ref[...]


def _mixer_b(proj, lb_logits, norm_g, s0, seq_len, rows, layer):
    t = proj.shape[0]
    bsz = t // seq_len
    c = CHUNK if seq_len % CHUNK == 0 else seq_len
    spb = seq_len // rows
    s0t = jnp.swapaxes(s0, -1, -2)
    col = lambda j: pl.BlockSpec((rows, HALF_W), lambda b, s, j=j: (b * spb + s, j))
    nl = lb_logits.shape[0]
    y, st = pl.pallas_call(
        functools.partial(_hgrn_kernel, c=c, layer=layer), grid=(bsz, spb),
        in_specs=[col(2), col(3), col(4), col(5),
                  pl.BlockSpec((nl, HALF_W), lambda b, s: (0, 0)),
                  pl.BlockSpec((1, HALF_W), lambda b, s: (0, 0)),
                  pl.BlockSpec((1, H_B, V_B, K_B), lambda b, s: (b, 0, 0, 0)),
                  pl.BlockSpec((3 * c, c), lambda b, s: (0, 0))],
        out_specs=[pl.BlockSpec((rows, HALF_W), lambda b, s: (b * spb + s, 0)),
                   pl.BlockSpec((1, H_B, V_B, K_B), lambda b, s: (b, 0, 0, 0))],
        out_shape=[jax.ShapeDtypeStruct((t, HALF_W), BF16),
                   jax.ShapeDtypeStruct((bsz, H_B, V_B, K_B), F32)],
        scratch_shapes=[pltpu.VMEM((H_B, V_B, K_B), F32)],
        compiler_params=_cparams("parallel", "arbitrary"), name="mixer_b_hgrn")(
            proj, proj, proj, proj, lb_logits, norm_g.reshape(1, HALF_W), s0t, _scan_mats(c))
    return y, jnp.swapaxes(st, -1, -2)


def _glac_kernel(q_ref, k_ref, v_ref, r_ref, lrdt_ref, w2_ref, gb_ref, ng_ref, s0_ref, cm_ref,
                 y_ref, so_ref, s_ref, *, c):
    step = pl.program_id(1)

    @pl.when(step == 0)
    def _():
        s_ref[...] = s0_ref[0]

    rows = q_ref.shape[0]
    for ci in range(rows // c):
        rs = slice(ci * c, (ci + 1) * c)
        z = jnp.dot(lrdt_ref[rs], w2_ref[...], preferred_element_type=F32,
                    precision=lax.Precision.HIGHEST) + gb_ref[...]
        g = -_softplus(-z) / GLA_TAU
        q = q_ref[rs] * (K_C ** -0.5)
        k = k_ref[rs]
        v = v_ref[rs]
        sc = _dot_exact_l(cm_ref[...], g)
        for h in range(H_C):
            ks = slice(h * K_C, (h + 1) * K_C)
            vs = slice(h * V_C, (h + 1) * V_C)
            o = _gla_head(q[:, ks], k[:, ks], v[:, vs], sc[0:c, ks], sc[c:2 * c, ks],
                          sc[2 * c:3 * c, ks], s_ref, h, c)
            ms = jnp.mean(o * o, -1, keepdims=True)
            y = o * lax.rsqrt(ms + EPS) * ng_ref[:, vs] * jax.nn.silu(r_ref[rs, vs])
            y_ref[rs, vs] = y.astype(BF16)

    @pl.when(step == pl.num_programs(1) - 1)
    def _():
        so_ref[0] = s_ref[...]


def _mixer_c(proj, gate_w2, gate_b, norm_g, s0, seq_len, rows):
    t = proj.shape[0]
    bsz = t // seq_len
    c = CHUNK if seq_len % CHUNK == 0 else seq_len
    spb = seq_len // rows
    s0t = jnp.swapaxes(s0, -1, -2)
    hk = H_C * K_C

    def col(width, off):
        return pl.BlockSpec((rows, width), lambda b, s: (b * spb + s, off // width))

    y, st = pl.pallas_call(
        functools.partial(_glac_kernel, c=c), grid=(bsz, spb),
        in_specs=[col(hk, OD_Q), col(hk, OD_K), col(HALF_W, OD_V), col(HALF_W, OD_R),
                  col(LANE, OD_LRDT),
                  pl.BlockSpec((LANE, hk), lambda b, s: (0, 0)),
                  pl.BlockSpec((1, hk), lambda b, s: (0, 0)),
                  pl.BlockSpec((1, HALF_W), lambda b, s: (0, 0)),
                  pl.BlockSpec((1, H_C, V_C, K_C), lambda b, s: (b, 0, 0, 0)),
                  pl.BlockSpec((3 * c, c), lambda b, s: (0, 0))],
        out_specs=[pl.BlockSpec((rows, HALF_W), lambda b, s: (b * spb + s, 0)),
                   pl.BlockSpec((1, H_C, V_C, K_C), lambda b, s: (b, 0, 0, 0))],
        out_shape=[jax.ShapeDtypeStruct((t, HALF_W), BF16),
                   jax.ShapeDtypeStruct((bsz, H_C, V_C, K_C), F32)],
        scratch_shapes=[pltpu.VMEM((H_C, V_C, K_C), F32)],
        compiler_params=_cparams("parallel", "arbitrary"), name="mixer_c_gla")(
            proj, proj, proj, proj, proj,
            jnp.pad(gate_w2, ((0, LANE - GLA_RANK), (0, 0))),
            gate_b.reshape(1, hk),
            norm_g.reshape(1, HALF_W), s0t, _scan_mats(c))
    return y, jnp.swapaxes(st, -1, -2)


GW = HPG_D * P_D


def _ssd_kernel(z_ref, xbc_ref, lrdt_ref, cw_ref, cb_ref, dtb_ref, alog_ref, skip_ref, ng_ref,
                ex_ref, eye_ref, cm_ref, conv0_ref, s0_ref, y_ref, so_ref, s_ref, tail_ref, *, c):
    step = pl.program_id(1)

    @pl.when(step == 0)
    def _():
        s_ref[...] = s0_ref[0]
        tail_ref[...] = conv0_ref[0]

    x = xbc_ref[...]
    rows = x.shape[0]
    tail = tail_ref[...]
    sub8 = lax.broadcasted_iota(I32, (8, CONV_DIM), 0)
    conv = x * cw_ref[D_CONV - 1:D_CONV]
    for sh in range(1, D_CONV):
        rolled = pltpu.roll(x, sh, 0)
        head = jnp.where(sub8 < sh, pltpu.roll(tail, sh, 0), rolled[0:8])
        xk = jnp.concatenate([head, rolled[8:]], 0) if rows > 8 else head
        conv = conv + xk * cw_ref[D_CONV - 1 - sh:D_CONV - sh]
    tail_ref[...] = x[rows - 8:rows]
    xc = jax.nn.silu(conv + cb_ref[...])
    a = -jnp.exp(alog_ref[...])
    ex = ex_ref[...]
    tri = lax.broadcasted_iota(I32, (c, c), 1) <= lax.broadcasted_iota(I32, (c, c), 0)
    for ci in range(rows // c):
        rs = slice(ci * c, (ci + 1) * c)
        xs = xc[rs, 0:HALF_W]
        bm = xc[rs, HALF_W:HALF_W + G_D * N_D]
        cmat = xc[rs, HALF_W + G_D * N_D:CONV_DIM]
        dt = _softplus(lrdt_ref[rs] + dtb_ref[...])
        la = dt * a
        sc = _dot_exact_l(cm_ref[...], la)
        cum = sc[0:c]
        scx = _dot_exact_r(sc, ex)
        cumx = scx[0:c]
        lastx = scx[c:2 * c]
        dtx = _dot_exact_r(dt, ex)
        xdt = xs * dtx
        cum_t = sum(_dot_nt(eye_ref[...], p) for p in _split3(cum))
        e_in = jnp.exp(cumx)
        xw = (xdt * jnp.exp(lastx - cumx)).astype(BF16)
        e_last = jnp.exp(lastx[0:1])
        xdtb = xdt.astype(BF16)
        ys = []
        for g in range(G_D):
            gs = slice(g * GW, (g + 1) * GW)
            ns = slice(g * N_D, (g + 1) * N_D)
            cg = cmat[:, ns].astype(BF16)
            bg = bm[:, ns]
            cb = _dot_nt(cg, bg.astype(BF16))
            parts = []
            for hp in range(HPG_D):
                h = g * HPG_D + hp
                hl = DT_OFF + h
                diff = cum[:, hl:hl + 1] - cum_t[hl:hl + 1, :]
                dec = jnp.where(tri, jnp.exp(jnp.minimum(diff, 0.0)), 0.0)
                wmat = (cb * dec).astype(BF16)
                parts.append(jnp.dot(wmat, xdtb[:, h * P_D:(h + 1) * P_D],
                                     preferred_element_type=F32))
            y_intra = jnp.concatenate(parts, 1)
            sg = s_ref[g]
            y_inter = jnp.dot(cg, sg.astype(BF16), preferred_element_type=F32) * e_in[:, gs]
            upd = jnp.dot(bg.T.astype(BF16), xw[:, gs], preferred_element_type=F32)
            s_ref[g] = sg * e_last[:, gs] + upd
            y = y_intra + y_inter + xs[:, gs] * skip_ref[:, gs]
            yg = y * jax.nn.silu(z_ref[rs, gs])
            ms = jnp.mean(yg * yg, -1, keepdims=True)
            ys.append(yg * lax.rsqrt(ms + EPS) * ng_ref[:, gs])
        y_ref[rs, :] = jnp.concatenate(ys, 1).astype(BF16)

    @pl.when(step == pl.num_programs(1) - 1)
    def _():
        so_ref[0] = s_ref[...]


def _mixer_d(proj, conv_w, conv_b, dt_bias, a_log, skip, norm_g, s0, conv0, seq_len, rows):
    t = proj.shape[0]
    bsz = t // seq_len
    c = CHUNK if seq_len % CHUNK == 0 else seq_len
    spb = seq_len // rows
    s0g = s0.reshape(bsz, G_D, HPG_D, N_D, P_D).transpose(0, 1, 3, 2, 4).reshape(bsz, G_D, N_D, GW)
    conv0p = jnp.pad(conv0, ((0, 0), (8 - (D_CONV - 1), 0), (0, 0)))
    expand_np = np.zeros((LANE, HALF_W), np.float32)
    expand_np[DT_OFF:DT_OFF + H_D] = np.repeat(np.eye(H_D), P_D, axis=1)
    expand = jnp.asarray(expand_np, BF16)
    eye = jnp.asarray(np.eye(LANE), BF16)
    i = np.arange(c)[:, None]
    j = np.arange(c)[None, :]
    cm = jnp.asarray(np.concatenate([j <= i, np.ones((c, c), bool)], 0), BF16)
    skipx = jnp.repeat(skip, P_D)[None, :]
    lane_pad = lambda v: jnp.pad(v.reshape(1, H_D), ((0, 0), (DT_OFF, LANE - DT_OFF - H_D)))

    def col(width, off):
        return pl.BlockSpec((rows, width), lambda b, s: (b * spb + s, off // width))

    def full(shape):
        return pl.BlockSpec(shape, lambda b, s: (0,) * len(shape))

    y, st = pl.pallas_call(
        functools.partial(_ssd_kernel, c=c), grid=(bsz, spb),
        in_specs=[col(HALF_W, OD_Z),
                  col(CONV_DIM, OD_XBC),
                  col(LANE, OD_LRDT),
                  full((D_CONV, CONV_DIM)), full((1, CONV_DIM)), full((1, LANE)), full((1, LANE)),
                  full((1, HALF_W)), full((1, HALF_W)), full((LANE, HALF_W)), full((LANE, LANE)),
                  full((2 * c, c)),
                  pl.BlockSpec((1, 8, CONV_DIM), lambda b, s: (b, 0, 0)),
                  pl.BlockSpec((1, G_D, N_D, GW), lambda b, s: (b, 0, 0, 0))],
        out_specs=[pl.BlockSpec((rows, HALF_W), lambda b, s: (b * spb + s, 0)),
                   pl.BlockSpec((1, G_D, N_D, GW), lambda b, s: (b, 0, 0, 0))],
        out_shape=[jax.ShapeDtypeStruct((t, HALF_W), BF16),
                   jax.ShapeDtypeStruct((bsz, G_D, N_D, GW), F32)],
        scratch_shapes=[pltpu.VMEM((G_D, N_D, GW), F32), pltpu.VMEM((8, CONV_DIM), F32)],
        compiler_params=_cparams("parallel", "arbitrary"), name="mixer_d_ssd")(
            proj, proj, proj, conv_w, conv_b.reshape(1, CONV_DIM), lane_pad(dt_bias),
            lane_pad(a_log), skipx, norm_g.reshape(1, HALF_W), expand, eye, cm, conv0p, s0g)
    st = st.reshape(bsz, G_D, N_D, HPG_D, P_D).transpose(0, 1, 3, 2, 4).reshape(bsz, H_D, N_D, P_D)
    return y, st


def _layer_norm(hpre, g, b):
    mu = jnp.mean(hpre, -1, keepdims=True)
    d = hpre - mu
    var = jnp.mean(d * d, -1, keepdims=True)
    return d * lax.rsqrt(var + EPS) * g + b


U32 = jnp.uint32
HI_MASK = 0xFFFF0000


def _pack_halves(x):
    half = x.shape[1] // 2
    lo = lax.bitcast_convert_type(x[:, :half].astype(BF16).astype(F32), U32) >> 16
    hi = lax.bitcast_convert_type(x[:, half:].astype(BF16).astype(F32), U32) & U32(HI_MASK)
    return lo | hi


def _unpack_halves(w):
    return (lax.bitcast_convert_type(w << 16, F32), lax.bitcast_convert_type(w & U32(HI_MASK), F32))


def _emit_x(xn, o_ref, ob_ref, op_ref):
    o_ref[...] = xn
    ob_ref[...] = xn.astype(BF16)
    op_ref[...] = _pack_halves(xn)


def _x_out(t, tm):
    row = lambda width: pl.BlockSpec((tm, width), lambda i: (i, 0))
    return ([row(D_MODEL), row(D_MODEL), row(HALF_W)],
            [jax.ShapeDtypeStruct((t, D_MODEL), F32), jax.ShapeDtypeStruct((t, D_MODEL), BF16),
             jax.ShapeDtypeStruct((t, HALF_W), U32)])


def _outproj_kernel(ya_ref, yb_ref, x_ref, w_ref, g_ref, b_ref, o_ref, ob_ref, op_ref):
    acc = jnp.dot(ya_ref[...], w_ref[0:HALF_W, :], preferred_element_type=F32)
    acc = acc + jnp.dot(yb_ref[...], w_ref[HALF_W:D_MODEL, :], preferred_element_type=F32)
    _emit_x(_layer_norm(ALPHA * x_ref[...] + acc, g_ref[...], b_ref[...]), o_ref, ob_ref, op_ref)


def _out_proj_ln(ya, yb, x, w, g, b, tm):
    t = x.shape[0]
    row = lambda width: pl.BlockSpec((tm, width), lambda i: (i, 0))
    out_specs, out_shape = _x_out(t, tm)
    return pl.pallas_call(
        _outproj_kernel, grid=(t // tm,),
        in_specs=[row(HALF_W), row(HALF_W), row(D_MODEL),
                  pl.BlockSpec((D_MODEL, D_MODEL), lambda i: (0, 0)),
                  pl.BlockSpec((1, D_MODEL), lambda i: (0, 0)),
                  pl.BlockSpec((1, D_MODEL), lambda i: (0, 0))],
        out_specs=out_specs, out_shape=out_shape,
        compiler_params=_cparams("parallel"), name="out_proj_ln")(
            ya, yb, x, w, g.reshape(1, D_MODEL), b.reshape(1, D_MODEL))


def _router_kernel(x_ref, wr_ref, bias_ref, tri_ref, eidx_ref, gw_ref, rank_ref, cnt_ref, run_ref):
    step = pl.program_id(0)

    @pl.when(step == 0)
    def _():
        run_ref[...] = jnp.zeros_like(run_ref)

    tm = x_ref.shape[0]
    logits = _dot_nt(wr_ref[...], x_ref[...])
    scores = jax.nn.sigmoid(logits)
    sel = scores + bias_ref[:, 0:1]
    neg = -jnp.inf
    sub = lax.broadcasted_iota(I32, (GROUP_SIZE, tm), 0).astype(F32)
    gsc = []
    for g in range(N_GROUPS):
        blk = sel[g * GROUP_SIZE:(g + 1) * GROUP_SIZE]
        m1 = jnp.max(blk, 0, keepdims=True)
        i1 = jnp.min(jnp.where(blk == m1, sub, float(GROUP_SIZE)), 0, keepdims=True)
        m2 = jnp.max(jnp.where(sub == i1, neg, blk), 0, keepdims=True)
        gsc.append(m1 + m2)
    cur = jnp.concatenate(gsc, 0)
    gio = lax.broadcasted_iota(I32, (N_GROUPS, tm), 0).astype(F32)
    gmask = jnp.zeros((N_GROUPS, tm), F32)
    for _ in range(TOPK_GROUPS):
        m = jnp.max(cur, 0, keepdims=True)
        i = jnp.min(jnp.where(cur == m, gio, float(N_GROUPS)), 0, keepdims=True)
        pick = gio == i
        gmask = jnp.where(pick, 1.0, gmask)
        cur = jnp.where(pick, neg, cur)
    emask = jnp.concatenate(
        [jnp.broadcast_to(gmask[g:g + 1], (GROUP_SIZE, tm)) for g in range(N_GROUPS)], 0)
    cur = jnp.where(emask > 0.5, sel, neg)
    eio = lax.broadcasted_iota(I32, (N_EXPERTS, tm), 0).astype(F32)
    member = jnp.zeros((N_EXPERTS, tm), F32)
    idxs, scs = [], []
    for _ in range(TOP_K):
        m = jnp.max(cur, 0, keepdims=True)
        i = jnp.min(jnp.where(cur == m, eio, float(N_EXPERTS)), 0, keepdims=True)
        pick = eio == i
        idxs.append(i)
        scs.append(jnp.sum(jnp.where(pick, scores, 0.0), 0, keepdims=True))
        member = jnp.where(pick, 1.0, member)
        cur = jnp.where(pick, neg, cur)
    idx = jnp.concatenate(idxs, 0)
    sc = jnp.concatenate(scs, 0)
    eidx_ref[...] = idx.astype(I32)
    gw_ref[...] = sc / jnp.sum(sc, 0, keepdims=True) * ROUTE_SCALE
    before = jnp.dot(member.astype(BF16), tri_ref[...], preferred_element_type=F32) + run_ref[:, 0:1]
    ranks = [jnp.sum(jnp.where(eio == idxs[k], before, 0.0), 0, keepdims=True) for k in range(TOP_K)]
    rank_ref[...] = jnp.concatenate(ranks, 0).astype(I32)
    run_ref[...] = run_ref[...] + jnp.sum(member, 1, keepdims=True)
    cnt_ref[...] = run_ref[...]


def _router(x, w_router, bias, tm):
    t = x.shape[0]
    tri = jnp.asarray(np.arange(tm)[:, None] < np.arange(tm)[None, :], BF16)
    kt = lambda dt: jax.ShapeDtypeStruct((TOP_K, t), dt)
    return pl.pallas_call(
        _router_kernel, grid=(t // tm,),
        in_specs=[pl.BlockSpec((tm, D_MODEL), lambda i: (i, 0)),
                  pl.BlockSpec((N_EXPERTS, D_MODEL), lambda i: (0, 0)),
                  pl.BlockSpec((N_EXPERTS, LANE), lambda i: (0, 0)),
                  pl.BlockSpec((tm, tm), lambda i: (0, 0))],
        out_specs=[pl.BlockSpec((TOP_K, tm), lambda i: (0, i)),
                   pl.BlockSpec((TOP_K, tm), lambda i: (0, i)),
                   pl.BlockSpec((TOP_K, tm), lambda i: (0, i)),
                   pl.BlockSpec((N_EXPERTS, LANE), lambda i: (0, 0))],
        out_shape=[kt(I32), kt(F32), kt(I32), jax.ShapeDtypeStruct((N_EXPERTS, LANE), F32)],
        scratch_shapes=[pltpu.VMEM((N_EXPERTS, LANE), F32)],
        compiler_params=_cparams("arbitrary"), name="moe_router")(
            x, w_router.T.astype(BF16), jnp.broadcast_to(bias[:, None], (N_EXPERTS, LANE)), tri)


def _dispatch_kernel(pos_ref, x_ref, xs_in_ref, xs_ref, sem):
    del xs_in_ref
    tm = x_ref.shape[0]

    def copy(r, k):
        return pltpu.make_async_copy(x_ref.at[pl.ds(r, 1)], xs_ref.at[pl.ds(pos_ref[0, 0, k * tm + r], 1)], sem)

    def start(r, carry):
        for k in range(TOP_K):
            copy(r, k).start()
        return carry

    def wait(r, carry):
        for k in range(TOP_K):
            copy(r, k).wait()
        return carry

    lax.fori_loop(0, tm, start, 0)
    lax.fori_loop(0, tm, wait, 0)


def _dispatch(xp, pos_tiles, nslot, tm):
    t = xp.shape[0]
    zeros = jnp.zeros((nslot, HALF_W), xp.dtype)
    return pl.pallas_call(
        _dispatch_kernel, grid=(t // tm,),
        in_specs=[pl.BlockSpec((1, 1, TOP_K * tm), lambda i: (i, 0, 0), memory_space=pltpu.SMEM),
                  pl.BlockSpec((tm, HALF_W), lambda i: (i, 0)),
                  pl.BlockSpec(memory_space=pl.ANY)],
        out_specs=pl.BlockSpec(memory_space=pl.ANY),
        out_shape=jax.ShapeDtypeStruct((nslot, HALF_W), xp.dtype),
        scratch_shapes=[pltpu.SemaphoreType.DMA(())],
        input_output_aliases={2: 0},
        compiler_params=_cparams("arbitrary"), name="moe_dispatch")(pos_tiles, xp, zeros)


def _expert_kernel(be_ref, nu_ref, x_ref, wg_ref, wu_ref, wd_ref, o_ref, *, packed):
    del be_ref

    @pl.when(pl.program_id(0) < nu_ref[0])
    def _():
        if packed:
            lo, hi = _unpack_halves(x_ref[...])
            lo = lo.astype(BF16)
            hi = hi.astype(BF16)
            hg = (jnp.dot(lo, wg_ref[0, 0:HALF_W, :], preferred_element_type=F32)
                  + jnp.dot(hi, wg_ref[0, HALF_W:D_MODEL, :], preferred_element_type=F32))
            hu = (jnp.dot(lo, wu_ref[0, 0:HALF_W, :], preferred_element_type=F32)
                  + jnp.dot(hi, wu_ref[0, HALF_W:D_MODEL, :], preferred_element_type=F32))
        else:
            x = x_ref[...]
            hg = jnp.dot(x, wg_ref[0], preferred_element_type=F32)
            hu = jnp.dot(x, wu_ref[0], preferred_element_type=F32)
        hb = (jax.nn.silu(hg) * hu).astype(BF16)
        out = jnp.dot(hb, wd_ref[0], preferred_element_type=F32)
        o_ref[...] = _pack_halves(out) if packed else out.astype(BF16)

    @pl.when(pl.program_id(0) >= nu_ref[0])
    def _():
        o_ref[...] = jnp.zeros_like(o_ref)


def _expert_ffn(xs, blk_e, nused, wg, wu, wd, bm, packed):
    nslot, width = xs.shape
    nb = nslot // bm
    de = wg.shape[-1]
    gs = pltpu.PrefetchScalarGridSpec(
        num_scalar_prefetch=2, grid=(nb,),
        in_specs=[pl.BlockSpec((bm, width), lambda b, be, nu: (b, 0)),
                  pl.BlockSpec((1, D_MODEL, de), lambda b, be, nu: (be[b], 0, 0)),
                  pl.BlockSpec((1, D_MODEL, de), lambda b, be, nu: (be[b], 0, 0)),
                  pl.BlockSpec((1, de, D_MODEL), lambda b, be, nu: (be[b], 0, 0))],
        out_specs=pl.BlockSpec((bm, width), lambda b, be, nu: (b, 0)))
    return pl.pallas_call(
        functools.partial(_expert_kernel, packed=packed), grid_spec=gs,
        out_shape=jax.ShapeDtypeStruct((nslot, width), xs.dtype),
        compiler_params=_cparams("arbitrary"), name="moe_expert_ffn")(blk_e, nused, xs, wg, wu, wd)


def _combine_kernel(pos_ref, gw_ref, x_ref, sh_ref, eo_ref, g_ref, b_ref, o_ref, ob_ref, op_ref, buf, sem):
    tm = x_ref.shape[0]

    def copy(r, k):
        return pltpu.make_async_copy(eo_ref.at[pl.ds(pos_ref[0, 0, k * tm + r], 1)],
                                     buf.at[k, pl.ds(r, 1)], sem)

    def start(r, carry):
        for k in range(TOP_K):
            copy(r, k).start()
        return carry

    def wait(r, carry):
        for k in range(TOP_K):
            copy(r, k).wait()
        return carry

    lax.fori_loop(0, tm, start, 0)
    lax.fori_loop(0, tm, wait, 0)
    sh = sh_ref[...].astype(F32)
    lo_acc = sh[:, :HALF_W]
    hi_acc = sh[:, HALF_W:]
    for k in range(TOP_K):
        lo, hi = _unpack_halves(buf[k])
        wk = gw_ref[:, k:k + 1]
        lo_acc = lo_acc + lo * wk
        hi_acc = hi_acc + hi * wk
    ffn = jnp.concatenate([lo_acc, hi_acc], 1)
    _emit_x(_layer_norm(ALPHA * x_ref[...] + ffn, g_ref[...], b_ref[...]), o_ref, ob_ref, op_ref)


def _combine_ln(pos_tiles, gw_t, x, sh, eo, g, b, tm):
    t = x.shape[0]
    row = lambda width: pl.BlockSpec((tm, width), lambda i: (i, 0))
    out_specs, out_shape = _x_out(t, tm)
    return pl.pallas_call(
        _combine_kernel, grid=(t // tm,),
        in_specs=[pl.BlockSpec((1, 1, TOP_K * tm), lambda i: (i, 0, 0), memory_space=pltpu.SMEM),
                  row(TOP_K), row(D_MODEL), row(D_MODEL),
                  pl.BlockSpec(memory_space=pl.ANY),
                  pl.BlockSpec((1, D_MODEL), lambda i: (0, 0)),
                  pl.BlockSpec((1, D_MODEL), lambda i: (0, 0))],
        out_specs=out_specs, out_shape=out_shape,
        scratch_shapes=[pltpu.VMEM((TOP_K, tm, HALF_W), U32), pltpu.SemaphoreType.DMA(())],
        compiler_params=_cparams("arbitrary"), name="moe_combine_ln")(
            pos_tiles, gw_t, x, sh, eo, g.reshape(1, D_MODEL), b.reshape(1, D_MODEL))


def _moe_ln(x, xb, xp, w_router, bias, wg, wu, wd, wsg, wsu, wsd, g, b, tm, bm):
    t = x.shape[0]
    eidx, gw, rank, cnt = _router(xb, w_router, bias, tm)
    counts = cnt[:, 0].astype(I32)
    padded = (counts + bm - 1) // bm * bm
    pad_end = jnp.cumsum(padded)
    pad_start = pad_end - padded
    pos = pad_start[eidx] + rank
    nb = -(-(t * TOP_K + N_EXPERTS * (bm - 1)) // bm)
    blk_e = jnp.minimum(jnp.searchsorted(pad_end, jnp.arange(nb, dtype=I32) * bm, side='right'),
                        N_EXPERTS - 1).astype(I32)
    nused = (pad_end[-1] // bm).astype(I32).reshape(1)
    pos_tiles = pos.reshape(TOP_K, t // tm, tm).transpose(1, 0, 2).reshape(t // tm, 1, TOP_K * tm)
    xs = _dispatch(xp, pos_tiles, nb * bm, tm)
    eo = _expert_ffn(xs, blk_e, nused, wg, wu, wd, bm, True)
    shared = _expert_ffn(xb, jnp.zeros((t // tm,), I32), jnp.full((1,), t // tm, I32),
                         wsg[None], wsu[None], wsd[None], tm, False)
    return _combine_ln(pos_tiles, gw.T, x, shared, eo, g, b, tm)


def _odd_w_in(w):
    sizes = (H_C * K_C, H_C * K_C, H_C * V_C, GLA_RANK, H_C * V_C, HALF_W, CONV_DIM, H_D)
    q, k, v, lr, r, z, xbc, dt = jnp.split(w, [int(s) for s in np.cumsum(sizes)[:-1]], axis=-1)
    pad = jnp.zeros((w.shape[0], OD_XBC - (OD_LRDT + GLA_RANK + H_D)), w.dtype)
    return jnp.concatenate([q, k, v, r, z, lr, dt, pad, xbc], -1)


def _trunk(x3, hgrn0, gla0, ssm0, conv0, p, cfg):
    bsz, seq_len, _ = x3.shape
    t = bsz * seq_len
    x = x3.reshape(t, D_MODEL)
    xb = None
    tm, rows, bm = cfg["tm"], cfg["rows"], cfg["bm"]
    outs = {}
    for l in range(DEPTH):
        j = l // 2
        if l % 2 == 0:
            proj = _proj(x if xb is None else xb, p["ev_w_in"][j], cfg["ptm"], cfg["ptn"])
            ya, v_rows = _mixer_a(proj, p["ev_a_ln_g"][j], p["ev_a_ln_b"][j], p["ev_a_ws"][j],
                                  p["ev_a_bs"][j], seq_len, cfg["arows"], cfg["emit_v"])
            yb, s_b = _mixer_b(proj, p["hgrn_lb_logits"], p["ev_b_norm_g"][j], hgrn0[j], seq_len, rows, l)
            outs["v_rows"] = v_rows
            outs["hgrn"] = s_b
            w_out = p["ev_w_out"][j]
        else:
            proj = _proj(x if xb is None else xb, p["od_w_in"][j], cfg["ptm"], cfg["ptn"])
            ya, s_c = _mixer_c(proj, p["od_c_gate_w2"][j], p["od_c_gate_b"][j], p["od_c_norm_g"][j],
                               gla0[j], seq_len, rows)
            yb, s_d = _mixer_d(proj, p["od_d_conv_w"][j], p["od_d_conv_b"][j], p["od_d_dt_bias"][j],
                               p["od_d_a_log"][j], p["od_d_skip"][j], p["od_d_norm_g"][j],
                               ssm0[j], conv0[j], seq_len, rows)
            xbc = proj.reshape(bsz, seq_len, OD_PAD)[:, :, OD_XBC:OD_XBC + CONV_DIM]
            outs["conv"] = jnp.concatenate([conv0[j], xbc], 1)[:, -(D_CONV - 1):]
            outs["gla"] = s_c
            outs["ssm"] = s_d
            w_out = p["od_w_out"][j]
        x, xb, xp = _out_proj_ln(ya, yb, x, w_out, p["ln1_g"][l], p["ln1_b"][l], tm)
        x, xb, _ = _moe_ln(x, xb, xp, p["moe_w_router"][l], p["moe_router_bias"][l], p["moe_w_gate"][l],
                           p["moe_w_up"][l], p["moe_w_down"][l], p["moe_ws_gate"][l], p["moe_ws_up"][l],
                           p["moe_ws_down"][l], p["ln2_g"][l], p["ln2_b"][l], tm, bm)
    return x.reshape(bsz, seq_len, D_MODEL), outs


def _config(bsz, seq_len):
    t = bsz * seq_len
    if seq_len % CHUNK == 0:
        return dict(tm=256, rows=min(128, seq_len), arows=min(256, seq_len), bm=512,
                    ptm=min(1024, t), ptn=512, emit_v=False)
    return dict(tm=t, rows=seq_len, arows=t, bm=128, ptm=t, ptn=512, emit_v=True)


def kernel(x_prompt, x_sample, state_b_hgrn, state_c_gla, state_d_ssm, state_d_conv, ev_w_in, ev_a_ln_g, ev_a_ln_b, ev_a_ws, ev_a_bs, ev_b_norm_g, ev_w_out, hgrn_lb_logits, od_w_in, od_c_gate_w2, od_c_gate_b, od_c_norm_g, od_d_conv_w, od_d_conv_b, od_d_dt_bias, od_d_a_log, od_d_skip, od_d_norm_g, od_w_out, ln1_g, ln1_b, ln2_g, ln2_b, moe_w_router, moe_router_bias, moe_w_gate, moe_w_up, moe_w_down, moe_ws_gate, moe_ws_up, moe_ws_down):
    p = dict(
        ev_w_in=ev_w_in.astype(BF16), ev_a_ln_g=ev_a_ln_g, ev_a_ln_b=ev_a_ln_b, ev_a_ws=ev_a_ws,
        ev_a_bs=ev_a_bs, ev_b_norm_g=ev_b_norm_g, ev_w_out=ev_w_out.astype(BF16),
        hgrn_lb_logits=hgrn_lb_logits,
        od_w_in=jnp.stack([_odd_w_in(od_w_in[j]) for j in range(od_w_in.shape[0])]).astype(BF16),
        od_c_gate_w2=od_c_gate_w2, od_c_gate_b=od_c_gate_b, od_c_norm_g=od_c_norm_g,
        od_d_conv_w=od_d_conv_w, od_d_conv_b=od_d_conv_b, od_d_dt_bias=od_d_dt_bias,
        od_d_a_log=od_d_a_log, od_d_skip=od_d_skip, od_d_norm_g=od_d_norm_g,
        od_w_out=od_w_out.astype(BF16), ln1_g=ln1_g, ln1_b=ln1_b, ln2_g=ln2_g, ln2_b=ln2_b,
        moe_w_router=moe_w_router, moe_router_bias=moe_router_bias,
        moe_w_gate=moe_w_gate.astype(BF16), moe_w_up=moe_w_up.astype(BF16),
        moe_w_down=moe_w_down.astype(BF16), moe_ws_gate=moe_ws_gate.astype(BF16),
        moe_ws_up=moe_ws_up.astype(BF16), moe_ws_down=moe_ws_down.astype(BF16))
    bp, lp, _ = x_prompt.shape
    n_even = state_b_hgrn.shape[0]
    n_odd = state_c_gla.shape[0]
    zeros = lambda n, *s: jnp.zeros((n, bp) + s, F32)
    y_p, o_p = _trunk(x_prompt, zeros(n_even, H_B, K_B, V_B), zeros(n_odd, H_C, K_C, V_C),
                      zeros(n_odd, H_D, N_D, P_D), zeros(n_odd, D_CONV - 1, CONV_DIM), p,
                      _config(bp, lp))
    bs, ls, _ = x_sample.shape
    y_s, o_s = _trunk(x_sample, state_b_hgrn, state_c_gla, state_d_ssm, state_d_conv, p,
                      _config(bs, ls))
    a_v = o_s["v_rows"].reshape(1, bs, ls, HALF_W)
    return (y_p, y_s, a_v, o_p["hgrn"][None], o_s["hgrn"][None], o_p["gla"][None], o_s["gla"][None],
            o_p["ssm"][None], o_s["ssm"][None], o_p["conv"][None], o_s["conv"][None])
```

```python
import functools
import math

import jax
import jax.numpy as jnp
import numpy as np
from jax import lax
from jax.experimental import pallas as pl
from jax.experimental.pallas import tpu as pltpu

F32 = jnp.float32
BF16 = jnp.bfloat16
I32 = jnp.int32

D_MODEL = 2048
DEPTH = 2
CHUNK = 64
SUB = 16
HALF_W = D_MODEL // 2
A_CHUNK = 128
H_A = 4
DA = HALF_W // H_A
H_B = 8
K_B = 128
V_B = HALF_W // H_B
H_C = 4
V_C = HALF_W // H_C
K_C = V_C // 2
GLA_RANK = 16
GLA_TAU = 16.0
P_D = 64
H_D = HALF_W // P_D
G_D = 2
HPG_D = H_D // G_D
N_D = 128
D_CONV = 4
CONV_DIM = HALF_W + 2 * G_D * N_D
N_EXPERTS = 64
N_GROUPS = 8
GROUP_SIZE = N_EXPERTS // N_GROUPS
TOPK_GROUPS = 4
TOP_K = 8
D_EXPERT = 512
ROUTE_SCALE = 2.5
ALPHA = (2 * DEPTH) ** 0.25
EPS = 1e-5
LANE = 128
VMEM_LIMIT = 56 * 1024 * 1024

OD_Q, OD_K, OD_V, OD_R, OD_Z, OD_LRDT, OD_XBC = 0, 512, 1024, 2048, 3072, 4096, 4608
OD_PAD = 6144
DT_OFF = GLA_RANK


def _cparams(*sem):
    return pltpu.CompilerParams(dimension_semantics=sem, vmem_limit_bytes=VMEM_LIMIT)


def _split3(x):
    hi = x.astype(BF16)
    r = x - hi.astype(F32)
    mid = r.astype(BF16)
    lo = (r - mid.astype(F32)).astype(BF16)
    return hi, mid, lo


def _split2(x):
    hi = x.astype(BF16)
    return hi, (x - hi.astype(F32)).astype(BF16)


def _dot_exact_l(a_bf16, x):
    return sum(jnp.dot(a_bf16, p, preferred_element_type=F32) for p in _split3(x))


def _dot_exact_r(x, b_bf16):
    return sum(jnp.dot(p, b_bf16, preferred_element_type=F32) for p in _split3(x))


def _dot_nt(a, b):
    return lax.dot_general(a, b, (((1,), (1,)), ((), ())), preferred_element_type=F32)


def _softplus(x):
    return jnp.maximum(x, 0.0) + jnp.log1p(jnp.exp(-jnp.abs(x)))


def _proj_kernel(x_ref, w_ref, o_ref, *scratch):
    if scratch:
        xb_ref, = scratch

        @pl.when(pl.program_id(1) == 0)
        def _():
            xb_ref[...] = x_ref[...].astype(BF16)
        x = xb_ref[...]
    else:
        x = x_ref[...]
    o_ref[...] = jnp.dot(x, w_ref[...], preferred_element_type=F32)


def _proj(x, w, tm, tn):
    t, k = x.shape
    n = w.shape[1]
    scratch = [] if x.dtype == BF16 else [pltpu.VMEM((tm, k), BF16)]
    return pl.pallas_call(
        _proj_kernel, grid=(t // tm, n // tn),
        in_specs=[pl.BlockSpec((tm, k), lambda i, j: (i, 0)),
                  pl.BlockSpec((k, tn), lambda i, j: (0, j))],
        out_specs=pl.BlockSpec((tm, tn), lambda i, j: (i, j)),
        out_shape=jax.ShapeDtypeStruct((t, n), F32),
        scratch_shapes=scratch, compiler_params=_cparams("parallel", "arbitrary"),
        name="in_proj")(x, w)


def _mixa_kernel(u_ref, v_ref, lng_ref, lnb_ref, w_ref, bs_ref, ya_ref, *vrows, ca):
    gu = jax.nn.gelu(u_ref[...])
    gv = jax.nn.gelu(v_ref[...])
    rows = gu.shape[0]
    for h in range(H_A):
        sl = slice(h * DA, (h + 1) * DA)
        vh = gv[:, sl]
        mu = jnp.mean(vh, -1, keepdims=True)
        d = vh - mu
        var = jnp.mean(d * d, -1, keepdims=True)
        vn = d * lax.rsqrt(var + EPS) * lng_ref[h] + lnb_ref[h]
        if vrows:
            vrows[0][:, sl] = vn
        vnb = vn.astype(BF16)
        for c in range(rows // ca):
            rs = slice(c * ca, (c + 1) * ca)
            s = jnp.dot(w_ref[h], vnb[rs], preferred_element_type=F32) + bs_ref[h]
            ya_ref[rs, sl] = (gu[rs, sl] * s).astype(BF16)


def _mixer_a(proj, lng, lnb, ws, bs, seq_len, rows, emit_v):
    t = proj.shape[0]
    ca = min(A_CHUNK, seq_len)
    pos = np.arange(A_CHUNK)
    mask = (pos[None, :] // CHUNK) <= (pos[:, None] // CHUNK)
    w = (ws * mask)[:, :ca, :ca].astype(BF16)
    bsb = jnp.broadcast_to(bs[:, :ca, None], (H_A, ca, DA)).astype(F32)
    nb = HALF_W // HALF_W
    out_shape = [jax.ShapeDtypeStruct((t, HALF_W), BF16)]
    out_specs = [pl.BlockSpec((rows, HALF_W), lambda i: (i, 0))]
    if emit_v:
        out_shape.append(jax.ShapeDtypeStruct((t, HALF_W), F32))
        out_specs.append(pl.BlockSpec((rows, HALF_W), lambda i: (i, 0)))
    del nb
    res = pl.pallas_call(
        functools.partial(_mixa_kernel, ca=ca), grid=(t // rows,),
        in_specs=[pl.BlockSpec((rows, HALF_W), lambda i: (i, 0)),
                  pl.BlockSpec((rows, HALF_W), lambda i: (i, 1)),
                  pl.BlockSpec((H_A, 1, DA), lambda i: (0, 0, 0)),
                  pl.BlockSpec((H_A, 1, DA), lambda i: (0, 0, 0)),
                  pl.BlockSpec((H_A, ca, ca), lambda i: (0, 0, 0)),
                  pl.BlockSpec((H_A, ca, DA), lambda i: (0, 0, 0))],
        out_specs=out_specs, out_shape=out_shape,
        compiler_params=_cparams("parallel"), name="mixer_a")(
            proj, proj, lng[:, None, :], lnb[:, None, :], w, bsb)
    return res if emit_v else (res[0], None)


def _scan_mats(c):
    i = np.arange(c)[:, None]
    j = np.arange(c)[None, :]
    tril = (j <= i)
    local = tril & ((i // SUB) == (j // SUB))
    ones = np.ones((c, c), bool)
    return jnp.asarray(np.concatenate([local, tril, ones], 0), BF16)


def _gla_head(q, k, v, lc, cum, last, s_ref, h, c):
    ns = c // SUB
    q_hi, q_lo = _split2(q * jnp.exp(lc))
    pre = cum - lc
    vb = v.astype(BF16)
    parts = []
    for blk in range(ns):
        n = SUB * (blk + 1)
        r0 = SUB * blk
        k_hi, k_lo = _split2(k[:n] * jnp.exp(pre[r0:r0 + 1] - cum[:n]))
        rq = slice(r0, r0 + SUB)
        att = _dot_nt(q_hi[rq], k_hi) + (_dot_nt(q_hi[rq], k_lo) + _dot_nt(q_lo[rq], k_hi))
        row = lax.broadcasted_iota(I32, (SUB, n), 0)
        col = lax.broadcasted_iota(I32, (SUB, n), 1)
        att = jnp.where(col - r0 <= row, att, 0.0)
        parts.append(jnp.dot(att.astype(BF16), vb[:n], preferred_element_type=F32))
    o = parts[0] if ns == 1 else jnp.concatenate(parts, 0)
    st = s_ref[h]
    o = o + _dot_nt((q * jnp.exp(cum)).astype(BF16), st.astype(BF16))
    kc = (k * jnp.exp(last - cum)).astype(BF16)
    upd = jnp.dot(v.T.astype(BF16), kc, preferred_element_type=F32)
    s_ref[h] = st * jnp.exp(last[0:1]) + upd
    return o


def _hgrn_kernel(q_ref, f_ref, i_ref, g_ref, lbl_ref, ng_ref, s0_ref, cm_ref, y_ref, so_ref, s_ref,
                 *, c, layer):
    step = pl.program_id(1)

    @pl.when(step == 0)
    def _():
        s_ref[...] = s0_ref[0]

    lg = lbl_ref[...]
    ex = jnp.exp(lg - jnp.max(lg, 0, keepdims=True))
    sm = ex / jnp.sum(ex, 0, keepdims=True)
    lb = jnp.sum(sm[:layer + 1], 0, keepdims=True)
    rows = q_ref.shape[0]
    for ci in range(rows // c):
        rs = slice(ci * c, (ci + 1) * c)
        f = lb + (1.0 - lb) * jax.nn.sigmoid(f_ref[rs])
        g = jnp.log(f)
        k = 1.0 - f
        q = jax.nn.silu(q_ref[rs])
        v = i_ref[rs]
        sc = _dot_exact_l(cm_ref[...], g)
        for h in range(H_B):
            hs = slice(h * K_B, (h + 1) * K_B)
            o = _gla_head(q[:, hs], k[:, hs], v[:, hs], sc[0:c, hs], sc[c:2 * c, hs],
                          sc[2 * c:3 * c, hs], s_ref, h, c)
            ms = jnp.mean(o * o, -1, keepdims=True)
            y = o * lax.rsqrt(ms + EPS) * ng_ref[:, hs] * jax.nn.sigmoid(g_ref[rs, hs])
            y_ref[rs, hs] = y.astype(BF16)

    @pl.when(step == pl.num_programs(1) - 1)
    def _():
        so_ref[0] = s_ref[...]


def _mixer_b(proj, lb_logits, norm_g, s0, seq_len, rows, layer):
    t = proj.shape[0]
    bsz = t // seq_len
    c = CHUNK if seq_len % CHUNK == 0 else seq_len
    spb = seq_len // rows
    s0t = jnp.swapaxes(s0, -1, -2)
    col = lambda j: pl.BlockSpec((rows, HALF_W), lambda b, s, j=j: (b * spb + s, j))
    nl = lb_logits.shape[0]
    y, st = pl.pallas_call(
        functools.partial(_hgrn_kernel, c=c, layer=layer), grid=(bsz, spb),
        in_specs=[col(2), col(3), col(4), col(5),
                  pl.BlockSpec((nl, HALF_W), lambda b, s: (0, 0)),
                  pl.BlockSpec((1, HALF_W), lambda b, s: (0, 0)),
                  pl.BlockSpec((1, H_B, V_B, K_B), lambda b, s: (b, 0, 0, 0)),
                  pl.BlockSpec((3 * c, c), lambda b, s: (0, 0))],
        out_specs=[pl.BlockSpec((rows, HALF_W), lambda b, s: (b * spb + s, 0)),
                   pl.BlockSpec((1, H_B, V_B, K_B), lambda b, s: (b, 0, 0, 0))],
        out_shape=[jax.ShapeDtypeStruct((t, HALF_W), BF16),
                   jax.ShapeDtypeStruct((bsz, H_B, V_B, K_B), F32)],
        scratch_shapes=[pltpu.VMEM((H_B, V_B, K_B), F32)],
        compiler_params=_cparams("parallel", "arbitrary"), name="mixer_b_hgrn")(
            proj, proj, proj, proj, lb_logits, norm_g.reshape(1, HALF_W), s0t, _scan_mats(c))
    return y, jnp.swapaxes(st, -1, -2)


def _glac_kernel(q_ref, k_ref, v_ref, r_ref, lrdt_ref, w2_ref, gb_ref, ng_ref, s0_ref, cm_ref,
                 y_ref, so_ref, s_ref, *, c):
    step = pl.program_id(1)

    @pl.when(step == 0)
    def _():
        s_ref[...] = s0_ref[0]

    rows = q_ref.shape[0]
    for ci in range(rows // c):
        rs = slice(ci * c, (ci + 1) * c)
        z = jnp.dot(lrdt_ref[rs], w2_ref[...], preferred_element_type=F32,
                    precision=lax.Precision.HIGHEST) + gb_ref[...]
        g = -_softplus(-z) / GLA_TAU
        q = q_ref[rs] * (K_C ** -0.5)
        k = k_ref[rs]
        v = v_ref[rs]
        sc = _dot_exact_l(cm_ref[...], g)
        for h in range(H_C):
            ks = slice(h * K_C, (h + 1) * K_C)
            vs = slice(h * V_C, (h + 1) * V_C)
            o = _gla_head(q[:, ks], k[:, ks], v[:, vs], sc[0:c, ks], sc[c:2 * c, ks],
                          sc[2 * c:3 * c, ks], s_ref, h, c)
            ms = jnp.mean(o * o, -1, keepdims=True)
            y = o * lax.rsqrt(ms + EPS) * ng_ref[:, vs] * jax.nn.silu(r_ref[rs, vs])
            y_ref[rs, vs] = y.astype(BF16)

    @pl.when(step == pl.num_programs(1) - 1)
    def _():
        so_ref[0] = s_ref[...]


def _mixer_c(proj, gate_w2, gate_b, norm_g, s0, seq_len, rows):
    t = proj.shape[0]
    bsz = t // seq_len
    c = CHUNK if seq_len % CHUNK == 0 else seq_len
    spb = seq_len // rows
    s0t = jnp.swapaxes(s0, -1, -2)
    hk = H_C * K_C

    def col(width, off):
        return pl.BlockSpec((rows, width), lambda b, s: (b * spb + s, off // width))

    y, st = pl.pallas_call(
        functools.partial(_glac_kernel, c=c), grid=(bsz, spb),
        in_specs=[col(hk, OD_Q), col(hk, OD_K), col(HALF_W, OD_V), col(HALF_W, OD_R),
                  col(LANE, OD_LRDT),
                  pl.BlockSpec((LANE, hk), lambda b, s: (0, 0)),
                  pl.BlockSpec((1, hk), lambda b, s: (0, 0)),
                  pl.BlockSpec((1, HALF_W), lambda b, s: (0, 0)),
                  pl.BlockSpec((1, H_C, V_C, K_C), lambda b, s: (b, 0, 0, 0)),
                  pl.BlockSpec((3 * c, c), lambda b, s: (0, 0))],
        out_specs=[pl.BlockSpec((rows, HALF_W), lambda b, s: (b * spb + s, 0)),
                   pl.BlockSpec((1, H_C, V_C, K_C), lambda b, s: (b, 0, 0, 0))],
        out_shape=[jax.ShapeDtypeStruct((t, HALF_W), BF16),
                   jax.ShapeDtypeStruct((bsz, H_C, V_C, K_C), F32)],
        scratch_shapes=[pltpu.VMEM((H_C, V_C, K_C), F32)],
        compiler_params=_cparams("parallel", "arbitrary"), name="mixer_c_gla")(
            proj, proj, proj, proj, proj,
            jnp.pad(gate_w2, ((0, LANE - GLA_RANK), (0, 0))),
            gate_b.reshape(1, hk),
            norm_g.reshape(1, HALF_W), s0t, _scan_mats(c))
    return y, jnp.swapaxes(st, -1, -2)


GW = HPG_D * P_D


def _ssd_kernel(z_ref, xbc_ref, lrdt_ref, cw_ref, cb_ref, dtb_ref, alog_ref, skip_ref, ng_ref,
                ex_ref, eye_ref, cm_ref, conv0_ref, s0_ref, y_ref, so_ref, s_ref, tail_ref, *, c):
    step = pl.program_id(1)

    @pl.when(step == 0)
    def _():
        s_ref[...] = s0_ref[0]
        tail_ref[...] = conv0_ref[0]

    x = xbc_ref[...]
    rows = x.shape[0]
    tail = tail_ref[...]
    sub8 = lax.broadcasted_iota(I32, (8, CONV_DIM), 0)
    conv = x * cw_ref[D_CONV - 1:D_CONV]
    for sh in range(1, D_CONV):
        rolled = pltpu.roll(x, sh, 0)
        head = jnp.where(sub8 < sh, pltpu.roll(tail, sh, 0), rolled[0:8])
        xk = jnp.concatenate([head, rolled[8:]], 0) if rows > 8 else head
        conv = conv + xk * cw_ref[D_CONV - 1 - sh:D_CONV - sh]
    tail_ref[...] = x[rows - 8:rows]
    xc = jax.nn.silu(conv + cb_ref[...])
    a = -jnp.exp(alog_ref[...])
    ex = ex_ref[...]
    tri = lax.broadcasted_iota(I32, (c, c), 1) <= lax.broadcasted_iota(I32, (c, c), 0)
    for ci in range(rows // c):
        rs = slice(ci * c, (ci + 1) * c)
        xs = xc[rs, 0:HALF_W]
        bm = xc[rs, HALF_W:HALF_W + G_D * N_D]
        cmat = xc[rs, HALF_W + G_D * N_D:CONV_DIM]
        dt = _softplus(lrdt_ref[rs] + dtb_ref[...])
        la = dt * a
        sc = _dot_exact_l(cm_ref[...], la)
        cum = sc[0:c]
        scx = _dot_exact_r(sc, ex)
        cumx = scx[0:c]
        lastx = scx[c:2 * c]
        dtx = _dot_exact_r(dt, ex)
        xdt = xs * dtx
        cum_t = sum(_dot_nt(eye_ref[...], p) for p in _split3(cum))
        e_in = jnp.exp(cumx)
        xw = (xdt * jnp.exp(lastx - cumx)).astype(BF16)
        e_last = jnp.exp(lastx[0:1])
        xdtb = xdt.astype(BF16)
        ys = []
        for g in range(G_D):
            gs = slice(g * GW, (g + 1) * GW)
            ns = slice(g * N_D, (g + 1) * N_D)
            cg = cmat[:, ns].astype(BF16)
            bg = bm[:, ns]
            cb = _dot_nt(cg, bg.astype(BF16))
            parts = []
            for hp in range(HPG_D):
                h = g * HPG_D + hp
                hl = DT_OFF + h
                diff = cum[:, hl:hl + 1] - cum_t[hl:hl + 1, :]
                dec = jnp.where(tri, jnp.exp(jnp.minimum(diff, 0.0)), 0.0)
                wmat = (cb * dec).astype(BF16)
                parts.append(jnp.dot(wmat, xdtb[:, h * P_D:(h + 1) * P_D],
                                     preferred_element_type=F32))
            y_intra = jnp.concatenate(parts, 1)
            sg = s_ref[g]
            y_inter = jnp.dot(cg, sg.astype(BF16), preferred_element_type=F32) * e_in[:, gs]
            upd = jnp.dot(bg.T.astype(BF16), xw[:, gs], preferred_element_type=F32)
            s_ref[g] = sg * e_last[:, gs] + upd
            y = y_intra + y_inter + xs[:, gs] * skip_ref[:, gs]
            yg = y * jax.nn.silu(z_ref[rs, gs])
            ms = jnp.mean(yg * yg, -1, keepdims=True)
            ys.append(yg * lax.rsqrt(ms + EPS) * ng_ref[:, gs])
        y_ref[rs, :] = jnp.concatenate(ys, 1).astype(BF16)

    @pl.when(step == pl.num_programs(1) - 1)
    def _():
        so_ref[0] = s_ref[...]


def _mixer_d(proj, conv_w, conv_b, dt_bias, a_log, skip, norm_g, s0, conv0, seq_len, rows):
    t = proj.shape[0]
    bsz = t // seq_len
    c = CHUNK if seq_len % CHUNK == 0 else seq_len
    spb = seq_len // rows
    s0g = s0.reshape(bsz, G_D, HPG_D, N_D, P_D).transpose(0, 1, 3, 2, 4).reshape(bsz, G_D, N_D, GW)
    conv0p = jnp.pad(conv0, ((0, 0), (8 - (D_CONV - 1), 0), (0, 0)))
    expand_np = np.zeros((LANE, HALF_W), np.float32)
    expand_np[DT_OFF:DT_OFF + H_D] = np.repeat(np.eye(H_D), P_D, axis=1)
    expand = jnp.asarray(expand_np, BF16)
    eye = jnp.asarray(np.eye(LANE), BF16)
    i = np.arange(c)[:, None]
    j = np.arange(c)[None, :]
    cm = jnp.asarray(np.concatenate([j <= i, np.ones((c, c), bool)], 0), BF16)
    skipx = jnp.repeat(skip, P_D)[None, :]
    lane_pad = lambda v: jnp.pad(v.reshape(1, H_D), ((0, 0), (DT_OFF, LANE - DT_OFF - H_D)))

    def col(width, off):
        return pl.BlockSpec((rows, width), lambda b, s: (b * spb + s, off // width))

    def full(shape):
        return pl.BlockSpec(shape, lambda b, s: (0,) * len(shape))

    y, st = pl.pallas_call(
        functools.partial(_ssd_kernel, c=c), grid=(bsz, spb),
        in_specs=[col(HALF_W, OD_Z),
                  col(CONV_DIM, OD_XBC),
                  col(LANE, OD_LRDT),
                  full((D_CONV, CONV_DIM)), full((1, CONV_DIM)), full((1, LANE)), full((1, LANE)),
                  full((1, HALF_W)), full((1, HALF_W)), full((LANE, HALF_W)), full((LANE, LANE)),
                  full((2 * c, c)),
                  pl.BlockSpec((1, 8, CONV_DIM), lambda b, s: (b, 0, 0)),
                  pl.BlockSpec((1, G_D, N_D, GW), lambda b, s: (b, 0, 0, 0))],
        out_specs=[pl.BlockSpec((rows, HALF_W), lambda b, s: (b * spb + s, 0)),
                   pl.BlockSpec((1, G_D, N_D, GW), lambda b, s: (b, 0, 0, 0))],
        out_shape=[jax.ShapeDtypeStruct((t, HALF_W), BF16),
                   jax.ShapeDtypeStruct((bsz, G_D, N_D, GW), F32)],
        scratch_shapes=[pltpu.VMEM((G_D, N_D, GW), F32), pltpu.VMEM((8, CONV_DIM), F32)],
        compiler_params=_cparams("parallel", "arbitrary"), name="mixer_d_ssd")(
            proj, proj, proj, conv_w, conv_b.reshape(1, CONV_DIM), lane_pad(dt_bias),
            lane_pad(a_log), skipx, norm_g.reshape(1, HALF_W), expand, eye, cm, conv0p, s0g)
    st = st.reshape(bsz, G_D, N_D, HPG_D, P_D).transpose(0, 1, 3, 2, 4).reshape(bsz, H_D, N_D, P_D)
    return y, st


def _layer_norm(hpre, g, b):
    mu = jnp.mean(hpre, -1, keepdims=True)
    d = hpre - mu
    var = jnp.mean(d * d, -1, keepdims=True)
    return d * lax.rsqrt(var + EPS) * g + b


U32 = jnp.uint32
HI_MASK = 0xFFFF0000


def _pack_halves(x):
    half = x.shape[1] // 2
    lo = lax.bitcast_convert_type(x[:, :half].astype(BF16).astype(F32), U32) >> 16
    hi = lax.bitcast_convert_type(x[:, half:].astype(BF16).astype(F32), U32) & U32(HI_MASK)
    return lo | hi


def _unpack_halves(w):
    return (lax.bitcast_convert_type(w << 16, F32), lax.bitcast_convert_type(w & U32(HI_MASK), F32))


def _emit_x(xn, o_ref, ob_ref, op_ref):
    o_ref[...] = xn
    ob_ref[...] = xn.astype(BF16)
    op_ref[...] = _pack_halves(xn)


def _x_out(t, tm):
    row = lambda width: pl.BlockSpec((tm, width), lambda i, *_: (i, 0))
    return ([row(D_MODEL), row(D_MODEL), row(HALF_W)],
            [jax.ShapeDtypeStruct((t, D_MODEL), F32), jax.ShapeDtypeStruct((t, D_MODEL), BF16),
             jax.ShapeDtypeStruct((t, HALF_W), U32)])


def _outproj_kernel(ya_ref, yb_ref, x_ref, w_ref, g_ref, b_ref, o_ref, ob_ref, op_ref):
    acc = jnp.dot(ya_ref[...], w_ref[0:HALF_W, :], preferred_element_type=F32)
    acc = acc + jnp.dot(yb_ref[...], w_ref[HALF_W:D_MODEL, :], preferred_element_type=F32)
    _emit_x(_layer_norm(ALPHA * x_ref[...] + acc, g_ref[...], b_ref[...]), o_ref, ob_ref, op_ref)


def _out_proj_ln(ya, yb, x, w, g, b, tm):
    t = x.shape[0]
    row = lambda width: pl.BlockSpec((tm, width), lambda i: (i, 0))
    out_specs, out_shape = _x_out(t, tm)
    return pl.pallas_call(
        _outproj_kernel, grid=(t // tm,),
        in_specs=[row(HALF_W), row(HALF_W), row(D_MODEL),
                  pl.BlockSpec((D_MODEL, D_MODEL), lambda i: (0, 0)),
                  pl.BlockSpec((1, D_MODEL), lambda i: (0, 0)),
                  pl.BlockSpec((1, D_MODEL), lambda i: (0, 0))],
        out_specs=out_specs, out_shape=out_shape,
        compiler_params=_cparams("parallel"), name="out_proj_ln")(
            ya, yb, x, w, g.reshape(1, D_MODEL), b.reshape(1, D_MODEL))


RANK_BITS = 20
RANK_MASK = (1 << RANK_BITS) - 1


def _router_kernel(x_ref, wr_ref, bias_ref, tri_ref, code_ref, gw_ref, cnt_ref, run_ref):
    step = pl.program_id(0)

    @pl.when(step == 0)
    def _():
        run_ref[...] = jnp.zeros_like(run_ref)

    tm = x_ref.shape[0]
    logits = _dot_nt(wr_ref[...], x_ref[...])
    scores = jax.nn.sigmoid(logits)
    sel = scores + bias_ref[:, 0:1]
    neg = -jnp.inf
    sub = lax.broadcasted_iota(I32, (GROUP_SIZE, tm), 0).astype(F32)
    gsc = []
    for g in range(N_GROUPS):
        blk = sel[g * GROUP_SIZE:(g + 1) * GROUP_SIZE]
        m1 = jnp.max(blk, 0, keepdims=True)
        i1 = jnp.min(jnp.where(blk == m1, sub, float(GROUP_SIZE)), 0, keepdims=True)
        m2 = jnp.max(jnp.where(sub == i1, neg, blk), 0, keepdims=True)
        gsc.append(m1 + m2)
    cur = jnp.concatenate(gsc, 0)
    gio = lax.broadcasted_iota(I32, (N_GROUPS, tm), 0).astype(F32)
    gmask = jnp.zeros((N_GROUPS, tm), F32)
    for _ in range(TOPK_GROUPS):
        m = jnp.max(cur, 0, keepdims=True)
        i = jnp.min(jnp.where(cur == m, gio, float(N_GROUPS)), 0, keepdims=True)
        pick = gio == i
        gmask = jnp.where(pick, 1.0, gmask)
        cur = jnp.where(pick, neg, cur)
    emask = jnp.concatenate(
        [jnp.broadcast_to(gmask[g:g + 1], (GROUP_SIZE, tm)) for g in range(N_GROUPS)], 0)
    cur = jnp.where(emask > 0.5, sel, neg)
    eio = lax.broadcasted_iota(I32, (N_EXPERTS, tm), 0).astype(F32)
    member = jnp.zeros((N_EXPERTS, tm), F32)
    idxs, scs = [], []
    for _ in range(TOP_K):
        m = jnp.max(cur, 0, keepdims=True)
        i = jnp.min(jnp.where(cur == m, eio, float(N_EXPERTS)), 0, keepdims=True)
        pick = eio == i
        idxs.append(i)
        scs.append(jnp.sum(jnp.where(pick, scores, 0.0), 0, keepdims=True))
        member = jnp.where(pick, 1.0, member)
        cur = jnp.where(pick, neg, cur)
    idx = jnp.concatenate(idxs, 0)
    sc = jnp.concatenate(scs, 0)
    gw_ref[...] = sc / jnp.sum(sc, 0, keepdims=True) * ROUTE_SCALE
    before = jnp.dot(member.astype(BF16), tri_ref[...], preferred_element_type=F32) + run_ref[:, 0:1]
    ranks = [jnp.sum(jnp.where(eio == idxs[k], before, 0.0), 0, keepdims=True) for k in range(TOP_K)]
    rank = jnp.concatenate(ranks, 0).astype(I32)
    code_ref[...] = idx.astype(I32) * (1 << RANK_BITS) + rank
    run_ref[...] = run_ref[...] + jnp.sum(member, 1, keepdims=True)
    cnt_ref[...] = run_ref[...]


def _router(x, w_router, bias, tm):
    t = x.shape[0]
    assert t * TOP_K < (1 << RANK_BITS)
    tri = jnp.asarray(np.arange(tm)[:, None] < np.arange(tm)[None, :], BF16)
    kt = lambda dt: jax.ShapeDtypeStruct((TOP_K, t), dt)
    return pl.pallas_call(
        _router_kernel, grid=(t // tm,),
        in_specs=[pl.BlockSpec((tm, D_MODEL), lambda i: (i, 0)),
                  pl.BlockSpec((N_EXPERTS, D_MODEL), lambda i: (0, 0)),
                  pl.BlockSpec((N_EXPERTS, LANE), lambda i: (0, 0)),
                  pl.BlockSpec((tm, tm), lambda i: (0, 0))],
        out_specs=[pl.BlockSpec((TOP_K, tm), lambda i: (0, i)),
                   pl.BlockSpec((TOP_K, tm), lambda i: (0, i)),
                   pl.BlockSpec((N_EXPERTS, LANE), lambda i: (0, 0))],
        out_shape=[kt(I32), kt(F32), jax.ShapeDtypeStruct((N_EXPERTS, LANE), F32)],
        scratch_shapes=[pltpu.VMEM((N_EXPERTS, LANE), F32)],
        compiler_params=_cparams("arbitrary"), name="moe_router")(
            x, w_router.T.astype(BF16), jnp.broadcast_to(bias[:, None], (N_EXPERTS, LANE)), tri)


def _slot(code, start_ref):
    return start_ref[lax.shift_right_logical(code, RANK_BITS)] + (code & RANK_MASK)


def _dispatch_kernel(start_ref, fill_ref, code_ref, x_ref, xs_ref, zrow_ref, sem, *, eps):
    step = pl.program_id(0)
    tm = x_ref.shape[0]
    zrow_ref[...] = jnp.zeros_like(zrow_ref)

    def copy(r, k):
        return pltpu.make_async_copy(x_ref.at[pl.ds(r, 1)],
                                     xs_ref.at[pl.ds(_slot(code_ref[0, 0, k * tm + r], start_ref), 1)], sem)

    def zero_copy(row):
        return pltpu.make_async_copy(zrow_ref.at[pl.ds(0, 1)], xs_ref.at[pl.ds(row, 1)], sem)

    def start(r, carry):
        for k in range(TOP_K):
            copy(r, k).start()
        return carry

    def wait(r, carry):
        for k in range(TOP_K):
            copy(r, k).wait()
        return carry

    def fill_expert(e, total):
        base = fill_ref[0, e]
        n = fill_ref[1, e]

        def go(r, c):
            zero_copy(base + r).start()
            return c

        lax.fori_loop(0, n, go, 0)
        return total + n

    def fill_wait(r, c):
        zero_copy(0).wait()
        return c

    lax.fori_loop(0, tm, start, 0)
    e0 = jnp.minimum(step * eps, N_EXPERTS)
    e1 = jnp.minimum(e0 + eps, N_EXPERTS)
    n_fill = lax.fori_loop(e0, e1, fill_expert, 0)
    lax.fori_loop(0, tm, wait, 0)
    lax.fori_loop(0, n_fill, fill_wait, 0)


def _dispatch(xp, code_tiles, slot_start, fill, nslot, tm):
    t = xp.shape[0]
    steps = t // tm
    eps = -(-N_EXPERTS // steps)
    gs = pltpu.PrefetchScalarGridSpec(
        num_scalar_prefetch=2, grid=(steps,),
        in_specs=[pl.BlockSpec((1, 1, TOP_K * tm), lambda i, *_: (i, 0, 0), memory_space=pltpu.SMEM),
                  pl.BlockSpec((tm, HALF_W), lambda i, *_: (i, 0))],
        out_specs=pl.BlockSpec(memory_space=pl.ANY),
        scratch_shapes=[pltpu.VMEM((8, HALF_W), xp.dtype), pltpu.SemaphoreType.DMA(())])
    return pl.pallas_call(
        functools.partial(_dispatch_kernel, eps=eps), grid_spec=gs,
        out_shape=jax.ShapeDtypeStruct((nslot, HALF_W), xp.dtype),
        compiler_params=_cparams("arbitrary"), name="moe_dispatch")(slot_start, fill, code_tiles, xp)


def _expert_kernel(be_ref, nu_ref, x_ref, wg_ref, wu_ref, wd_ref, o_ref, *, packed):
    del be_ref

    @pl.when(pl.program_id(0) < nu_ref[0])
    def _():
        if packed:
            lo, hi = _unpack_halves(x_ref[...])
            lo = lo.astype(BF16)
            hi = hi.astype(BF16)
            hg = (jnp.dot(lo, wg_ref[0, 0:HALF_W, :], preferred_element_type=F32)
                  + jnp.dot(hi, wg_ref[0, HALF_W:D_MODEL, :], preferred_element_type=F32))
            hu = (jnp.dot(lo, wu_ref[0, 0:HALF_W, :], preferred_element_type=F32)
                  + jnp.dot(hi, wu_ref[0, HALF_W:D_MODEL, :], preferred_element_type=F32))
        else:
            x = x_ref[...]
            hg = jnp.dot(x, wg_ref[0], preferred_element_type=F32)
            hu = jnp.dot(x, wu_ref[0], preferred_element_type=F32)
        hb = (jax.nn.silu(hg) * hu).astype(BF16)
        out = jnp.dot(hb, wd_ref[0], preferred_element_type=F32)
        o_ref[...] = _pack_halves(out) if packed else out.astype(BF16)

    @pl.when(pl.program_id(0) >= nu_ref[0])
    def _():
        o_ref[...] = jnp.zeros_like(o_ref)


def _expert_ffn(xs, blk_e, nused, wg, wu, wd, bm, packed):
    nslot, width = xs.shape
    nb = nslot // bm
    de = wg.shape[-1]
    gs = pltpu.PrefetchScalarGridSpec(
        num_scalar_prefetch=2, grid=(nb,),
        in_specs=[pl.BlockSpec((bm, width), lambda b, be, nu: (jnp.minimum(b, nu[0] - 1), 0)),
                  pl.BlockSpec((1, D_MODEL, de), lambda b, be, nu: (be[b], 0, 0)),
                  pl.BlockSpec((1, D_MODEL, de), lambda b, be, nu: (be[b], 0, 0)),
                  pl.BlockSpec((1, de, D_MODEL), lambda b, be, nu: (be[b], 0, 0))],
        out_specs=pl.BlockSpec((bm, width), lambda b, be, nu: (b, 0)))
    return pl.pallas_call(
        functools.partial(_expert_kernel, packed=packed), grid_spec=gs,
        out_shape=jax.ShapeDtypeStruct((nslot, width), xs.dtype),
        compiler_params=_cparams("arbitrary"), name="moe_expert_ffn")(blk_e, nused, xs, wg, wu, wd)


def _combine_kernel(start_ref, code_ref, gw_ref, x_ref, sh_ref, eo_ref, g_ref, b_ref, o_ref, ob_ref, op_ref,
                    buf, sem):
    tm = x_ref.shape[0]

    def copy(r, k):
        return pltpu.make_async_copy(eo_ref.at[pl.ds(_slot(code_ref[0, 0, k * tm + r], start_ref), 1)],
                                     buf.at[k, pl.ds(r, 1)], sem)

    def start(r, carry):
        for k in range(TOP_K):
            copy(r, k).start()
        return carry

    def wait(r, carry):
        for k in range(TOP_K):
            copy(r, k).wait()
        return carry

    lax.fori_loop(0, tm, start, 0)
    lax.fori_loop(0, tm, wait, 0)
    sh = sh_ref[...].astype(F32)
    lo_acc = sh[:, :HALF_W]
    hi_acc = sh[:, HALF_W:]
    for k in range(TOP_K):
        lo, hi = _unpack_halves(buf[k])
        wk = gw_ref[:, k:k + 1]
        lo_acc = lo_acc + lo * wk
        hi_acc = hi_acc + hi * wk
    ffn = jnp.concatenate([lo_acc, hi_acc], 1)
    _emit_x(_layer_norm(ALPHA * x_ref[...] + ffn, g_ref[...], b_ref[...]), o_ref, ob_ref, op_ref)


def _combine_ln(code_tiles, slot_start, gw_t, x, sh, eo, g, b, tm):
    t = x.shape[0]
    row = lambda width: pl.BlockSpec((tm, width), lambda i, *_: (i, 0))
    const = pl.BlockSpec((1, D_MODEL), lambda i, *_: (0, 0))
    out_specs, out_shape = _x_out(t, tm)
    gs = pltpu.PrefetchScalarGridSpec(
        num_scalar_prefetch=1, grid=(t // tm,),
        in_specs=[pl.BlockSpec((1, 1, TOP_K * tm), lambda i, *_: (i, 0, 0), memory_space=pltpu.SMEM),
                  row(TOP_K), row(D_MODEL), row(D_MODEL), pl.BlockSpec(memory_space=pl.ANY), const, const],
        out_specs=out_specs,
        scratch_shapes=[pltpu.VMEM((TOP_K, tm, HALF_W), U32), pltpu.SemaphoreType.DMA(())])
    return pl.pallas_call(
        _combine_kernel, grid_spec=gs, out_shape=out_shape,
        compiler_params=_cparams("arbitrary"), name="moe_combine_ln")(
            slot_start, code_tiles, gw_t, x, sh, eo, g.reshape(1, D_MODEL), b.reshape(1, D_MODEL))


def _moe_ln(x, xb, xp, w_router, bias, wg, wu, wd, wsg, wsu, wsd, g, b, tm, bm):
    t = x.shape[0]
    code, gw, cnt = _router(xb, w_router, bias, tm)
    counts = cnt[:, 0].astype(I32)
    padded = (counts + bm - 1) // bm * bm
    pad_end = jnp.cumsum(padded)
    slot_start = pad_end - padded
    fill = jnp.stack([slot_start + counts, padded - counts])
    nb = -(-(t * TOP_K + N_EXPERTS * (bm - 1)) // bm)
    blk_first = jnp.arange(nb, dtype=I32) * bm
    blk_e = jnp.minimum(jnp.sum((pad_end[None, :] <= blk_first[:, None]).astype(I32), 1), N_EXPERTS - 1)
    nused = (pad_end[-1] // bm).astype(I32).reshape(1)
    code_tiles = code.reshape(TOP_K, t // tm, tm).transpose(1, 0, 2).reshape(t // tm, 1, TOP_K * tm)
    xs = _dispatch(xp, code_tiles, slot_start, fill, nb * bm, tm)
    eo = _expert_ffn(xs, blk_e, nused, wg, wu, wd, bm, True)
    shared = _expert_ffn(xb, jnp.zeros((t // tm,), I32), jnp.full((1,), t // tm, I32),
                         wsg[None], wsu[None], wsd[None], tm, False)
    return _combine_ln(code_tiles, slot_start, gw.T, x, shared, eo, g, b, tm)


def _odd_w_in(w):
    sizes = (H_C * K_C, H_C * K_C, H_C * V_C, GLA_RANK, H_C * V_C, HALF_W, CONV_DIM, H_D)
    q, k, v, lr, r, z, xbc, dt = jnp.split(w, [int(s) for s in np.cumsum(sizes)[:-1]], axis=-1)
    pad = jnp.zeros((w.shape[0], OD_XBC - (OD_LRDT + GLA_RANK + H_D)), w.dtype)
    return jnp.concatenate([q, k, v, r, z, lr, dt, pad, xbc], -1)


def _trunk(x3, hgrn0, gla0, ssm0, conv0, p, cfg):
    bsz, seq_len, _ = x3.shape
    t = bsz * seq_len
    x = x3.reshape(t, D_MODEL)
    xb = None
    tm, rows, bm = cfg["tm"], cfg["rows"], cfg["bm"]
    outs = {}
    for l in range(DEPTH):
        j = l // 2
        if l % 2 == 0:
            proj = _proj(x if xb is None else xb, p["ev_w_in"][j], cfg["ptm"], cfg["ptn"])
            ya, v_rows = _mixer_a(proj, p["ev_a_ln_g"][j], p["ev_a_ln_b"][j], p["ev_a_ws"][j],
                                  p["ev_a_bs"][j], seq_len, cfg["arows"], cfg["emit_v"])
            yb, s_b = _mixer_b(proj, p["hgrn_lb_logits"], p["ev_b_norm_g"][j], hgrn0[j], seq_len, rows, l)
            outs["v_rows"] = v_rows
            outs["hgrn"] = s_b
            w_out = p["ev_w_out"][j]
        else:
            proj = _proj(x if xb is None else xb, p["od_w_in"][j], cfg["ptm"], cfg["ptn"])
            ya, s_c = _mixer_c(proj, p["od_c_gate_w2"][j], p["od_c_gate_b"][j], p["od_c_norm_g"][j],
                               gla0[j], seq_len, rows)
            yb, s_d = _mixer_d(proj, p["od_d_conv_w"][j], p["od_d_conv_b"][j], p["od_d_dt_bias"][j],
                               p["od_d_a_log"][j], p["od_d_skip"][j], p["od_d_norm_g"][j],
                               ssm0[j], conv0[j], seq_len, rows)
            keep = min(D_CONV - 1, seq_len)
            xbc = proj.reshape(bsz, seq_len, OD_PAD)[:, seq_len - keep:, OD_XBC:OD_XBC + CONV_DIM]
            outs["conv"] = jnp.concatenate([conv0[j], xbc], 1)[:, -(D_CONV - 1):]
            outs["gla"] = s_c
            outs["ssm"] = s_d
            w_out = p["od_w_out"][j]
        x, xb, xp = _out_proj_ln(ya, yb, x, w_out, p["ln1_g"][l], p["ln1_b"][l], tm)
        x, xb, _ = _moe_ln(x, xb, xp, p["moe_w_router"][l], p["moe_router_bias"][l], p["moe_w_gate"][l],
                           p["moe_w_up"][l], p["moe_w_down"][l], p["moe_ws_gate"][l], p["moe_ws_up"][l],
                           p["moe_ws_down"][l], p["ln2_g"][l], p["ln2_b"][l], tm, bm)
    return x.reshape(bsz, seq_len, D_MODEL), outs


def _config(bsz, seq_len):
    t = bsz * seq_len
    if seq_len % CHUNK == 0:
        return dict(tm=256, rows=min(128, seq_len), arows=min(256, seq_len), bm=512,
                    ptm=min(1024, t), ptn=512, emit_v=False)
    return dict(tm=t, rows=seq_len, arows=t, bm=128, ptm=t, ptn=512, emit_v=True)


def kernel(x_prompt, x_sample, state_b_hgrn, state_c_gla, state_d_ssm, state_d_conv, ev_w_in, ev_a_ln_g, ev_a_ln_b, ev_a_ws, ev_a_bs, ev_b_norm_g, ev_w_out, hgrn_lb_logits, od_w_in, od_c_gate_w2, od_c_gate_b, od_c_norm_g, od_d_conv_w, od_d_conv_b, od_d_dt_bias, od_d_a_log, od_d_skip, od_d_norm_g, od_w_out, ln1_g, ln1_b, ln2_g, ln2_b, moe_w_router, moe_router_bias, moe_w_gate, moe_w_up, moe_w_down, moe_ws_gate, moe_ws_up, moe_ws_down):
    p = dict(
        ev_w_in=ev_w_in.astype(BF16), ev_a_ln_g=ev_a_ln_g, ev_a_ln_b=ev_a_ln_b, ev_a_ws=ev_a_ws,
        ev_a_bs=ev_a_bs, ev_b_norm_g=ev_b_norm_g, ev_w_out=ev_w_out.astype(BF16),
        hgrn_lb_logits=hgrn_lb_logits,
        od_w_in=jnp.stack([_odd_w_in(od_w_in[j]) for j in range(od_w_in.shape[0])]).astype(BF16),
        od_c_gate_w2=od_c_gate_w2, od_c_gate_b=od_c_gate_b, od_c_norm_g=od_c_norm_g,
        od_d_conv_w=od_d_conv_w, od_d_conv_b=od_d_conv_b, od_d_dt_bias=od_d_dt_bias,
        od_d_a_log=od_d_a_log, od_d_skip=od_d_skip, od_d_norm_g=od_d_norm_g,
        od_w_out=od_w_out.astype(BF16), ln1_g=ln1_g, ln1_b=ln1_b, ln2_g=ln2_g, ln2_b=ln2_b,
        moe_w_router=moe_w_router, moe_router_bias=moe_router_bias,
        moe_w_gate=moe_w_gate.astype(BF16), moe_w_up=moe_w_up.astype(BF16),
        moe_w_down=moe_w_down.astype(BF16), moe_ws_gate=moe_ws_gate.astype(BF16),
        moe_ws_up=moe_ws_up.astype(BF16), moe_ws_down=moe_ws_down.astype(BF16))
    bp, lp, _ = x_prompt.shape
    n_even = state_b_hgrn.shape[0]
    n_odd = state_c_gla.shape[0]
    zeros = lambda n, *s: jnp.zeros((n, bp) + s, F32)
    y_p, o_p = _trunk(x_prompt, zeros(n_even, H_B, K_B, V_B), zeros(n_odd, H_C, K_C, V_C),
                      zeros(n_odd, H_D, N_D, P_D), zeros(n_odd, D_CONV - 1, CONV_DIM), p,
                      _config(bp, lp))
    bs, ls, _ = x_sample.shape
    y_s, o_s = _trunk(x_sample, state_b_hgrn, state_c_gla, state_d_ssm, state_d_conv, p,
                      _config(bs, ls))
    a_v = o_s["v_rows"].reshape(1, bs, ls, HALF_W)
    return (y_p, y_s, a_v, o_p["hgrn"][None], o_s["hgrn"][None], o_p["gla"][None], o_s["gla"][None],
            o_p["ssm"][None], o_s["ssm"][None], o_p["conv"][None], o_s["conv"][None])
```

```python
import functools
import math

import jax
import jax.numpy as jnp
import numpy as np
from jax import lax
from jax.experimental import pallas as pl
from jax.experimental.pallas import tpu as pltpu

F32 = jnp.float32
BF16 = jnp.bfloat16
I32 = jnp.int32

D_MODEL = 2048
DEPTH = 2
CHUNK = 64
SUB = 16
HALF_W = D_MODEL // 2
A_CHUNK = 128
H_A = 4
DA = HALF_W // H_A
H_B = 8
K_B = 128
V_B = HALF_W // H_B
H_C = 4
V_C = HALF_W // H_C
K_C = V_C // 2
GLA_RANK = 16
GLA_TAU = 16.0
P_D = 64
H_D = HALF_W // P_D
G_D = 2
HPG_D = H_D // G_D
N_D = 128
D_CONV = 4
CONV_DIM = HALF_W + 2 * G_D * N_D
N_EXPERTS = 64
N_GROUPS = 8
GROUP_SIZE = N_EXPERTS // N_GROUPS
TOPK_GROUPS = 4
TOP_K = 8
D_EXPERT = 512
ROUTE_SCALE = 2.5
ALPHA = (2 * DEPTH) ** 0.25
EPS = 1e-5
LANE = 128
VMEM_LIMIT = 56 * 1024 * 1024

OD_Q, OD_K, OD_V, OD_R, OD_Z, OD_LRDT, OD_XBC = 0, 512, 1024, 2048, 3072, 4096, 4608
OD_PAD = 6144
DT_OFF = GLA_RANK


def _cparams(*sem):
    return pltpu.CompilerParams(dimension_semantics=sem, vmem_limit_bytes=VMEM_LIMIT)


def _split3(x):
    hi = x.astype(BF16)
    r = x - hi.astype(F32)
    mid = r.astype(BF16)
    lo = (r - mid.astype(F32)).astype(BF16)
    return hi, mid, lo


def _split2(x):
    hi = x.astype(BF16)
    return hi, (x - hi.astype(F32)).astype(BF16)


def _dot_exact_l(a_bf16, x):
    return sum(jnp.dot(a_bf16, p, preferred_element_type=F32) for p in _split3(x))


def _dot_exact_r(x, b_bf16):
    return sum(jnp.dot(p, b_bf16, preferred_element_type=F32) for p in _split3(x))


def _dot_nt(a, b):
    return lax.dot_general(a, b, (((1,), (1,)), ((), ())), preferred_element_type=F32)


def _softplus(x):
    return jnp.maximum(x, 0.0) + jnp.log1p(jnp.exp(-jnp.abs(x)))


def _proj_kernel(x_ref, w_ref, o_ref, *scratch):
    if scratch:
        xb_ref, = scratch

        @pl.when(pl.program_id(1) == 0)
        def _():
            xb_ref[...] = x_ref[...].astype(BF16)
        x = xb_ref[...]
    else:
        x = x_ref[...]
    o_ref[...] = jnp.dot(x, w_ref[...], preferred_element_type=F32)


def _proj(x, w, tm, tn):
    t, k = x.shape
    n = w.shape[1]
    scratch = [] if x.dtype == BF16 else [pltpu.VMEM((tm, k), BF16)]
    return pl.pallas_call(
        _proj_kernel, grid=(t // tm, n // tn),
        in_specs=[pl.BlockSpec((tm, k), lambda i, j: (i, 0)),
                  pl.BlockSpec((k, tn), lambda i, j: (0, j))],
        out_specs=pl.BlockSpec((tm, tn), lambda i, j: (i, j)),
        out_shape=jax.ShapeDtypeStruct((t, n), F32),
        scratch_shapes=scratch, compiler_params=_cparams("parallel", "arbitrary"),
        name="in_proj")(x, w)


def _mixa_kernel(u_ref, v_ref, lng_ref, lnb_ref, w_ref, bs_ref, ya_ref, *vrows, ca):
    gu = jax.nn.gelu(u_ref[...])
    gv = jax.nn.gelu(v_ref[...])
    rows = gu.shape[0]
    for h in range(H_A):
        sl = slice(h * DA, (h + 1) * DA)
        vh = gv[:, sl]
        mu = jnp.mean(vh, -1, keepdims=True)
        d = vh - mu
        var = jnp.mean(d * d, -1, keepdims=True)
        vn = d * lax.rsqrt(var + EPS) * lng_ref[h] + lnb_ref[h]
        if vrows:
            vrows[0][:, sl] = vn
        vnb = vn.astype(BF16)
        for c in range(rows // ca):
            rs = slice(c * ca, (c + 1) * ca)
            s = jnp.dot(w_ref[h], vnb[rs], preferred_element_type=F32) + bs_ref[h]
            ya_ref[rs, sl] = (gu[rs, sl] * s).astype(BF16)


def _mixer_a(proj, lng, lnb, ws, bs, seq_len, rows, emit_v):
    t = proj.shape[0]
    ca = min(A_CHUNK, seq_len)
    pos = np.arange(A_CHUNK)
    mask = (pos[None, :] // CHUNK) <= (pos[:, None] // CHUNK)
    w = (ws * mask)[:, :ca, :ca].astype(BF16)
    bsb = jnp.broadcast_to(bs[:, :ca, None], (H_A, ca, DA)).astype(F32)
    nb = HALF_W // HALF_W
    out_shape = [jax.ShapeDtypeStruct((t, HALF_W), BF16)]
    out_specs = [pl.BlockSpec((rows, HALF_W), lambda i: (i, 0))]
    if emit_v:
        out_shape.append(jax.ShapeDtypeStruct((t, HALF_W), F32))
        out_specs.append(pl.BlockSpec((rows, HALF_W), lambda i: (i, 0)))
    del nb
    res = pl.pallas_call(
        functools.partial(_mixa_kernel, ca=ca), grid=(t // rows,),
        in_specs=[pl.BlockSpec((rows, HALF_W), lambda i: (i, 0)),
                  pl.BlockSpec((rows, HALF_W), lambda i: (i, 1)),
                  pl.BlockSpec((H_A, 1, DA), lambda i: (0, 0, 0)),
                  pl.BlockSpec((H_A, 1, DA), lambda i: (0, 0, 0)),
                  pl.BlockSpec((H_A, ca, ca), lambda i: (0, 0, 0)),
                  pl.BlockSpec((H_A, ca, DA), lambda i: (0, 0, 0))],
        out_specs=out_specs, out_shape=out_shape,
        compiler_params=_cparams("parallel"), name="mixer_a")(
            proj, proj, lng[:, None, :], lnb[:, None, :], w, bsb)
    return res if emit_v else (res[0], None)


def _scan_mats(c):
    i = np.arange(c)[:, None]
    j = np.arange(c)[None, :]
    tril = (j <= i)
    local = tril & ((i // SUB) == (j // SUB))
    ones = np.ones((c, c), bool)
    return jnp.asarray(np.concatenate([local, tril, ones], 0), BF16)


def _cat(parts, axis):
    return parts[0] if len(parts) == 1 else jnp.concatenate(parts, axis)


def _gla_intra(q, k, v, sc, c, nh, kd, vd):
    lc, cum, last = sc[0:c], sc[c:2 * c], sc[2 * c:3 * c]
    ns = c // SUB
    q_hi, q_lo = _split2(q * jnp.exp(lc))
    pre = cum - lc
    vb = v.astype(BF16)
    att = []
    for blk in range(ns):
        n = SUB * (blk + 1)
        r0 = SUB * blk
        k_hi, k_lo = _split2(k[:n] * jnp.exp(pre[r0:r0 + 1] - cum[:n]))
        rq = slice(r0, r0 + SUB)
        for h in range(nh):
            ks = slice(h * kd, (h + 1) * kd)
            att.append(_dot_nt(q_hi[rq, ks], k_hi[:, ks])
                       + (_dot_nt(q_hi[rq, ks], k_lo[:, ks]) + _dot_nt(q_lo[rq, ks], k_hi[:, ks])))
    attb = []
    for blk in range(ns):
        n = SUB * (blk + 1)
        keep = (lax.broadcasted_iota(I32, (SUB, n), 1) - SUB * blk) <= lax.broadcasted_iota(I32, (SUB, n), 0)
        attb += [jnp.where(keep, att[blk * nh + h], 0.0).astype(BF16) for h in range(nh)]
    rows_out = []
    for blk in range(ns):
        n = SUB * (blk + 1)
        rows_out.append(_cat([jnp.dot(attb[blk * nh + h], vb[:n, h * vd:(h + 1) * vd],
                                      preferred_element_type=F32) for h in range(nh)], 1))
    qc = (q * jnp.exp(cum)).astype(BF16)
    kc = (k * jnp.exp(last - cum)).astype(BF16)
    return _cat(rows_out, 0), qc, kc, v.T.astype(BF16), jnp.exp(last[0:1])


def _gla_state(intra, s_ref, nh, kd, vd):
    o_intra, qc, kc, vt, e_last = intra
    sts = [s_ref[h] for h in range(nh)]
    inter = [_dot_nt(qc[:, h * kd:(h + 1) * kd], sts[h].astype(BF16)) for h in range(nh)]
    upd = [jnp.dot(vt[h * vd:(h + 1) * vd], kc[:, h * kd:(h + 1) * kd], preferred_element_type=F32)
           for h in range(nh)]
    for h in range(nh):
        s_ref[h] = sts[h] * e_last[:, h * kd:(h + 1) * kd] + upd[h]
    return o_intra + _cat(inter, 1)


def _hgrn_kernel(q_ref, f_ref, i_ref, g_ref, lbl_ref, ng_ref, s0_ref, cm_ref, y_ref, so_ref, s_ref,
                 *, c, layer):
    step = pl.program_id(1)

    @pl.when(step == 0)
    def _():
        s_ref[...] = s0_ref[0]

    lg = lbl_ref[...]
    ex = jnp.exp(lg - jnp.max(lg, 0, keepdims=True))
    sm = ex / jnp.sum(ex, 0, keepdims=True)
    lb = jnp.sum(sm[:layer + 1], 0, keepdims=True)
    rows = q_ref.shape[0]
    intra = []
    for ci in range(rows // c):
        rs = slice(ci * c, (ci + 1) * c)
        f = lb + (1.0 - lb) * jax.nn.sigmoid(f_ref[rs])
        g = jnp.log(f)
        sc = _dot_exact_l(cm_ref[...], g)
        intra.append(_gla_intra(jax.nn.silu(q_ref[rs]), 1.0 - f, i_ref[rs], sc, c, H_B, K_B, V_B))
    for ci in range(rows // c):
        rs = slice(ci * c, (ci + 1) * c)
        o = _gla_state(intra[ci], s_ref, H_B, K_B, V_B)
        rn = _cat([jnp.broadcast_to(lax.rsqrt(jnp.mean(jnp.square(o[:, h * V_B:(h + 1) * V_B]), -1,
                                                        keepdims=True) + EPS), (c, V_B))
                   for h in range(H_B)], 1)
        y_ref[rs, :] = (o * rn * ng_ref[...] * jax.nn.sigmoid(g_ref[rs])).astype(BF16)

    @pl.when(step == pl.num_programs(1) - 1)
    def _():
        so_ref[0] = s_ref[...]


def _mixer_b(proj, lb_logits, norm_g, s0, seq_len, rows, layer):
    t = proj.shape[0]
    bsz = t // seq_len
    c = CHUNK if seq_len % CHUNK == 0 else seq_len
    spb = seq_len // rows
    s0t = jnp.swapaxes(s0, -1, -2)
    col = lambda j: pl.BlockSpec((rows, HALF_W), lambda b, s, j=j: (b * spb + s, j))
    nl = lb_logits.shape[0]
    y, st = pl.pallas_call(
        functools.partial(_hgrn_kernel, c=c, layer=layer), grid=(bsz, spb),
        in_specs=[col(2), col(3), col(4), col(5),
                  pl.BlockSpec((nl, HALF_W), lambda b, s: (0, 0)),
                  pl.BlockSpec((1, HALF_W), lambda b, s: (0, 0)),
                  pl.BlockSpec((1, H_B, V_B, K_B), lambda b, s: (b, 0, 0, 0)),
                  pl.BlockSpec((3 * c, c), lambda b, s: (0, 0))],
        out_specs=[pl.BlockSpec((rows, HALF_W), lambda b, s: (b * spb + s, 0)),
                   pl.BlockSpec((1, H_B, V_B, K_B), lambda b, s: (b, 0, 0, 0))],
        out_shape=[jax.ShapeDtypeStruct((t, HALF_W), BF16),
                   jax.ShapeDtypeStruct((bsz, H_B, V_B, K_B), F32)],
        scratch_shapes=[pltpu.VMEM((H_B, V_B, K_B), F32)],
        compiler_params=_cparams("parallel", "arbitrary"), name="mixer_b_hgrn")(
            proj, proj, proj, proj, lb_logits, norm_g.reshape(1, HALF_W), s0t, _scan_mats(c))
    return y, jnp.swapaxes(st, -1, -2)


def _glac_kernel(q_ref, k_ref, v_ref, r_ref, lrdt_ref, w2_ref, gb_ref, ng_ref, s0_ref, cm_ref,
                 y_ref, so_ref, s_ref, *, c):
    step = pl.program_id(1)

    @pl.when(step == 0)
    def _():
        s_ref[...] = s0_ref[0]

    rows = q_ref.shape[0]
    intra = []
    for ci in range(rows // c):
        rs = slice(ci * c, (ci + 1) * c)
        z = jnp.dot(lrdt_ref[rs], w2_ref[...], preferred_element_type=F32,
                    precision=lax.Precision.HIGHEST) + gb_ref[...]
        g = -_softplus(-z) / GLA_TAU
        sc = _dot_exact_l(cm_ref[...], g)
        intra.append(_gla_intra(q_ref[rs] * (K_C ** -0.5), k_ref[rs], v_ref[rs], sc, c, H_C, K_C, V_C))
    for ci in range(rows // c):
        rs = slice(ci * c, (ci + 1) * c)
        o = _gla_state(intra[ci], s_ref, H_C, K_C, V_C)
        rn = _cat([jnp.broadcast_to(lax.rsqrt(jnp.mean(jnp.square(o[:, h * V_C:(h + 1) * V_C]), -1,
                                                        keepdims=True) + EPS), (c, V_C))
                   for h in range(H_C)], 1)
        y_ref[rs, :] = (o * rn * ng_ref[...] * jax.nn.silu(r_ref[rs])).astype(BF16)

    @pl.when(step == pl.num_programs(1) - 1)
    def _():
        so_ref[0] = s_ref[...]


def _mixer_c(proj, gate_w2, gate_b, norm_g, s0, seq_len, rows):
    t = proj.shape[0]
    bsz = t // seq_len
    c = CHUNK if seq_len % CHUNK == 0 else seq_len
    spb = seq_len // rows
    s0t = jnp.swapaxes(s0, -1, -2)
    hk = H_C * K_C

    def col(width, off):
        return pl.BlockSpec((rows, width), lambda b, s: (b * spb + s, off // width))

    y, st = pl.pallas_call(
        functools.partial(_glac_kernel, c=c), grid=(bsz, spb),
        in_specs=[col(hk, OD_Q), col(hk, OD_K), col(HALF_W, OD_V), col(HALF_W, OD_R),
                  col(LANE, OD_LRDT),
                  pl.BlockSpec((LANE, hk), lambda b, s: (0, 0)),
                  pl.BlockSpec((1, hk), lambda b, s: (0, 0)),
                  pl.BlockSpec((1, HALF_W), lambda b, s: (0, 0)),
                  pl.BlockSpec((1, H_C, V_C, K_C), lambda b, s: (b, 0, 0, 0)),
                  pl.BlockSpec((3 * c, c), lambda b, s: (0, 0))],
        out_specs=[pl.BlockSpec((rows, HALF_W), lambda b, s: (b * spb + s, 0)),
                   pl.BlockSpec((1, H_C, V_C, K_C), lambda b, s: (b, 0, 0, 0))],
        out_shape=[jax.ShapeDtypeStruct((t, HALF_W), BF16),
                   jax.ShapeDtypeStruct((bsz, H_C, V_C, K_C), F32)],
        scratch_shapes=[pltpu.VMEM((H_C, V_C, K_C), F32)],
        compiler_params=_cparams("parallel", "arbitrary"), name="mixer_c_gla")(
            proj, proj, proj, proj, proj,
            jnp.pad(gate_w2, ((0, LANE - GLA_RANK), (0, 0))),
            gate_b.reshape(1, hk),
            norm_g.reshape(1, HALF_W), s0t, _scan_mats(c))
    return y, jnp.swapaxes(st, -1, -2)


GW = HPG_D * P_D


def _ssd_kernel(z_ref, xbc_ref, lrdt_ref, cw_ref, cb_ref, dtb_ref, alog_ref, skip_ref, ng_ref,
                ex_ref, eye_ref, cm_ref, conv0_ref, s0_ref, y_ref, so_ref, s_ref, tail_ref, *, c):
    step = pl.program_id(1)

    @pl.when(step == 0)
    def _():
        s_ref[...] = s0_ref[0]
        tail_ref[...] = conv0_ref[0]

    x = xbc_ref[...]
    rows = x.shape[0]
    tail = tail_ref[...]
    sub8 = lax.broadcasted_iota(I32, (8, CONV_DIM), 0)
    conv = x * cw_ref[D_CONV - 1:D_CONV]
    for sh in range(1, D_CONV):
        rolled = pltpu.roll(x, sh, 0)
        head = jnp.where(sub8 < sh, pltpu.roll(tail, sh, 0), rolled[0:8])
        xk = jnp.concatenate([head, rolled[8:]], 0) if rows > 8 else head
        conv = conv + xk * cw_ref[D_CONV - 1 - sh:D_CONV - sh]
    tail_ref[...] = x[rows - 8:rows]
    xc = jax.nn.silu(conv + cb_ref[...])
    a = -jnp.exp(alog_ref[...])
    ex = ex_ref[...]
    tri = lax.broadcasted_iota(I32, (c, c), 1) <= lax.broadcasted_iota(I32, (c, c), 0)
    group = lambda arr, g, w: arr[:, g * w:(g + 1) * w]
    intra = []
    for ci in range(rows // c):
        rs = slice(ci * c, (ci + 1) * c)
        xs = xc[rs, 0:HALF_W]
        bm = xc[rs, HALF_W:HALF_W + G_D * N_D]
        cmat = xc[rs, HALF_W + G_D * N_D:CONV_DIM].astype(BF16)
        dt = _softplus(lrdt_ref[rs] + dtb_ref[...])
        la = dt * a
        sc = _dot_exact_l(cm_ref[...], la)
        cum = sc[0:c]
        scx = _dot_exact_r(sc, ex)
        cumx = scx[0:c]
        lastx = scx[c:2 * c]
        xdt = xs * _dot_exact_r(dt, ex)
        cum_t = sum(_dot_nt(eye_ref[...], p) for p in _split3(cum))
        cbs = [_dot_nt(group(cmat, g, N_D), group(bm, g, N_D).astype(BF16)) for g in range(G_D)]
        xdtb = xdt.astype(BF16)
        wmats = []
        for h in range(H_D):
            hl = DT_OFF + h
            diff = cum[:, hl:hl + 1] - cum_t[hl:hl + 1, :]
            dec = jnp.where(tri, jnp.exp(jnp.minimum(diff, 0.0)), 0.0)
            wmats.append((cbs[h // HPG_D] * dec).astype(BF16))
        y_intra = _cat([jnp.dot(wmats[h], xdtb[:, h * P_D:(h + 1) * P_D], preferred_element_type=F32)
                        for h in range(H_D)], 1)
        intra.append((y_intra + xs * skip_ref[...], cmat, jnp.exp(cumx),
                      bm.T.astype(BF16), (xdt * jnp.exp(lastx - cumx)).astype(BF16), jnp.exp(lastx[0:1])))
    for ci in range(rows // c):
        rs = slice(ci * c, (ci + 1) * c)
        y_local, cmat, e_in, bm_t, xw, e_last = intra[ci]
        sgs = [s_ref[g] for g in range(G_D)]
        inter = [jnp.dot(group(cmat, g, N_D), sgs[g].astype(BF16), preferred_element_type=F32)
                 for g in range(G_D)]
        upd = [jnp.dot(bm_t[g * N_D:(g + 1) * N_D], group(xw, g, GW), preferred_element_type=F32)
               for g in range(G_D)]
        for g in range(G_D):
            s_ref[g] = sgs[g] * group(e_last, g, GW) + upd[g]
        yg = (y_local + _cat(inter, 1) * e_in) * jax.nn.silu(z_ref[rs])
        rn = _cat([jnp.broadcast_to(lax.rsqrt(jnp.mean(jnp.square(group(yg, g, GW)), -1, keepdims=True) + EPS),
                                    (c, GW)) for g in range(G_D)], 1)
        y_ref[rs, :] = (yg * rn * ng_ref[...]).astype(BF16)

    @pl.when(step == pl.num_programs(1) - 1)
    def _():
        so_ref[0] = s_ref[...]


def _mixer_d(proj, conv_w, conv_b, dt_bias, a_log, skip, norm_g, s0, conv0, seq_len, rows):
    t = proj.shape[0]
    bsz = t // seq_len
    c = CHUNK if seq_len % CHUNK == 0 else seq_len
    spb = seq_len // rows
    s0g = s0.reshape(bsz, G_D, HPG_D, N_D, P_D).transpose(0, 1, 3, 2, 4).reshape(bsz, G_D, N_D, GW)
    conv0p = jnp.pad(conv0, ((0, 0), (8 - (D_CONV - 1), 0), (0, 0)))
    expand_np = np.zeros((LANE, HALF_W), np.float32)
    expand_np[DT_OFF:DT_OFF + H_D] = np.repeat(np.eye(H_D), P_D, axis=1)
    expand = jnp.asarray(expand_np, BF16)
    eye = jnp.asarray(np.eye(LANE), BF16)
    i = np.arange(c)[:, None]
    j = np.arange(c)[None, :]
    cm = jnp.asarray(np.concatenate([j <= i, np.ones((c, c), bool)], 0), BF16)
    skipx = jnp.repeat(skip, P_D)[None, :]
    lane_pad = lambda v: jnp.pad(v.reshape(1, H_D), ((0, 0), (DT_OFF, LANE - DT_OFF - H_D)))

    def col(width, off):
        return pl.BlockSpec((rows, width), lambda b, s: (b * spb + s, off // width))

    def full(shape):
        return pl.BlockSpec(shape, lambda b, s: (0,) * len(shape))

    y, st = pl.pallas_call(
        functools.partial(_ssd_kernel, c=c), grid=(bsz, spb),
        in_specs=[col(HALF_W, OD_Z),
                  col(CONV_DIM, OD_XBC),
                  col(LANE, OD_LRDT),
                  full((D_CONV, CONV_DIM)), full((1, CONV_DIM)), full((1, LANE)), full((1, LANE)),
                  full((1, HALF_W)), full((1, HALF_W)), full((LANE, HALF_W)), full((LANE, LANE)),
                  full((2 * c, c)),
                  pl.BlockSpec((1, 8, CONV_DIM), lambda b, s: (b, 0, 0)),
                  pl.BlockSpec((1, G_D, N_D, GW), lambda b, s: (b, 0, 0, 0))],
        out_specs=[pl.BlockSpec((rows, HALF_W), lambda b, s: (b * spb + s, 0)),
                   pl.BlockSpec((1, G_D, N_D, GW), lambda b, s: (b, 0, 0, 0))],
        out_shape=[jax.ShapeDtypeStruct((t, HALF_W), BF16),
                   jax.ShapeDtypeStruct((bsz, G_D, N_D, GW), F32)],
        scratch_shapes=[pltpu.VMEM((G_D, N_D, GW), F32), pltpu.VMEM((8, CONV_DIM), F32)],
        compiler_params=_cparams("parallel", "arbitrary"), name="mixer_d_ssd")(
            proj, proj, proj, conv_w, conv_b.reshape(1, CONV_DIM), lane_pad(dt_bias),
            lane_pad(a_log), skipx, norm_g.reshape(1, HALF_W), expand, eye, cm, conv0p, s0g)
    st = st.reshape(bsz, G_D, N_D, HPG_D, P_D).transpose(0, 1, 3, 2, 4).reshape(bsz, H_D, N_D, P_D)
    return y, st


def _layer_norm(hpre, g, b):
    mu = jnp.mean(hpre, -1, keepdims=True)
    d = hpre - mu
    var = jnp.mean(d * d, -1, keepdims=True)
    return d * lax.rsqrt(var + EPS) * g + b


U32 = jnp.uint32
HI_MASK = 0xFFFF0000


def _pack_halves(x):
    half = x.shape[1] // 2
    lo = lax.bitcast_convert_type(x[:, :half].astype(BF16).astype(F32), U32) >> 16
    hi = lax.bitcast_convert_type(x[:, half:].astype(BF16).astype(F32), U32) & U32(HI_MASK)
    return lo | hi


def _unpack_halves(w):
    return (lax.bitcast_convert_type(w << 16, F32), lax.bitcast_convert_type(w & U32(HI_MASK), F32))


def _emit_x(xn, o_ref, ob_ref, op_ref):
    o_ref[...] = xn
    ob_ref[...] = xn.astype(BF16)
    op_ref[...] = _pack_halves(xn)


def _x_out(t, tm):
    row = lambda width: pl.BlockSpec((tm, width), lambda i, *_: (i, 0))
    return ([row(D_MODEL), row(D_MODEL), row(HALF_W)],
            [jax.ShapeDtypeStruct((t, D_MODEL), F32), jax.ShapeDtypeStruct((t, D_MODEL), BF16),
             jax.ShapeDtypeStruct((t, HALF_W), U32)])


def _outproj_kernel(ya_ref, yb_ref, x_ref, w_ref, g_ref, b_ref, o_ref, ob_ref, op_ref):
    acc = jnp.dot(ya_ref[...], w_ref[0:HALF_W, :], preferred_element_type=F32)
    acc = acc + jnp.dot(yb_ref[...], w_ref[HALF_W:D_MODEL, :], preferred_element_type=F32)
    _emit_x(_layer_norm(ALPHA * x_ref[...] + acc, g_ref[...], b_ref[...]), o_ref, ob_ref, op_ref)


def _out_proj_ln(ya, yb, x, w, g, b, tm):
    t = x.shape[0]
    row = lambda width: pl.BlockSpec((tm, width), lambda i: (i, 0))
    out_specs, out_shape = _x_out(t, tm)
    return pl.pallas_call(
        _outproj_kernel, grid=(t // tm,),
        in_specs=[row(HALF_W), row(HALF_W), row(D_MODEL),
                  pl.BlockSpec((D_MODEL, D_MODEL), lambda i: (0, 0)),
                  pl.BlockSpec((1, D_MODEL), lambda i: (0, 0)),
                  pl.BlockSpec((1, D_MODEL), lambda i: (0, 0))],
        out_specs=out_specs, out_shape=out_shape,
        compiler_params=_cparams("parallel"), name="out_proj_ln")(
            ya, yb, x, w, g.reshape(1, D_MODEL), b.reshape(1, D_MODEL))


RANK_BITS = 20
RANK_MASK = (1 << RANK_BITS) - 1


def _router_kernel(x_ref, wr_ref, bias_ref, tri_ref, code_ref, gw_ref, cnt_ref, run_ref):
    step = pl.program_id(0)

    @pl.when(step == 0)
    def _():
        run_ref[...] = jnp.zeros_like(run_ref)

    tm = x_ref.shape[0]
    logits = _dot_nt(wr_ref[...], x_ref[...])
    scores = jax.nn.sigmoid(logits)
    sel = scores + bias_ref[:, 0:1]
    neg = -jnp.inf
    sub = lax.broadcasted_iota(I32, (GROUP_SIZE, tm), 0).astype(F32)
    gsc = []
    for g in range(N_GROUPS):
        blk = sel[g * GROUP_SIZE:(g + 1) * GROUP_SIZE]
        m1 = jnp.max(blk, 0, keepdims=True)
        i1 = jnp.min(jnp.where(blk == m1, sub, float(GROUP_SIZE)), 0, keepdims=True)
        m2 = jnp.max(jnp.where(sub == i1, neg, blk), 0, keepdims=True)
        gsc.append(m1 + m2)
    cur = jnp.concatenate(gsc, 0)
    gio = lax.broadcasted_iota(I32, (N_GROUPS, tm), 0).astype(F32)
    gmask = jnp.zeros((N_GROUPS, tm), F32)
    for _ in range(TOPK_GROUPS):
        m = jnp.max(cur, 0, keepdims=True)
        i = jnp.min(jnp.where(cur == m, gio, float(N_GROUPS)), 0, keepdims=True)
        pick = gio == i
        gmask = jnp.where(pick, 1.0, gmask)
        cur = jnp.where(pick, neg, cur)
    emask = jnp.concatenate(
        [jnp.broadcast_to(gmask[g:g + 1], (GROUP_SIZE, tm)) for g in range(N_GROUPS)], 0)
    cur = jnp.where(emask > 0.5, sel, neg)
    eio = lax.broadcasted_iota(I32, (N_EXPERTS, tm), 0).astype(F32)
    member = jnp.zeros((N_EXPERTS, tm), F32)
    idxs, scs = [], []
    for _ in range(TOP_K):
        m = jnp.max(cur, 0, keepdims=True)
        i = jnp.min(jnp.where(cur == m, eio, float(N_EXPERTS)), 0, keepdims=True)
        pick = eio == i
        idxs.append(i)
        scs.append(jnp.sum(jnp.where(pick, scores, 0.0), 0, keepdims=True))
        member = jnp.where(pick, 1.0, member)
        cur = jnp.where(pick, neg, cur)
    idx = jnp.concatenate(idxs, 0)
    sc = jnp.concatenate(scs, 0)
    gw_ref[...] = sc / jnp.sum(sc, 0, keepdims=True) * ROUTE_SCALE
    before = jnp.dot(member.astype(BF16), tri_ref[...], preferred_element_type=F32) + run_ref[:, 0:1]
    ranks = [jnp.sum(jnp.where(eio == idxs[k], before, 0.0), 0, keepdims=True) for k in range(TOP_K)]
    rank = jnp.concatenate(ranks, 0).astype(I32)
    code_ref[...] = idx.astype(I32) * (1 << RANK_BITS) + rank
    run_ref[...] = run_ref[...] + jnp.sum(member, 1, keepdims=True)
    cnt_ref[...] = run_ref[...]


def _router(x, w_router, bias, tm):
    t = x.shape[0]
    assert t * TOP_K < (1 << RANK_BITS)
    tri = jnp.asarray(np.arange(tm)[:, None] < np.arange(tm)[None, :], BF16)
    kt = lambda dt: jax.ShapeDtypeStruct((TOP_K, t), dt)
    return pl.pallas_call(
        _router_kernel, grid=(t // tm,),
        in_specs=[pl.BlockSpec((tm, D_MODEL), lambda i: (i, 0)),
                  pl.BlockSpec((N_EXPERTS, D_MODEL), lambda i: (0, 0)),
                  pl.BlockSpec((N_EXPERTS, LANE), lambda i: (0, 0)),
                  pl.BlockSpec((tm, tm), lambda i: (0, 0))],
        out_specs=[pl.BlockSpec((TOP_K, tm), lambda i: (0, i)),
                   pl.BlockSpec((TOP_K, tm), lambda i: (0, i)),
                   pl.BlockSpec((N_EXPERTS, LANE), lambda i: (0, 0))],
        out_shape=[kt(I32), kt(F32), jax.ShapeDtypeStruct((N_EXPERTS, LANE), F32)],
        scratch_shapes=[pltpu.VMEM((N_EXPERTS, LANE), F32)],
        compiler_params=_cparams("arbitrary"), name="moe_router")(
            x, w_router.T.astype(BF16), jnp.broadcast_to(bias[:, None], (N_EXPERTS, LANE)), tri)


def _slots_kernel(start_ref, code_ref, pos_ref):
    code = code_ref[...]
    e = lax.shift_right_logical(code, RANK_BITS)
    r = code & RANK_MASK
    pos = r
    for x in range(N_EXPERTS):
        pos = jnp.where(e == x, r + start_ref[x], pos)
    pos_ref[...] = pos


def _slots(code, slot_start):
    k, t = code.shape
    tile = min(t, 4096)
    gs = pltpu.PrefetchScalarGridSpec(
        num_scalar_prefetch=1, grid=(t // tile,),
        in_specs=[pl.BlockSpec((k, tile), lambda i, *_: (0, i))],
        out_specs=pl.BlockSpec((k, tile), lambda i, *_: (0, i)))
    return pl.pallas_call(_slots_kernel, grid_spec=gs, out_shape=jax.ShapeDtypeStruct((k, t), I32),
                          compiler_params=_cparams("parallel"), name="moe_slots")(slot_start, code)


def _dispatch_kernel(fill_ref, pos_ref, x_ref, xs_ref, zrow_ref, sem, *, eps):
    step = pl.program_id(0)
    tm = x_ref.shape[0]
    zrow_ref[...] = jnp.zeros_like(zrow_ref)

    def copy(r, k):
        return pltpu.make_async_copy(x_ref.at[pl.ds(r, 1)], xs_ref.at[pl.ds(pos_ref[0, 0, k * tm + r], 1)], sem)

    def zero_copy(row):
        return pltpu.make_async_copy(zrow_ref.at[pl.ds(0, 1)], xs_ref.at[pl.ds(row, 1)], sem)

    def start(r, carry):
        for k in range(TOP_K):
            copy(r, k).start(priority=k % 2)
        return carry

    def wait(r, carry):
        for k in range(TOP_K):
            copy(r, k).wait()
        return carry

    def fill_expert(e, total):
        base = fill_ref[0, e]
        n = fill_ref[1, e]

        def go(r, c):
            zero_copy(base + r).start()
            return c

        lax.fori_loop(0, n, go, 0)
        return total + n

    def fill_wait(r, c):
        zero_copy(0).wait()
        return c

    lax.fori_loop(0, tm, start, 0)
    e0 = jnp.minimum(step * eps, N_EXPERTS)
    e1 = jnp.minimum(e0 + eps, N_EXPERTS)
    n_fill = lax.fori_loop(e0, e1, fill_expert, 0)
    lax.fori_loop(0, tm, wait, 0)
    lax.fori_loop(0, n_fill, fill_wait, 0)


def _dispatch(xp, pos_tiles, fill, nslot, tm):
    t = xp.shape[0]
    steps = t // tm
    eps = -(-N_EXPERTS // steps)
    gs = pltpu.PrefetchScalarGridSpec(
        num_scalar_prefetch=1, grid=(steps,),
        in_specs=[pl.BlockSpec((1, 1, TOP_K * tm), lambda i, *_: (i, 0, 0), memory_space=pltpu.SMEM),
                  pl.BlockSpec((tm, HALF_W), lambda i, *_: (i, 0))],
        out_specs=pl.BlockSpec(memory_space=pl.ANY),
        scratch_shapes=[pltpu.VMEM((8, HALF_W), xp.dtype), pltpu.SemaphoreType.DMA(())])
    return pl.pallas_call(
        functools.partial(_dispatch_kernel, eps=eps), grid_spec=gs,
        out_shape=jax.ShapeDtypeStruct((nslot, HALF_W), xp.dtype),
        compiler_params=_cparams("arbitrary"), name="moe_dispatch")(fill, pos_tiles, xp)


def _expert_kernel(be_ref, nu_ref, x_ref, wg_ref, wu_ref, wd_ref, o_ref, *, packed):
    del be_ref

    @pl.when(pl.program_id(0) < nu_ref[0])
    def _():
        if packed:
            lo, hi = _unpack_halves(x_ref[...])
            lo = lo.astype(BF16)
            hi = hi.astype(BF16)
            hg = (jnp.dot(lo, wg_ref[0, 0:HALF_W, :], preferred_element_type=F32)
                  + jnp.dot(hi, wg_ref[0, HALF_W:D_MODEL, :], preferred_element_type=F32))
            hu = (jnp.dot(lo, wu_ref[0, 0:HALF_W, :], preferred_element_type=F32)
                  + jnp.dot(hi, wu_ref[0, HALF_W:D_MODEL, :], preferred_element_type=F32))
        else:
            x = x_ref[...]
            hg = jnp.dot(x, wg_ref[0], preferred_element_type=F32)
            hu = jnp.dot(x, wu_ref[0], preferred_element_type=F32)
        hb = (jax.nn.silu(hg) * hu).astype(BF16)
        out = jnp.dot(hb, wd_ref[0], preferred_element_type=F32)
        o_ref[...] = _pack_halves(out) if packed else out.astype(BF16)

    @pl.when(pl.program_id(0) >= nu_ref[0])
    def _():
        o_ref[...] = jnp.zeros_like(o_ref)


def _expert_ffn(xs, blk_e, nused, wg, wu, wd, bm, packed):
    nslot, width = xs.shape
    nb = nslot // bm
    de = wg.shape[-1]
    gs = pltpu.PrefetchScalarGridSpec(
        num_scalar_prefetch=2, grid=(nb,),
        in_specs=[pl.BlockSpec((bm, width), lambda b, be, nu: (jnp.minimum(b, nu[0] - 1), 0)),
                  pl.BlockSpec((1, D_MODEL, de), lambda b, be, nu: (be[b], 0, 0)),
                  pl.BlockSpec((1, D_MODEL, de), lambda b, be, nu: (be[b], 0, 0)),
                  pl.BlockSpec((1, de, D_MODEL), lambda b, be, nu: (be[b], 0, 0))],
        out_specs=pl.BlockSpec((bm, width), lambda b, be, nu: (b, 0)))
    return pl.pallas_call(
        functools.partial(_expert_kernel, packed=packed), grid_spec=gs,
        out_shape=jax.ShapeDtypeStruct((nslot, width), xs.dtype),
        compiler_params=_cparams("arbitrary"), name="moe_expert_ffn")(blk_e, nused, xs, wg, wu, wd)


def _combine_kernel(pos_ref, gw_ref, x_ref, sh_ref, eo_ref, g_ref, b_ref, o_ref, ob_ref, op_ref, buf, sem):
    tm = x_ref.shape[0]

    def copy(r, k):
        return pltpu.make_async_copy(eo_ref.at[pl.ds(pos_ref[0, 0, k * tm + r], 1)],
                                     buf.at[k, pl.ds(r, 1)], sem)

    def start(r, carry):
        for k in range(TOP_K):
            copy(r, k).start(priority=k % 2)
        return carry

    def wait(r, carry):
        for k in range(TOP_K):
            copy(r, k).wait()
        return carry

    lax.fori_loop(0, tm, start, 0)
    lax.fori_loop(0, tm, wait, 0)
    sh = sh_ref[...].astype(F32)
    lo_acc = sh[:, :HALF_W]
    hi_acc = sh[:, HALF_W:]
    for k in range(TOP_K):
        lo, hi = _unpack_halves(buf[k])
        wk = gw_ref[:, k:k + 1]
        lo_acc = lo_acc + lo * wk
        hi_acc = hi_acc + hi * wk
    ffn = jnp.concatenate([lo_acc, hi_acc], 1)
    _emit_x(_layer_norm(ALPHA * x_ref[...] + ffn, g_ref[...], b_ref[...]), o_ref, ob_ref, op_ref)


def _combine_ln(pos_tiles, gw_t, x, sh, eo, g, b, tm):
    t = x.shape[0]
    row = lambda width: pl.BlockSpec((tm, width), lambda i: (i, 0))
    const = pl.BlockSpec((1, D_MODEL), lambda i: (0, 0))
    out_specs, out_shape = _x_out(t, tm)
    return pl.pallas_call(
        _combine_kernel, grid=(t // tm,),
        in_specs=[pl.BlockSpec((1, 1, TOP_K * tm), lambda i: (i, 0, 0), memory_space=pltpu.SMEM),
                  row(TOP_K), row(D_MODEL), row(D_MODEL), pl.BlockSpec(memory_space=pl.ANY), const, const],
        out_specs=out_specs, out_shape=out_shape,
        scratch_shapes=[pltpu.VMEM((TOP_K, tm, HALF_W), U32), pltpu.SemaphoreType.DMA(())],
        compiler_params=_cparams("arbitrary"), name="moe_combine_ln")(
            pos_tiles, gw_t, x, sh, eo, g.reshape(1, D_MODEL), b.reshape(1, D_MODEL))


def _moe_ln(x, xb, xp, w_router, bias, wg, wu, wd, wsg, wsu, wsd, g, b, tm, bm):
    t = x.shape[0]
    code, gw, cnt = _router(xb, w_router, bias, tm)
    counts = cnt[:, 0].astype(I32)
    padded = (counts + bm - 1) // bm * bm
    pad_end = jnp.cumsum(padded)
    slot_start = pad_end - padded
    fill = jnp.stack([slot_start + counts, padded - counts])
    nb = -(-(t * TOP_K + N_EXPERTS * (bm - 1)) // bm)
    blk_first = jnp.arange(nb, dtype=I32) * bm
    blk_e = jnp.minimum(jnp.sum((pad_end[None, :] <= blk_first[:, None]).astype(I32), 1), N_EXPERTS - 1)
    nused = (pad_end[-1] // bm).astype(I32).reshape(1)
    pos = _slots(code, slot_start)
    pos_tiles = pos.reshape(TOP_K, t // tm, tm).transpose(1, 0, 2).reshape(t // tm, 1, TOP_K * tm)
    xs = _dispatch(xp, pos_tiles, fill, nb * bm, tm)
    eo = _expert_ffn(xs, blk_e, nused, wg, wu, wd, bm, True)
    shared = _expert_ffn(xb, jnp.zeros((t // tm,), I32), jnp.full((1,), t // tm, I32),
                         wsg[None], wsu[None], wsd[None], tm, False)
    return _combine_ln(pos_tiles, gw.T, x, shared, eo, g, b, tm)


def _odd_w_in(w):
    sizes = (H_C * K_C, H_C * K_C, H_C * V_C, GLA_RANK, H_C * V_C, HALF_W, CONV_DIM, H_D)
    q, k, v, lr, r, z, xbc, dt = jnp.split(w, [int(s) for s in np.cumsum(sizes)[:-1]], axis=-1)
    pad = jnp.zeros((w.shape[0], OD_XBC - (OD_LRDT + GLA_RANK + H_D)), w.dtype)
    return jnp.concatenate([q, k, v, r, z, lr, dt, pad, xbc], -1)


def _trunk(x3, hgrn0, gla0, ssm0, conv0, p, cfg):
    bsz, seq_len, _ = x3.shape
    t = bsz * seq_len
    x = x3.reshape(t, D_MODEL)
    xb = None
    tm, rows, bm = cfg["tm"], cfg["rows"], cfg["bm"]
    outs = {}
    for l in range(DEPTH):
        j = l // 2
        if l % 2 == 0:
            proj = _proj(x if xb is None else xb, p["ev_w_in"][j], cfg["ptm"], cfg["ptn"])
            ya, v_rows = _mixer_a(proj, p["ev_a_ln_g"][j], p["ev_a_ln_b"][j], p["ev_a_ws"][j],
                                  p["ev_a_bs"][j], seq_len, cfg["arows"], cfg["emit_v"])
            yb, s_b = _mixer_b(proj, p["hgrn_lb_logits"], p["ev_b_norm_g"][j], hgrn0[j], seq_len, rows, l)
            outs["v_rows"] = v_rows
            outs["hgrn"] = s_b
            w_out = p["ev_w_out"][j]
        else:
            proj = _proj(x if xb is None else xb, p["od_w_in"][j], cfg["ptm"], cfg["ptn"])
            ya, s_c = _mixer_c(proj, p["od_c_gate_w2"][j], p["od_c_gate_b"][j], p["od_c_norm_g"][j],
                               gla0[j], seq_len, rows)
            yb, s_d = _mixer_d(proj, p["od_d_conv_w"][j], p["od_d_conv_b"][j], p["od_d_dt_bias"][j],
                               p["od_d_a_log"][j], p["od_d_skip"][j], p["od_d_norm_g"][j],
                               ssm0[j], conv0[j], seq_len, rows)
            keep = min(D_CONV - 1, seq_len)
            xbc = proj.reshape(bsz, seq_len, OD_PAD)[:, seq_len - keep:, OD_XBC:OD_XBC + CONV_DIM]
            outs["conv"] = jnp.concatenate([conv0[j], xbc], 1)[:, -(D_CONV - 1):]
            outs["gla"] = s_c
            outs["ssm"] = s_d
            w_out = p["od_w_out"][j]
        x, xb, xp = _out_proj_ln(ya, yb, x, w_out, p["ln1_g"][l], p["ln1_b"][l], tm)
        x, xb, _ = _moe_ln(x, xb, xp, p["moe_w_router"][l], p["moe_router_bias"][l], p["moe_w_gate"][l],
                           p["moe_w_up"][l], p["moe_w_down"][l], p["moe_ws_gate"][l], p["moe_ws_up"][l],
                           p["moe_ws_down"][l], p["ln2_g"][l], p["ln2_b"][l], tm, bm)
    return x.reshape(bsz, seq_len, D_MODEL), outs


def _config(bsz, seq_len):
    t = bsz * seq_len
    if seq_len % CHUNK == 0:
        return dict(tm=256, rows=min(128, seq_len), arows=min(256, seq_len), bm=512,
                    ptm=min(1024, t), ptn=512, emit_v=False)
    return dict(tm=t, rows=seq_len, arows=t, bm=128, ptm=t, ptn=512, emit_v=True)


def kernel(x_prompt, x_sample, state_b_hgrn, state_c_gla, state_d_ssm, state_d_conv, ev_w_in, ev_a_ln_g, ev_a_ln_b, ev_a_ws, ev_a_bs, ev_b_norm_g, ev_w_out, hgrn_lb_logits, od_w_in, od_c_gate_w2, od_c_gate_b, od_c_norm_g, od_d_conv_w, od_d_conv_b, od_d_dt_bias, od_d_a_log, od_d_skip, od_d_norm_g, od_w_out, ln1_g, ln1_b, ln2_g, ln2_b, moe_w_router, moe_router_bias, moe_w_gate, moe_w_up, moe_w_down, moe_ws_gate, moe_ws_up, moe_ws_down):
    p = dict(
        ev_w_in=ev_w_in.astype(BF16), ev_a_ln_g=ev_a_ln_g, ev_a_ln_b=ev_a_ln_b, ev_a_ws=ev_a_ws,
        ev_a_bs=ev_a_bs, ev_b_norm_g=ev_b_norm_g, ev_w_out=ev_w_out.astype(BF16),
        hgrn_lb_logits=hgrn_lb_logits,
        od_w_in=jnp.stack([_odd_w_in(od_w_in[j]) for j in range(od_w_in.shape[0])]).astype(BF16),
        od_c_gate_w2=od_c_gate_w2, od_c_gate_b=od_c_gate_b, od_c_norm_g=od_c_norm_g,
        od_d_conv_w=od_d_conv_w, od_d_conv_b=od_d_conv_b, od_d_dt_bias=od_d_dt_bias,
        od_d_a_log=od_d_a_log, od_d_skip=od_d_skip, od_d_norm_g=od_d_norm_g,
        od_w_out=od_w_out.astype(BF16), ln1_g=ln1_g, ln1_b=ln1_b, ln2_g=ln2_g, ln2_b=ln2_b,
        moe_w_router=moe_w_router, moe_router_bias=moe_router_bias,
        moe_w_gate=moe_w_gate.astype(BF16), moe_w_up=moe_w_up.astype(BF16),
        moe_w_down=moe_w_down.astype(BF16), moe_ws_gate=moe_ws_gate.astype(BF16),
        moe_ws_up=moe_ws_up.astype(BF16), moe_ws_down=moe_ws_down.astype(BF16))
    bp, lp, _ = x_prompt.shape
    n_even = state_b_hgrn.shape[0]
    n_odd = state_c_gla.shape[0]
    zeros = lambda n, *s: jnp.zeros((n, bp) + s, F32)
    y_p, o_p = _trunk(x_prompt, zeros(n_even, H_B, K_B, V_B), zeros(n_odd, H_C, K_C, V_C),
                      zeros(n_odd, H_D, N_D, P_D), zeros(n_odd, D_CONV - 1, CONV_DIM), p,
                      _config(bp, lp))
    bs, ls, _ = x_sample.shape
    y_s, o_s = _trunk(x_sample, state_b_hgrn, state_c_gla, state_d_ssm, state_d_conv, p,
                      _config(bs, ls))
    a_v = o_s["v_rows"].reshape(1, bs, ls, HALF_W)
    return (y_p, y_s, a_v, o_p["hgrn"][None], o_s["hgrn"][None], o_p["gla"][None], o_s["gla"][None],
            o_p["ssm"][None], o_s["ssm"][None], o_p["conv"][None], o_s["conv"][None])
```

```python
import functools
import math

import jax
import jax.numpy as jnp
import numpy as np
from jax import lax
from jax.experimental import pallas as pl
from jax.experimental.pallas import tpu as pltpu

F32 = jnp.float32
BF16 = jnp.bfloat16
I32 = jnp.int32

D_MODEL = 2048
DEPTH = 2
CHUNK = 64
SUB = 16
HALF_W = D_MODEL // 2
A_CHUNK = 128
H_A = 4
DA = HALF_W // H_A
H_B = 8
K_B = 128
V_B = HALF_W // H_B
H_C = 4
V_C = HALF_W // H_C
K_C = V_C // 2
GLA_RANK = 16
GLA_TAU = 16.0
P_D = 64
H_D = HALF_W // P_D
G_D = 2
HPG_D = H_D // G_D
N_D = 128
D_CONV = 4
CONV_DIM = HALF_W + 2 * G_D * N_D
N_EXPERTS = 64
N_GROUPS = 8
GROUP_SIZE = N_EXPERTS // N_GROUPS
TOPK_GROUPS = 4
TOP_K = 8
D_EXPERT = 512
ROUTE_SCALE = 2.5
ALPHA = (2 * DEPTH) ** 0.25
EPS = 1e-5
LANE = 128
VMEM_LIMIT = 56 * 1024 * 1024

OD_Q, OD_K, OD_V, OD_R, OD_Z, OD_LRDT, OD_XBC = 0, 512, 1024, 2048, 3072, 4096, 4608
OD_PAD = 6144
DT_OFF = GLA_RANK


def _cparams(*sem):
    return pltpu.CompilerParams(dimension_semantics=sem, vmem_limit_bytes=VMEM_LIMIT)


def _split3(x):
    hi = x.astype(BF16)
    r = x - hi.astype(F32)
    mid = r.astype(BF16)
    lo = (r - mid.astype(F32)).astype(BF16)
    return hi, mid, lo


def _split2(x):
    hi = x.astype(BF16)
    return hi, (x - hi.astype(F32)).astype(BF16)


def _dot_exact_l(a_bf16, x):
    return sum(jnp.dot(a_bf16, p, preferred_element_type=F32) for p in _split3(x))


def _dot_exact_r(x, b_bf16):
    return sum(jnp.dot(p, b_bf16, preferred_element_type=F32) for p in _split3(x))


def _dot_nt(a, b):
    return lax.dot_general(a, b, (((1,), (1,)), ((), ())), preferred_element_type=F32)


def _softplus(x):
    return jnp.maximum(x, 0.0) + jnp.log1p(jnp.exp(-jnp.abs(x)))


def _proj_kernel(x_ref, w_ref, o_ref, *scratch):
    if scratch:
        xb_ref, = scratch

        @pl.when(pl.program_id(1) == 0)
        def _():
            xb_ref[...] = x_ref[...].astype(BF16)
        x = xb_ref[...]
    else:
        x = x_ref[...]
    o_ref[...] = jnp.dot(x, w_ref[...], preferred_element_type=F32)


def _proj(x, w, tm, tn):
    t, k = x.shape
    n = w.shape[1]
    scratch = [] if x.dtype == BF16 else [pltpu.VMEM((tm, k), BF16)]
    return pl.pallas_call(
        _proj_kernel, grid=(t // tm, n // tn),
        in_specs=[pl.BlockSpec((tm, k), lambda i, j: (i, 0)),
                  pl.BlockSpec((k, tn), lambda i, j: (0, j))],
        out_specs=pl.BlockSpec((tm, tn), lambda i, j: (i, j)),
        out_shape=jax.ShapeDtypeStruct((t, n), F32),
        scratch_shapes=scratch, compiler_params=_cparams("parallel", "arbitrary"),
        name="in_proj")(x, w)


def _mixa_kernel(u_ref, v_ref, lng_ref, lnb_ref, w_ref, bs_ref, ya_ref, *vrows, ca):
    gu = jax.nn.gelu(u_ref[...])
    gv = jax.nn.gelu(v_ref[...])
    rows = gu.shape[0]
    for h in range(H_A):
        sl = slice(h * DA, (h + 1) * DA)
        vh = gv[:, sl]
        mu = jnp.mean(vh, -1, keepdims=True)
        d = vh - mu
        var = jnp.mean(d * d, -1, keepdims=True)
        vn = d * lax.rsqrt(var + EPS) * lng_ref[h] + lnb_ref[h]
        if vrows:
            vrows[0][:, sl] = vn
        vnb = vn.astype(BF16)
        for c in range(rows // ca):
            rs = slice(c * ca, (c + 1) * ca)
            s = jnp.dot(w_ref[h], vnb[rs], preferred_element_type=F32) + bs_ref[h]
            ya_ref[rs, sl] = (gu[rs, sl] * s).astype(BF16)


def _mixer_a(proj, lng, lnb, ws, bs, seq_len, rows, emit_v):
    t = proj.shape[0]
    ca = min(A_CHUNK, seq_len)
    pos = np.arange(A_CHUNK)
    mask = (pos[None, :] // CHUNK) <= (pos[:, None] // CHUNK)
    w = (ws * mask)[:, :ca, :ca].astype(BF16)
    bsb = jnp.broadcast_to(bs[:, :ca, None], (H_A, ca, DA)).astype(F32)
    nb = HALF_W // HALF_W
    out_shape = [jax.ShapeDtypeStruct((t, HALF_W), BF16)]
    out_specs = [pl.BlockSpec((rows, HALF_W), lambda i: (i, 0))]
    if emit_v:
        out_shape.append(jax.ShapeDtypeStruct((t, HALF_W), F32))
        out_specs.append(pl.BlockSpec((rows, HALF_W), lambda i: (i, 0)))
    del nb
    res = pl.pallas_call(
        functools.partial(_mixa_kernel, ca=ca), grid=(t // rows,),
        in_specs=[pl.BlockSpec((rows, HALF_W), lambda i: (i, 0)),
                  pl.BlockSpec((rows, HALF_W), lambda i: (i, 1)),
                  pl.BlockSpec((H_A, 1, DA), lambda i: (0, 0, 0)),
                  pl.BlockSpec((H_A, 1, DA), lambda i: (0, 0, 0)),
                  pl.BlockSpec((H_A, ca, ca), lambda i: (0, 0, 0)),
                  pl.BlockSpec((H_A, ca, DA), lambda i: (0, 0, 0))],
        out_specs=out_specs, out_shape=out_shape,
        compiler_params=_cparams("parallel"), name="mixer_a")(
            proj, proj, lng[:, None, :], lnb[:, None, :], w, bsb)
    return res if emit_v else (res[0], None)


def _scan_mats(c):
    i = np.arange(c)[:, None]
    j = np.arange(c)[None, :]
    tril = (j <= i)
    local = tril & ((i // SUB) == (j // SUB))
    ones = np.ones((c, c), bool)
    return jnp.asarray(np.concatenate([local, tril, ones], 0), BF16)


def _cat(parts, axis):
    return parts[0] if len(parts) == 1 else jnp.concatenate(parts, axis)


def _gla_intra(q, k, v, sc, c, nh, kd, vd):
    lc, cum, last = sc[0:c], sc[c:2 * c], sc[2 * c:3 * c]
    ns = c // SUB
    q_hi, q_lo = _split2(q * jnp.exp(lc))
    pre = cum - lc
    vb = v.astype(BF16)
    att = []
    for blk in range(ns):
        n = SUB * (blk + 1)
        r0 = SUB * blk
        k_hi, k_lo = _split2(k[:n] * jnp.exp(pre[r0:r0 + 1] - cum[:n]))
        rq = slice(r0, r0 + SUB)
        for h in range(nh):
            ks = slice(h * kd, (h + 1) * kd)
            att.append(_dot_nt(q_hi[rq, ks], k_hi[:, ks])
                       + (_dot_nt(q_hi[rq, ks], k_lo[:, ks]) + _dot_nt(q_lo[rq, ks], k_hi[:, ks])))
    attb = []
    for blk in range(ns):
        n = SUB * (blk + 1)
        keep = (lax.broadcasted_iota(I32, (SUB, n), 1) - SUB * blk) <= lax.broadcasted_iota(I32, (SUB, n), 0)
        attb += [jnp.where(keep, att[blk * nh + h], 0.0).astype(BF16) for h in range(nh)]
    rows_out = []
    for blk in range(ns):
        n = SUB * (blk + 1)
        rows_out.append(_cat([jnp.dot(attb[blk * nh + h], vb[:n, h * vd:(h + 1) * vd],
                                      preferred_element_type=F32) for h in range(nh)], 1))
    qc = (q * jnp.exp(cum)).astype(BF16)
    kc = (k * jnp.exp(last - cum)).astype(BF16)
    return _cat(rows_out, 0), qc, kc, v.T.astype(BF16), jnp.exp(last[0:1])


def _gla_state(intra, s_ref, nh, kd, vd):
    o_intra, qc, kc, vt, e_last = intra
    sts = [s_ref[h] for h in range(nh)]
    inter = [_dot_nt(qc[:, h * kd:(h + 1) * kd], sts[h].astype(BF16)) for h in range(nh)]
    upd = [jnp.dot(vt[h * vd:(h + 1) * vd], kc[:, h * kd:(h + 1) * kd], preferred_element_type=F32)
           for h in range(nh)]
    for h in range(nh):
        s_ref[h] = sts[h] * e_last[:, h * kd:(h + 1) * kd] + upd[h]
    return o_intra + _cat(inter, 1)


def _hgrn_kernel(q_ref, f_ref, i_ref, g_ref, lbl_ref, ng_ref, s0_ref, cm_ref, y_ref, so_ref, s_ref,
                 *, c, layer):
    step = pl.program_id(1)

    @pl.when(step == 0)
    def _():
        s_ref[...] = s0_ref[0]

    lg = lbl_ref[...]
    ex = jnp.exp(lg - jnp.max(lg, 0, keepdims=True))
    sm = ex / jnp.sum(ex, 0, keepdims=True)
    lb = jnp.sum(sm[:layer + 1], 0, keepdims=True)
    rows = q_ref.shape[0]
    intra = []
    for ci in range(rows // c):
        rs = slice(ci * c, (ci + 1) * c)
        f = lb + (1.0 - lb) * jax.nn.sigmoid(f_ref[rs])
        g = jnp.log(f)
        sc = _dot_exact_l(cm_ref[...], g)
        intra.append(_gla_intra(jax.nn.silu(q_ref[rs]), 1.0 - f, i_ref[rs], sc, c, H_B, K_B, V_B))
    for ci in range(rows // c):
        rs = slice(ci * c, (ci + 1) * c)
        o = _gla_state(intra[ci], s_ref, H_B, K_B, V_B)
        rn = _cat([jnp.broadcast_to(lax.rsqrt(jnp.mean(jnp.square(o[:, h * V_B:(h + 1) * V_B]), -1,
                                                        keepdims=True) + EPS), (c, V_B))
                   for h in range(H_B)], 1)
        y_ref[rs, :] = (o * rn * ng_ref[...] * jax.nn.sigmoid(g_ref[rs])).astype(BF16)

    @pl.when(step == pl.num_programs(1) - 1)
    def _():
        so_ref[0] = s_ref[...]


def _mixer_b(proj, lb_logits, norm_g, s0, seq_len, rows, layer):
    t = proj.shape[0]
    bsz = t // seq_len
    c = CHUNK if seq_len % CHUNK == 0 else seq_len
    spb = seq_len // rows
    s0t = jnp.swapaxes(s0, -1, -2)
    col = lambda j: pl.BlockSpec((rows, HALF_W), lambda b, s, j=j: (b * spb + s, j))
    nl = lb_logits.shape[0]
    y, st = pl.pallas_call(
        functools.partial(_hgrn_kernel, c=c, layer=layer), grid=(bsz, spb),
        in_specs=[col(2), col(3), col(4), col(5),
                  pl.BlockSpec((nl, HALF_W), lambda b, s: (0, 0)),
                  pl.BlockSpec((1, HALF_W), lambda b, s: (0, 0)),
                  pl.BlockSpec((1, H_B, V_B, K_B), lambda b, s: (b, 0, 0, 0)),
                  pl.BlockSpec((3 * c, c), lambda b, s: (0, 0))],
        out_specs=[pl.BlockSpec((rows, HALF_W), lambda b, s: (b * spb + s, 0)),
                   pl.BlockSpec((1, H_B, V_B, K_B), lambda b, s: (b, 0, 0, 0))],
        out_shape=[jax.ShapeDtypeStruct((t, HALF_W), BF16),
                   jax.ShapeDtypeStruct((bsz, H_B, V_B, K_B), F32)],
        scratch_shapes=[pltpu.VMEM((H_B, V_B, K_B), F32)],
        compiler_params=_cparams("parallel", "arbitrary"), name="mixer_b_hgrn")(
            proj, proj, proj, proj, lb_logits, norm_g.reshape(1, HALF_W), s0t, _scan_mats(c))
    return y, jnp.swapaxes(st, -1, -2)


def _glac_kernel(q_ref, k_ref, v_ref, r_ref, lrdt_ref, w2_ref, gb_ref, ng_ref, s0_ref, cm_ref,
                 y_ref, so_ref, s_ref, *, c):
    step = pl.program_id(1)

    @pl.when(step == 0)
    def _():
        s_ref[...] = s0_ref[0]

    rows = q_ref.shape[0]
    intra = []
    for ci in range(rows // c):
        rs = slice(ci * c, (ci + 1) * c)
        z = jnp.dot(lrdt_ref[rs], w2_ref[...], preferred_element_type=F32,
                    precision=lax.Precision.HIGHEST) + gb_ref[...]
        g = -_softplus(-z) / GLA_TAU
        sc = _dot_exact_l(cm_ref[...], g)
        intra.append(_gla_intra(q_ref[rs] * (K_C ** -0.5), k_ref[rs], v_ref[rs], sc, c, H_C, K_C, V_C))
    for ci in range(rows // c):
        rs = slice(ci * c, (ci + 1) * c)
        o = _gla_state(intra[ci], s_ref, H_C, K_C, V_C)
        rn = _cat([jnp.broadcast_to(lax.rsqrt(jnp.mean(jnp.square(o[:, h * V_C:(h + 1) * V_C]), -1,
                                                        keepdims=True) + EPS), (c, V_C))
                   for h in range(H_C)], 1)
        y_ref[rs, :] = (o * rn * ng_ref[...] * jax.nn.silu(r_ref[rs])).astype(BF16)

    @pl.when(step == pl.num_programs(1) - 1)
    def _():
        so_ref[0] = s_ref[...]


def _mixer_c(proj, gate_w2, gate_b, norm_g, s0, seq_len, rows):
    t = proj.shape[0]
    bsz = t // seq_len
    c = CHUNK if seq_len % CHUNK == 0 else seq_len
    spb = seq_len // rows
    s0t = jnp.swapaxes(s0, -1, -2)
    hk = H_C * K_C

    def col(width, off):
        return pl.BlockSpec((rows, width), lambda b, s: (b * spb + s, off // width))

    y, st = pl.pallas_call(
        functools.partial(_glac_kernel, c=c), grid=(bsz, spb),
        in_specs=[col(hk, OD_Q), col(hk, OD_K), col(HALF_W, OD_V), col(HALF_W, OD_R),
                  col(LANE, OD_LRDT),
                  pl.BlockSpec((LANE, hk), lambda b, s: (0, 0)),
                  pl.BlockSpec((1, hk), lambda b, s: (0, 0)),
                  pl.BlockSpec((1, HALF_W), lambda b, s: (0, 0)),
                  pl.BlockSpec((1, H_C, V_C, K_C), lambda b, s: (b, 0, 0, 0)),
                  pl.BlockSpec((3 * c, c), lambda b, s: (0, 0))],
        out_specs=[pl.BlockSpec((rows, HALF_W), lambda b, s: (b * spb + s, 0)),
                   pl.BlockSpec((1, H_C, V_C, K_C), lambda b, s: (b, 0, 0, 0))],
        out_shape=[jax.ShapeDtypeStruct((t, HALF_W), BF16),
                   jax.ShapeDtypeStruct((bsz, H_C, V_C, K_C), F32)],
        scratch_shapes=[pltpu.VMEM((H_C, V_C, K_C), F32)],
        compiler_params=_cparams("parallel", "arbitrary"), name="mixer_c_gla")(
            proj, proj, proj, proj, proj,
            jnp.pad(gate_w2, ((0, LANE - GLA_RANK), (0, 0))),
            gate_b.reshape(1, hk),
            norm_g.reshape(1, HALF_W), s0t, _scan_mats(c))
    return y, jnp.swapaxes(st, -1, -2)


GW = HPG_D * P_D


def _ssd_kernel(z_ref, xbc_ref, lrdt_ref, cw_ref, cb_ref, dtb_ref, alog_ref, skip_ref, ng_ref,
                ex_ref, eye_ref, cm_ref, conv0_ref, s0_ref, y_ref, so_ref, s_ref, tail_ref, *, c):
    step = pl.program_id(1)

    @pl.when(step == 0)
    def _():
        s_ref[...] = s0_ref[0]
        tail_ref[...] = conv0_ref[0]

    x = xbc_ref[...]
    rows = x.shape[0]
    tail = tail_ref[...]
    sub8 = lax.broadcasted_iota(I32, (8, CONV_DIM), 0)
    conv = x * cw_ref[D_CONV - 1:D_CONV]
    for sh in range(1, D_CONV):
        rolled = pltpu.roll(x, sh, 0)
        head = jnp.where(sub8 < sh, pltpu.roll(tail, sh, 0), rolled[0:8])
        xk = jnp.concatenate([head, rolled[8:]], 0) if rows > 8 else head
        conv = conv + xk * cw_ref[D_CONV - 1 - sh:D_CONV - sh]
    tail_ref[...] = x[rows - 8:rows]
    xc = jax.nn.silu(conv + cb_ref[...])
    a = -jnp.exp(alog_ref[...])
    ex = ex_ref[...]
    tri = lax.broadcasted_iota(I32, (c, c), 1) <= lax.broadcasted_iota(I32, (c, c), 0)
    group = lambda arr, g, w: arr[:, g * w:(g + 1) * w]
    intra = []
    for ci in range(rows // c):
        rs = slice(ci * c, (ci + 1) * c)
        xs = xc[rs, 0:HALF_W]
        bm = xc[rs, HALF_W:HALF_W + G_D * N_D]
        cmat = xc[rs, HALF_W + G_D * N_D:CONV_DIM].astype(BF16)
        dt = _softplus(lrdt_ref[rs] + dtb_ref[...])
        la = dt * a
        sc = _dot_exact_l(cm_ref[...], la)
        cum = sc[0:c]
        scx = _dot_exact_r(sc, ex)
        cumx = scx[0:c]
        lastx = scx[c:2 * c]
        xdt = xs * _dot_exact_r(dt, ex)
        cum_t = sum(_dot_nt(eye_ref[...], p) for p in _split3(cum))
        cbs = [_dot_nt(group(cmat, g, N_D), group(bm, g, N_D).astype(BF16)) for g in range(G_D)]
        xdtb = xdt.astype(BF16)
        wmats = []
        for h in range(H_D):
            hl = DT_OFF + h
            diff = cum[:, hl:hl + 1] - cum_t[hl:hl + 1, :]
            dec = jnp.where(tri, jnp.exp(jnp.minimum(diff, 0.0)), 0.0)
            wmats.append((cbs[h // HPG_D] * dec).astype(BF16))
        y_intra = _cat([jnp.dot(wmats[h], xdtb[:, h * P_D:(h + 1) * P_D], preferred_element_type=F32)
                        for h in range(H_D)], 1)
        intra.append((y_intra + xs * skip_ref[...], cmat, jnp.exp(cumx),
                      bm.T.astype(BF16), (xdt * jnp.exp(lastx - cumx)).astype(BF16), jnp.exp(lastx[0:1])))
    for ci in range(rows // c):
        rs = slice(ci * c, (ci + 1) * c)
        y_local, cmat, e_in, bm_t, xw, e_last = intra[ci]
        sgs = [s_ref[g] for g in range(G_D)]
        inter = [jnp.dot(group(cmat, g, N_D), sgs[g].astype(BF16), preferred_element_type=F32)
                 for g in range(G_D)]
        upd = [jnp.dot(bm_t[g * N_D:(g + 1) * N_D], group(xw, g, GW), preferred_element_type=F32)
               for g in range(G_D)]
        for g in range(G_D):
            s_ref[g] = sgs[g] * group(e_last, g, GW) + upd[g]
        yg = (y_local + _cat(inter, 1) * e_in) * jax.nn.silu(z_ref[rs])
        rn = _cat([jnp.broadcast_to(lax.rsqrt(jnp.mean(jnp.square(group(yg, g, GW)), -1, keepdims=True) + EPS),
                                    (c, GW)) for g in range(G_D)], 1)
        y_ref[rs, :] = (yg * rn * ng_ref[...]).astype(BF16)

    @pl.when(step == pl.num_programs(1) - 1)
    def _():
        so_ref[0] = s_ref[...]


def _mixer_d(proj, conv_w, conv_b, dt_bias, a_log, skip, norm_g, s0, conv0, seq_len, rows):
    t = proj.shape[0]
    bsz = t // seq_len
    c = CHUNK if seq_len % CHUNK == 0 else seq_len
    spb = seq_len // rows
    s0g = s0.reshape(bsz, G_D, HPG_D, N_D, P_D).transpose(0, 1, 3, 2, 4).reshape(bsz, G_D, N_D, GW)
    conv0p = jnp.pad(conv0, ((0, 0), (8 - (D_CONV - 1), 0), (0, 0)))
    expand_np = np.zeros((LANE, HALF_W), np.float32)
    expand_np[DT_OFF:DT_OFF + H_D] = np.repeat(np.eye(H_D), P_D, axis=1)
    expand = jnp.asarray(expand_np, BF16)
    eye = jnp.asarray(np.eye(LANE), BF16)
    i = np.arange(c)[:, None]
    j = np.arange(c)[None, :]
    cm = jnp.asarray(np.concatenate([j <= i, np.ones((c, c), bool)], 0), BF16)
    skipx = jnp.repeat(skip, P_D)[None, :]
    lane_pad = lambda v: jnp.pad(v.reshape(1, H_D), ((0, 0), (DT_OFF, LANE - DT_OFF - H_D)))

    def col(width, off):
        return pl.BlockSpec((rows, width), lambda b, s: (b * spb + s, off // width))

    def full(shape):
        return pl.BlockSpec(shape, lambda b, s: (0,) * len(shape))

    y, st = pl.pallas_call(
        functools.partial(_ssd_kernel, c=c), grid=(bsz, spb),
        in_specs=[col(HALF_W, OD_Z),
                  col(CONV_DIM, OD_XBC),
                  col(LANE, OD_LRDT),
                  full((D_CONV, CONV_DIM)), full((1, CONV_DIM)), full((1, LANE)), full((1, LANE)),
                  full((1, HALF_W)), full((1, HALF_W)), full((LANE, HALF_W)), full((LANE, LANE)),
                  full((2 * c, c)),
                  pl.BlockSpec((1, 8, CONV_DIM), lambda b, s: (b, 0, 0)),
                  pl.BlockSpec((1, G_D, N_D, GW), lambda b, s: (b, 0, 0, 0))],
        out_specs=[pl.BlockSpec((rows, HALF_W), lambda b, s: (b * spb + s, 0)),
                   pl.BlockSpec((1, G_D, N_D, GW), lambda b, s: (b, 0, 0, 0))],
        out_shape=[jax.ShapeDtypeStruct((t, HALF_W), BF16),
                   jax.ShapeDtypeStruct((bsz, G_D, N_D, GW), F32)],
        scratch_shapes=[pltpu.VMEM((G_D, N_D, GW), F32), pltpu.VMEM((8, CONV_DIM), F32)],
        compiler_params=_cparams("parallel", "arbitrary"), name="mixer_d_ssd")(
            proj, proj, proj, conv_w, conv_b.reshape(1, CONV_DIM), lane_pad(dt_bias),
            lane_pad(a_log), skipx, norm_g.reshape(1, HALF_W), expand, eye, cm, conv0p, s0g)
    st = st.reshape(bsz, G_D, N_D, HPG_D, P_D).transpose(0, 1, 3, 2, 4).reshape(bsz, H_D, N_D, P_D)
    return y, st


def _layer_norm(hpre, g, b):
    mu = jnp.mean(hpre, -1, keepdims=True)
    d = hpre - mu
    var = jnp.mean(d * d, -1, keepdims=True)
    return d * lax.rsqrt(var + EPS) * g + b


U32 = jnp.uint32
HI_MASK = 0xFFFF0000


def _pack_halves(x):
    half = x.shape[1] // 2
    lo = lax.bitcast_convert_type(x[:, :half].astype(BF16).astype(F32), U32) >> 16
    hi = lax.bitcast_convert_type(x[:, half:].astype(BF16).astype(F32), U32) & U32(HI_MASK)
    return lo | hi


def _unpack_halves(w):
    return (lax.bitcast_convert_type(w << 16, F32), lax.bitcast_convert_type(w & U32(HI_MASK), F32))


def _emit_x(xn, o_ref, ob_ref, op_ref):
    o_ref[...] = xn
    ob_ref[...] = xn.astype(BF16)
    op_ref[...] = _pack_halves(xn)


def _x_out(t, tm):
    row = lambda width: pl.BlockSpec((tm, width), lambda i, *_: (i, 0))
    return ([row(D_MODEL), row(D_MODEL), row(HALF_W)],
            [jax.ShapeDtypeStruct((t, D_MODEL), F32), jax.ShapeDtypeStruct((t, D_MODEL), BF16),
             jax.ShapeDtypeStruct((t, HALF_W), U32)])


def _outproj_kernel(ya_ref, yb_ref, x_ref, w_ref, g_ref, b_ref, o_ref, ob_ref, op_ref):
    acc = jnp.dot(ya_ref[...], w_ref[0:HALF_W, :], preferred_element_type=F32)
    acc = acc + jnp.dot(yb_ref[...], w_ref[HALF_W:D_MODEL, :], preferred_element_type=F32)
    _emit_x(_layer_norm(ALPHA * x_ref[...] + acc, g_ref[...], b_ref[...]), o_ref, ob_ref, op_ref)


def _out_proj_ln(ya, yb, x, w, g, b, tm):
    t = x.shape[0]
    row = lambda width: pl.BlockSpec((tm, width), lambda i: (i, 0))
    out_specs, out_shape = _x_out(t, tm)
    return pl.pallas_call(
        _outproj_kernel, grid=(t // tm,),
        in_specs=[row(HALF_W), row(HALF_W), row(D_MODEL),
                  pl.BlockSpec((D_MODEL, D_MODEL), lambda i: (0, 0)),
                  pl.BlockSpec((1, D_MODEL), lambda i: (0, 0)),
                  pl.BlockSpec((1, D_MODEL), lambda i: (0, 0))],
        out_specs=out_specs, out_shape=out_shape,
        compiler_params=_cparams("parallel"), name="out_proj_ln")(
            ya, yb, x, w, g.reshape(1, D_MODEL), b.reshape(1, D_MODEL))


RANK_BITS = 20
RANK_MASK = (1 << RANK_BITS) - 1


def _router_kernel(x_ref, wr_ref, bias_ref, tri_ref, code_ref, gw_ref, cnt_ref, run_ref):
    step = pl.program_id(0)

    @pl.when(step == 0)
    def _():
        run_ref[...] = jnp.zeros_like(run_ref)

    tm = x_ref.shape[0]
    logits = _dot_nt(wr_ref[...], x_ref[...])
    scores = jax.nn.sigmoid(logits)
    sel = scores + bias_ref[:, 0:1]
    neg = -jnp.inf
    sub = lax.broadcasted_iota(I32, (GROUP_SIZE, tm), 0).astype(F32)
    gsc = []
    for g in range(N_GROUPS):
        blk = sel[g * GROUP_SIZE:(g + 1) * GROUP_SIZE]
        m1 = jnp.max(blk, 0, keepdims=True)
        i1 = jnp.min(jnp.where(blk == m1, sub, float(GROUP_SIZE)), 0, keepdims=True)
        m2 = jnp.max(jnp.where(sub == i1, neg, blk), 0, keepdims=True)
        gsc.append(m1 + m2)
    cur = jnp.concatenate(gsc, 0)
    gio = lax.broadcasted_iota(I32, (N_GROUPS, tm), 0).astype(F32)
    gmask = jnp.zeros((N_GROUPS, tm), F32)
    for _ in range(TOPK_GROUPS):
        m = jnp.max(cur, 0, keepdims=True)
        i = jnp.min(jnp.where(cur == m, gio, float(N_GROUPS)), 0, keepdims=True)
        pick = gio == i
        gmask = jnp.where(pick, 1.0, gmask)
        cur = jnp.where(pick, neg, cur)
    emask = jnp.concatenate(
        [jnp.broadcast_to(gmask[g:g + 1], (GROUP_SIZE, tm)) for g in range(N_GROUPS)], 0)
    cur = jnp.where(emask > 0.5, sel, neg)
    eio = lax.broadcasted_iota(I32, (N_EXPERTS, tm), 0).astype(F32)
    member = jnp.zeros((N_EXPERTS, tm), F32)
    idxs, scs = [], []
    for _ in range(TOP_K):
        m = jnp.max(cur, 0, keepdims=True)
        i = jnp.min(jnp.where(cur == m, eio, float(N_EXPERTS)), 0, keepdims=True)
        pick = eio == i
        idxs.append(i)
        scs.append(jnp.sum(jnp.where(pick, scores, 0.0), 0, keepdims=True))
        member = jnp.where(pick, 1.0, member)
        cur = jnp.where(pick, neg, cur)
    idx = jnp.concatenate(idxs, 0)
    sc = jnp.concatenate(scs, 0)
    gw_ref[...] = sc / jnp.sum(sc, 0, keepdims=True) * ROUTE_SCALE
    before = jnp.dot(member.astype(BF16), tri_ref[...], preferred_element_type=F32) + run_ref[:, 0:1]
    ranks = [jnp.sum(jnp.where(eio == idxs[k], before, 0.0), 0, keepdims=True) for k in range(TOP_K)]
    rank = jnp.concatenate(ranks, 0).astype(I32)
    code_ref[...] = idx.astype(I32) * (1 << RANK_BITS) + rank
    run_ref[...] = run_ref[...] + jnp.sum(member, 1, keepdims=True)
    cnt_ref[...] = run_ref[...]


def _router(x, w_router, bias, tm):
    t = x.shape[0]
    assert t * TOP_K < (1 << RANK_BITS)
    tri = jnp.asarray(np.arange(tm)[:, None] < np.arange(tm)[None, :], BF16)
    kt = lambda dt: jax.ShapeDtypeStruct((TOP_K, t), dt)
    return pl.pallas_call(
        _router_kernel, grid=(t // tm,),
        in_specs=[pl.BlockSpec((tm, D_MODEL), lambda i: (i, 0)),
                  pl.BlockSpec((N_EXPERTS, D_MODEL), lambda i: (0, 0)),
                  pl.BlockSpec((N_EXPERTS, LANE), lambda i: (0, 0)),
                  pl.BlockSpec((tm, tm), lambda i: (0, 0))],
        out_specs=[pl.BlockSpec((TOP_K, tm), lambda i: (0, i)),
                   pl.BlockSpec((TOP_K, tm), lambda i: (0, i)),
                   pl.BlockSpec((N_EXPERTS, LANE), lambda i: (0, 0))],
        out_shape=[kt(I32), kt(F32), jax.ShapeDtypeStruct((N_EXPERTS, LANE), F32)],
        scratch_shapes=[pltpu.VMEM((N_EXPERTS, LANE), F32)],
        compiler_params=_cparams("arbitrary"), name="moe_router")(
            x, w_router.T.astype(BF16), jnp.broadcast_to(bias[:, None], (N_EXPERTS, LANE)), tri)


def _slots_kernel(start_ref, code_ref, pos_ref):
    code = code_ref[...]
    e = lax.shift_right_logical(code, RANK_BITS)
    r = code & RANK_MASK
    pos = r
    for x in range(N_EXPERTS):
        pos = jnp.where(e == x, r + start_ref[x], pos)
    pos_ref[...] = pos


def _slots(code, slot_start):
    k, t = code.shape
    tile = min(t, 4096)
    gs = pltpu.PrefetchScalarGridSpec(
        num_scalar_prefetch=1, grid=(t // tile,),
        in_specs=[pl.BlockSpec((k, tile), lambda i, *_: (0, i))],
        out_specs=pl.BlockSpec((k, tile), lambda i, *_: (0, i)))
    return pl.pallas_call(_slots_kernel, grid_spec=gs, out_shape=jax.ShapeDtypeStruct((k, t), I32),
                          compiler_params=_cparams("parallel"), name="moe_slots")(slot_start, code)


def _dispatch_kernel(fill_ref, pos_ref, x_ref, xs_ref, zrow_ref, sem, *, eps):
    step = pl.program_id(0)
    tm = x_ref.shape[0]
    zrow_ref[...] = jnp.zeros_like(zrow_ref)

    def copy(r, k):
        return pltpu.make_async_copy(x_ref.at[pl.ds(r, 1)], xs_ref.at[pl.ds(pos_ref[0, 0, k * tm + r], 1)], sem)

    def zero_copy(row):
        return pltpu.make_async_copy(zrow_ref.at[pl.ds(0, 1)], xs_ref.at[pl.ds(row, 1)], sem)

    def start(r, carry):
        for k in range(TOP_K):
            copy(r, k).start()
        return carry

    def wait(r, carry):
        for k in range(TOP_K):
            copy(r, k).wait()
        return carry

    def fill_expert(e, total):
        base = fill_ref[0, e]
        n = fill_ref[1, e]

        def go(r, c):
            zero_copy(base + r).start()
            return c

        lax.fori_loop(0, n, go, 0)
        return total + n

    def fill_wait(r, c):
        zero_copy(0).wait()
        return c

    lax.fori_loop(0, tm, start, 0)
    e0 = jnp.minimum(step * eps, N_EXPERTS)
    e1 = jnp.minimum(e0 + eps, N_EXPERTS)
    n_fill = lax.fori_loop(e0, e1, fill_expert, 0)
    lax.fori_loop(0, tm, wait, 0)
    lax.fori_loop(0, n_fill, fill_wait, 0)


def _dispatch(xp, pos_tiles, fill, nslot, tm):
    t = xp.shape[0]
    steps = t // tm
    eps = -(-N_EXPERTS // steps)
    gs = pltpu.PrefetchScalarGridSpec(
        num_scalar_prefetch=1, grid=(steps,),
        in_specs=[pl.BlockSpec((1, 1, TOP_K * tm), lambda i, *_: (i, 0, 0), memory_space=pltpu.SMEM),
                  pl.BlockSpec((tm, HALF_W), lambda i, *_: (i, 0))],
        out_specs=pl.BlockSpec(memory_space=pl.ANY),
        scratch_shapes=[pltpu.VMEM((8, HALF_W), xp.dtype), pltpu.SemaphoreType.DMA(())])
    return pl.pallas_call(
        functools.partial(_dispatch_kernel, eps=eps), grid_spec=gs,
        out_shape=jax.ShapeDtypeStruct((nslot, HALF_W), xp.dtype),
        compiler_params=_cparams("arbitrary"), name="moe_dispatch")(fill, pos_tiles, xp)


def _expert_kernel(be_ref, nu_ref, x_ref, wg_ref, wu_ref, wd_ref, o_ref, *, packed):
    del be_ref

    @pl.when(pl.program_id(0) < nu_ref[0])
    def _():
        if packed:
            lo, hi = _unpack_halves(x_ref[...])
            lo = lo.astype(BF16)
            hi = hi.astype(BF16)
            hg = (jnp.dot(lo, wg_ref[0, 0:HALF_W, :], preferred_element_type=F32)
                  + jnp.dot(hi, wg_ref[0, HALF_W:D_MODEL, :], preferred_element_type=F32))
            hu = (jnp.dot(lo, wu_ref[0, 0:HALF_W, :], preferred_element_type=F32)
                  + jnp.dot(hi, wu_ref[0, HALF_W:D_MODEL, :], preferred_element_type=F32))
        else:
            x = x_ref[...]
            hg = jnp.dot(x, wg_ref[0], preferred_element_type=F32)
            hu = jnp.dot(x, wu_ref[0], preferred_element_type=F32)
        hb = (jax.nn.silu(hg) * hu).astype(BF16)
        out = jnp.dot(hb, wd_ref[0], preferred_element_type=F32)
        o_ref[...] = _pack_halves(out) if packed else out.astype(BF16)

    @pl.when(pl.program_id(0) >= nu_ref[0])
    def _():
        o_ref[...] = jnp.zeros_like(o_ref)


def _expert_ffn(xs, blk_e, nused, wg, wu, wd, layer, bm, packed):
    nslot, width = xs.shape
    nb = nslot // bm
    de = wg.shape[-1]
    gs = pltpu.PrefetchScalarGridSpec(
        num_scalar_prefetch=2, grid=(nb,),
        in_specs=[pl.BlockSpec((bm, width), lambda b, be, nu: (jnp.minimum(b, nu[0] - 1), 0)),
                  pl.BlockSpec((None, 1, D_MODEL, de), lambda b, be, nu: (layer, be[b], 0, 0)),
                  pl.BlockSpec((None, 1, D_MODEL, de), lambda b, be, nu: (layer, be[b], 0, 0)),
                  pl.BlockSpec((None, 1, de, D_MODEL), lambda b, be, nu: (layer, be[b], 0, 0))],
        out_specs=pl.BlockSpec((bm, width), lambda b, be, nu: (b, 0)))
    return pl.pallas_call(
        functools.partial(_expert_kernel, packed=packed), grid_spec=gs,
        out_shape=jax.ShapeDtypeStruct((nslot, width), xs.dtype),
        compiler_params=_cparams("arbitrary"), name="moe_expert_ffn")(blk_e, nused, xs, wg, wu, wd)


def _combine_kernel(pos_ref, nxt_ref, gw_ref, x_ref, sh_ref, eo_ref, g_ref, b_ref, o_ref, ob_ref, op_ref,
                    buf, sems):
    step = pl.program_id(0)
    nsteps = pl.num_programs(0)
    tm = x_ref.shape[0]
    cur = lax.rem(step, 2)

    def copy(idx_ref, slot, r, k):
        return pltpu.make_async_copy(eo_ref.at[pl.ds(idx_ref[0, 0, k * tm + r], 1)],
                                     buf.at[slot, k, pl.ds(r, 1)], sems.at[slot])

    def gather(idx_ref, slot):
        def start(r, carry):
            for k in range(TOP_K):
                copy(idx_ref, slot, r, k).start()
            return carry
        lax.fori_loop(0, tm, start, 0)

    @pl.when(step == 0)
    def _():
        gather(pos_ref, 0)

    @pl.when(step + 1 < nsteps)
    def _():
        gather(nxt_ref, 1 - cur)

    def wait(r, carry):
        for k in range(TOP_K):
            copy(pos_ref, cur, r, k).wait()
        return carry

    lax.fori_loop(0, tm, wait, 0)
    sh = sh_ref[...].astype(F32)
    lo_acc = sh[:, :HALF_W]
    hi_acc = sh[:, HALF_W:]
    for k in range(TOP_K):
        lo, hi = _unpack_halves(buf[cur, k])
        wk = gw_ref[:, k:k + 1]
        lo_acc = lo_acc + lo * wk
        hi_acc = hi_acc + hi * wk
    ffn = jnp.concatenate([lo_acc, hi_acc], 1)
    _emit_x(_layer_norm(ALPHA * x_ref[...] + ffn, g_ref[...], b_ref[...]), o_ref, ob_ref, op_ref)


def _combine_ln(pos_tiles, gw_t, x, sh, eo, g, b, tm):
    t = x.shape[0]
    steps = t // tm
    row = lambda width: pl.BlockSpec((tm, width), lambda i: (i, 0))
    const = pl.BlockSpec((1, D_MODEL), lambda i: (0, 0))
    idx = lambda f: pl.BlockSpec((1, 1, TOP_K * tm), f, memory_space=pltpu.SMEM)
    out_specs, out_shape = _x_out(t, tm)
    return pl.pallas_call(
        _combine_kernel, grid=(steps,),
        in_specs=[idx(lambda i: (i, 0, 0)), idx(lambda i: (jnp.minimum(i + 1, steps - 1), 0, 0)),
                  row(TOP_K), row(D_MODEL), row(D_MODEL), pl.BlockSpec(memory_space=pl.ANY), const, const],
        out_specs=out_specs, out_shape=out_shape,
        scratch_shapes=[pltpu.VMEM((2, TOP_K, tm, HALF_W), U32), pltpu.SemaphoreType.DMA((2,))],
        compiler_params=_cparams("arbitrary"), name="moe_combine_ln")(
            pos_tiles, pos_tiles, gw_t, x, sh, eo, g.reshape(1, D_MODEL), b.reshape(1, D_MODEL))


def _moe_ln(x, xb, xp, w_router, bias, wg, wu, wd, wsg, wsu, wsd, layer, g, b, tm, bm):
    t = x.shape[0]
    code, gw, cnt = _router(xb, w_router, bias, tm)
    counts = cnt[:, 0].astype(I32)
    padded = (counts + bm - 1) // bm * bm
    pad_end = jnp.cumsum(padded)
    slot_start = pad_end - padded
    fill = jnp.stack([slot_start + counts, padded - counts])
    nb = -(-(t * TOP_K + N_EXPERTS * (bm - 1)) // bm)
    blk_first = jnp.arange(nb, dtype=I32) * bm
    blk_e = jnp.minimum(jnp.sum((pad_end[None, :] <= blk_first[:, None]).astype(I32), 1), N_EXPERTS - 1)
    nused = (pad_end[-1] // bm).astype(I32).reshape(1)
    pos = _slots(code, slot_start)
    pos_tiles = pos.reshape(TOP_K, t // tm, tm).transpose(1, 0, 2).reshape(t // tm, 1, TOP_K * tm)
    xs = _dispatch(xp, pos_tiles, fill, nb * bm, tm)
    eo = _expert_ffn(xs, blk_e, nused, wg, wu, wd, layer, bm, True)
    shared = _expert_ffn(xb, jnp.zeros((t // tm,), I32), jnp.full((1,), t // tm, I32),
                         wsg[:, None], wsu[:, None], wsd[:, None], layer, tm, False)
    return _combine_ln(pos_tiles, gw.T, x, shared, eo, g, b, tm)


def _odd_w_in(w):
    sizes = (H_C * K_C, H_C * K_C, H_C * V_C, GLA_RANK, H_C * V_C, HALF_W, CONV_DIM, H_D)
    q, k, v, lr, r, z, xbc, dt = jnp.split(w, [int(s) for s in np.cumsum(sizes)[:-1]], axis=-1)
    pad = jnp.zeros((w.shape[0], OD_XBC - (OD_LRDT + GLA_RANK + H_D)), w.dtype)
    return jnp.concatenate([q, k, v, r, z, lr, dt, pad, xbc], -1)


def _trunk(x3, hgrn0, gla0, ssm0, conv0, p, cfg):
    bsz, seq_len, _ = x3.shape
    t = bsz * seq_len
    x = x3.reshape(t, D_MODEL)
    xb = None
    tm, rows, bm = cfg["tm"], cfg["rows"], cfg["bm"]
    outs = {}
    for l in range(DEPTH):
        j = l // 2
        if l % 2 == 0:
            proj = _proj(x if xb is None else xb, p["ev_w_in"][j], cfg["ptm"], cfg["ptn"])
            ya, v_rows = _mixer_a(proj, p["ev_a_ln_g"][j], p["ev_a_ln_b"][j], p["ev_a_ws"][j],
                                  p["ev_a_bs"][j], seq_len, cfg["arows"], cfg["emit_v"])
            yb, s_b = _mixer_b(proj, p["hgrn_lb_logits"], p["ev_b_norm_g"][j], hgrn0[j], seq_len, rows, l)
            outs["v_rows"] = v_rows
            outs["hgrn"] = s_b
            w_out = p["ev_w_out"][j]
        else:
            proj = _proj(x if xb is None else xb, p["od_w_in"][j], cfg["ptm"], cfg["ptn"])
            ya, s_c = _mixer_c(proj, p["od_c_gate_w2"][j], p["od_c_gate_b"][j], p["od_c_norm_g"][j],
                               gla0[j], seq_len, rows)
            yb, s_d = _mixer_d(proj, p["od_d_conv_w"][j], p["od_d_conv_b"][j], p["od_d_dt_bias"][j],
                               p["od_d_a_log"][j], p["od_d_skip"][j], p["od_d_norm_g"][j],
                               ssm0[j], conv0[j], seq_len, rows)
            keep = min(D_CONV - 1, seq_len)
            xbc = proj.reshape(bsz, seq_len, OD_PAD)[:, seq_len - keep:, OD_XBC:OD_XBC + CONV_DIM]
            outs["conv"] = jnp.concatenate([conv0[j], xbc], 1)[:, -(D_CONV - 1):]
            outs["gla"] = s_c
            outs["ssm"] = s_d
            w_out = p["od_w_out"][j]
        x, xb, xp = _out_proj_ln(ya, yb, x, w_out, p["ln1_g"][l], p["ln1_b"][l], tm)
        x, xb, _ = _moe_ln(x, xb, xp, p["moe_w_router"][l], p["moe_router_bias"][l], p["moe_w_gate"],
                           p["moe_w_up"], p["moe_w_down"], p["moe_ws_gate"], p["moe_ws_up"],
                           p["moe_ws_down"], l, p["ln2_g"][l], p["ln2_b"][l], tm, bm)
    return x.reshape(bsz, seq_len, D_MODEL), outs


def _config(bsz, seq_len):
    t = bsz * seq_len
    if seq_len % CHUNK == 0:
        return dict(tm=256, rows=min(256, seq_len), arows=min(256, seq_len), bm=512,
                    ptm=min(1024, t), ptn=512, emit_v=False)
    return dict(tm=t, rows=seq_len, arows=t, bm=128, ptm=t, ptn=512, emit_v=True)


def kernel(x_prompt, x_sample, state_b_hgrn, state_c_gla, state_d_ssm, state_d_conv, ev_w_in, ev_a_ln_g, ev_a_ln_b, ev_a_ws, ev_a_bs, ev_b_norm_g, ev_w_out, hgrn_lb_logits, od_w_in, od_c_gate_w2, od_c_gate_b, od_c_norm_g, od_d_conv_w, od_d_conv_b, od_d_dt_bias, od_d_a_log, od_d_skip, od_d_norm_g, od_w_out, ln1_g, ln1_b, ln2_g, ln2_b, moe_w_router, moe_router_bias, moe_w_gate, moe_w_up, moe_w_down, moe_ws_gate, moe_ws_up, moe_ws_down):
    p = dict(
        ev_w_in=ev_w_in.astype(BF16), ev_a_ln_g=ev_a_ln_g, ev_a_ln_b=ev_a_ln_b, ev_a_ws=ev_a_ws,
        ev_a_bs=ev_a_bs, ev_b_norm_g=ev_b_norm_g, ev_w_out=ev_w_out.astype(BF16),
        hgrn_lb_logits=hgrn_lb_logits,
        od_w_in=jnp.stack([_odd_w_in(od_w_in[j]) for j in range(od_w_in.shape[0])]).astype(BF16),
        od_c_gate_w2=od_c_gate_w2, od_c_gate_b=od_c_gate_b, od_c_norm_g=od_c_norm_g,
        od_d_conv_w=od_d_conv_w, od_d_conv_b=od_d_conv_b, od_d_dt_bias=od_d_dt_bias,
        od_d_a_log=od_d_a_log, od_d_skip=od_d_skip, od_d_norm_g=od_d_norm_g,
        od_w_out=od_w_out.astype(BF16), ln1_g=ln1_g, ln1_b=ln1_b, ln2_g=ln2_g, ln2_b=ln2_b,
        moe_w_router=moe_w_router, moe_router_bias=moe_router_bias,
        moe_w_gate=moe_w_gate.astype(BF16), moe_w_up=moe_w_up.astype(BF16),
        moe_w_down=moe_w_down.astype(BF16), moe_ws_gate=moe_ws_gate.astype(BF16),
        moe_ws_up=moe_ws_up.astype(BF16), moe_ws_down=moe_ws_down.astype(BF16))
    bp, lp, _ = x_prompt.shape
    n_even = state_b_hgrn.shape[0]
    n_odd = state_c_gla.shape[0]
    zeros = lambda n, *s: jnp.zeros((n, bp) + s, F32)
    y_p, o_p = _trunk(x_prompt, zeros(n_even, H_B, K_B, V_B), zeros(n_odd, H_C, K_C, V_C),
                      zeros(n_odd, H_D, N_D, P_D), zeros(n_odd, D_CONV - 1, CONV_DIM), p,
                      _config(bp, lp))
    bs, ls, _ = x_sample.shape
    y_s, o_s = _trunk(x_sample, state_b_hgrn, state_c_gla, state_d_ssm, state_d_conv, p,
                      _config(bs, ls))
    a_v = o_s["v_rows"].reshape(1, bs, ls, HALF_W)
    return (y_p, y_s, a_v, o_p["hgrn"][None], o_s["hgrn"][None], o_p["gla"][None], o_s["gla"][None],
            o_p["ssm"][None], o_s["ssm"][None], o_p["conv"][None], o_s["conv"][None])
```

```python
import functools
import math

import jax
import jax.numpy as jnp
import numpy as np
from jax import lax
from jax.experimental import pallas as pl
from jax.experimental.pallas import tpu as pltpu

F32 = jnp.float32
BF16 = jnp.bfloat16
I32 = jnp.int32

D_MODEL = 2048
DEPTH = 2
CHUNK = 64
SUB = 16
HALF_W = D_MODEL // 2
A_CHUNK = 128
H_A = 4
DA = HALF_W // H_A
H_B = 8
K_B = 128
V_B = HALF_W // H_B
H_C = 4
V_C = HALF_W // H_C
K_C = V_C // 2
GLA_RANK = 16
GLA_TAU = 16.0
P_D = 64
H_D = HALF_W // P_D
G_D = 2
HPG_D = H_D // G_D
N_D = 128
D_CONV = 4
CONV_DIM = HALF_W + 2 * G_D * N_D
N_EXPERTS = 64
N_GROUPS = 8
GROUP_SIZE = N_EXPERTS // N_GROUPS
TOPK_GROUPS = 4
TOP_K = 8
D_EXPERT = 512
ROUTE_SCALE = 2.5
ALPHA = (2 * DEPTH) ** 0.25
EPS = 1e-5
LANE = 128
VMEM_LIMIT = 56 * 1024 * 1024

OD_Q, OD_K, OD_V, OD_R, OD_Z, OD_LRDT, OD_XBC = 0, 512, 1024, 2048, 3072, 4096, 4608
OD_PAD = 6144
DT_OFF = GLA_RANK


def _cparams(*sem):
    return pltpu.CompilerParams(dimension_semantics=sem, vmem_limit_bytes=VMEM_LIMIT)


def _split3(x):
    hi = x.astype(BF16)
    r = x - hi.astype(F32)
    mid = r.astype(BF16)
    lo = (r - mid.astype(F32)).astype(BF16)
    return hi, mid, lo


def _split2(x):
    hi = x.astype(BF16)
    return hi, (x - hi.astype(F32)).astype(BF16)


def _dot_exact_l(a_bf16, x):
    return sum(jnp.dot(a_bf16, p, preferred_element_type=F32) for p in _split3(x))


def _dot_exact_r(x, b_bf16):
    return sum(jnp.dot(p, b_bf16, preferred_element_type=F32) for p in _split3(x))


def _dot_nt(a, b):
    return lax.dot_general(a, b, (((1,), (1,)), ((), ())), preferred_element_type=F32)


def _softplus(x):
    return jnp.maximum(x, 0.0) + jnp.log1p(jnp.exp(-jnp.abs(x)))


def _proj_kernel(x_ref, w_ref, o_ref, *scratch):
    if scratch:
        xb_ref, = scratch

        @pl.when(pl.program_id(1) == 0)
        def _():
            xb_ref[...] = x_ref[...].astype(BF16)
        x = xb_ref[...]
    else:
        x = x_ref[...]
    o_ref[...] = jnp.dot(x, w_ref[...], preferred_element_type=F32)


def _proj(x, w, tm, tn):
    t, k = x.shape
    n = w.shape[1]
    scratch = [] if x.dtype == BF16 else [pltpu.VMEM((tm, k), BF16)]
    return pl.pallas_call(
        _proj_kernel, grid=(t // tm, n // tn),
        in_specs=[pl.BlockSpec((tm, k), lambda i, j: (i, 0)),
                  pl.BlockSpec((k, tn), lambda i, j: (0, j))],
        out_specs=pl.BlockSpec((tm, tn), lambda i, j: (i, j)),
        out_shape=jax.ShapeDtypeStruct((t, n), F32),
        scratch_shapes=scratch, compiler_params=_cparams("parallel", "arbitrary"),
        name="in_proj")(x, w)


def _mixa_kernel(u_ref, v_ref, lng_ref, lnb_ref, w_ref, bs_ref, ya_ref, *vrows, ca):
    gu = jax.nn.gelu(u_ref[...])
    gv = jax.nn.gelu(v_ref[...])
    rows = gu.shape[0]
    for h in range(H_A):
        sl = slice(h * DA, (h + 1) * DA)
        vh = gv[:, sl]
        mu = jnp.mean(vh, -1, keepdims=True)
        d = vh - mu
        var = jnp.mean(d * d, -1, keepdims=True)
        vn = d * lax.rsqrt(var + EPS) * lng_ref[h] + lnb_ref[h]
        if vrows:
            vrows[0][:, sl] = vn
        vnb = vn.astype(BF16)
        for c in range(rows // ca):
            rs = slice(c * ca, (c + 1) * ca)
            s = jnp.dot(w_ref[h], vnb[rs], preferred_element_type=F32) + bs_ref[h]
            ya_ref[rs, sl] = (gu[rs, sl] * s).astype(BF16)


def _mixer_a(proj, lng, lnb, ws, bs, seq_len, rows, emit_v):
    t = proj.shape[0]
    ca = min(A_CHUNK, seq_len)
    pos = np.arange(A_CHUNK)
    mask = (pos[None, :] // CHUNK) <= (pos[:, None] // CHUNK)
    w = (ws * mask)[:, :ca, :ca].astype(BF16)
    bsb = jnp.broadcast_to(bs[:, :ca, None], (H_A, ca, DA)).astype(F32)
    nb = HALF_W // HALF_W
    out_shape = [jax.ShapeDtypeStruct((t, HALF_W), BF16)]
    out_specs = [pl.BlockSpec((rows, HALF_W), lambda i: (i, 0))]
    if emit_v:
        out_shape.append(jax.ShapeDtypeStruct((t, HALF_W), F32))
        out_specs.append(pl.BlockSpec((rows, HALF_W), lambda i: (i, 0)))
    del nb
    res = pl.pallas_call(
        functools.partial(_mixa_kernel, ca=ca), grid=(t // rows,),
        in_specs=[pl.BlockSpec((rows, HALF_W), lambda i: (i, 0)),
                  pl.BlockSpec((rows, HALF_W), lambda i: (i, 1)),
                  pl.BlockSpec((H_A, 1, DA), lambda i: (0, 0, 0)),
                  pl.BlockSpec((H_A, 1, DA), lambda i: (0, 0, 0)),
                  pl.BlockSpec((H_A, ca, ca), lambda i: (0, 0, 0)),
                  pl.BlockSpec((H_A, ca, DA), lambda i: (0, 0, 0))],
        out_specs=out_specs, out_shape=out_shape,
        compiler_params=_cparams("parallel"), name="mixer_a")(
            proj, proj, lng[:, None, :], lnb[:, None, :], w, bsb)
    return res if emit_v else (res[0], None)


def _scan_mats(c):
    i = np.arange(c)[:, None]
    j = np.arange(c)[None, :]
    tril = (j <= i)
    local = tril & ((i // SUB) == (j // SUB))
    ones = np.ones((c, c), bool)
    return jnp.asarray(np.concatenate([local, tril, ones], 0), BF16)


def _cat(parts, axis):
    return parts[0] if len(parts) == 1 else jnp.concatenate(parts, axis)


def _gla_intra(q, k, v, sc, c, nh, kd, vd):
    lc, cum, last = sc[0:c], sc[c:2 * c], sc[2 * c:3 * c]
    ns = c // SUB
    q_hi, q_lo = _split2(q * jnp.exp(lc))
    pre = cum - lc
    vb = v.astype(BF16)
    att = []
    for blk in range(ns):
        n = SUB * (blk + 1)
        r0 = SUB * blk
        k_hi, k_lo = _split2(k[:n] * jnp.exp(pre[r0:r0 + 1] - cum[:n]))
        rq = slice(r0, r0 + SUB)
        for h in range(nh):
            ks = slice(h * kd, (h + 1) * kd)
            att.append(_dot_nt(q_hi[rq, ks], k_hi[:, ks])
                       + (_dot_nt(q_hi[rq, ks], k_lo[:, ks]) + _dot_nt(q_lo[rq, ks], k_hi[:, ks])))
    attb = []
    for blk in range(ns):
        n = SUB * (blk + 1)
        keep = (lax.broadcasted_iota(I32, (SUB, n), 1) - SUB * blk) <= lax.broadcasted_iota(I32, (SUB, n), 0)
        attb += [jnp.where(keep, att[blk * nh + h], 0.0).astype(BF16) for h in range(nh)]
    rows_out = []
    for blk in range(ns):
        n = SUB * (blk + 1)
        rows_out.append(_cat([jnp.dot(attb[blk * nh + h], vb[:n, h * vd:(h + 1) * vd],
                                      preferred_element_type=F32) for h in range(nh)], 1))
    qc = (q * jnp.exp(cum)).astype(BF16)
    kc = (k * jnp.exp(last - cum)).astype(BF16)
    return _cat(rows_out, 0), qc, kc, v.T.astype(BF16), jnp.exp(last[0:1])


def _gla_state(intra, s_ref, nh, kd, vd):
    o_intra, qc, kc, vt, e_last = intra
    sts = [s_ref[h] for h in range(nh)]
    inter = [_dot_nt(qc[:, h * kd:(h + 1) * kd], sts[h].astype(BF16)) for h in range(nh)]
    upd = [jnp.dot(vt[h * vd:(h + 1) * vd], kc[:, h * kd:(h + 1) * kd], preferred_element_type=F32)
           for h in range(nh)]
    for h in range(nh):
        s_ref[h] = sts[h] * e_last[:, h * kd:(h + 1) * kd] + upd[h]
    return o_intra + _cat(inter, 1)


def _hgrn_kernel(q_ref, f_ref, i_ref, g_ref, lbl_ref, ng_ref, s0_ref, cm_ref, y_ref, so_ref, s_ref,
                 *, c, layer):
    step = pl.program_id(1)

    @pl.when(step == 0)
    def _():
        s_ref[...] = s0_ref[0]

    lg = lbl_ref[...]
    ex = jnp.exp(lg - jnp.max(lg, 0, keepdims=True))
    sm = ex / jnp.sum(ex, 0, keepdims=True)
    lb = jnp.sum(sm[:layer + 1], 0, keepdims=True)
    rows = q_ref.shape[0]
    intra = []
    for ci in range(rows // c):
        rs = slice(ci * c, (ci + 1) * c)
        f = lb + (1.0 - lb) * jax.nn.sigmoid(f_ref[rs])
        g = jnp.log(f)
        sc = _dot_exact_l(cm_ref[...], g)
        intra.append(_gla_intra(jax.nn.silu(q_ref[rs]), 1.0 - f, i_ref[rs], sc, c, H_B, K_B, V_B))
    for ci in range(rows // c):
        rs = slice(ci * c, (ci + 1) * c)
        o = _gla_state(intra[ci], s_ref, H_B, K_B, V_B)
        rn = _cat([jnp.broadcast_to(lax.rsqrt(jnp.mean(jnp.square(o[:, h * V_B:(h + 1) * V_B]), -1,
                                                        keepdims=True) + EPS), (c, V_B))
                   for h in range(H_B)], 1)
        y_ref[rs, :] = (o * rn * ng_ref[...] * jax.nn.sigmoid(g_ref[rs])).astype(BF16)

    @pl.when(step == pl.num_programs(1) - 1)
    def _():
        so_ref[0] = s_ref[...]


def _mixer_b(proj, lb_logits, norm_g, s0, seq_len, rows, layer):
    t = proj.shape[0]
    bsz = t // seq_len
    c = CHUNK if seq_len % CHUNK == 0 else seq_len
    spb = seq_len // rows
    s0t = jnp.swapaxes(s0, -1, -2)
    col = lambda j: pl.BlockSpec((rows, HALF_W), lambda b, s, j=j: (b * spb + s, j))
    nl = lb_logits.shape[0]
    y, st = pl.pallas_call(
        functools.partial(_hgrn_kernel, c=c, layer=layer), grid=(bsz, spb),
        in_specs=[col(2), col(3), col(4), col(5),
                  pl.BlockSpec((nl, HALF_W), lambda b, s: (0, 0)),
                  pl.BlockSpec((1, HALF_W), lambda b, s: (0, 0)),
                  pl.BlockSpec((1, H_B, V_B, K_B), lambda b, s: (b, 0, 0, 0)),
                  pl.BlockSpec((3 * c, c), lambda b, s: (0, 0))],
        out_specs=[pl.BlockSpec((rows, HALF_W), lambda b, s: (b * spb + s, 0)),
                   pl.BlockSpec((1, H_B, V_B, K_B), lambda b, s: (b, 0, 0, 0))],
        out_shape=[jax.ShapeDtypeStruct((t, HALF_W), BF16),
                   jax.ShapeDtypeStruct((bsz, H_B, V_B, K_B), F32)],
        scratch_shapes=[pltpu.VMEM((H_B, V_B, K_B), F32)],
        compiler_params=_cparams("parallel", "arbitrary"), name="mixer_b_hgrn")(
            proj, proj, proj, proj, lb_logits, norm_g.reshape(1, HALF_W), s0t, _scan_mats(c))
    return y, jnp.swapaxes(st, -1, -2)


def _glac_kernel(q_ref, k_ref, v_ref, r_ref, lrdt_ref, w2_ref, gb_ref, ng_ref, s0_ref, cm_ref,
                 y_ref, so_ref, s_ref, *, c):
    step = pl.program_id(1)

    @pl.when(step == 0)
    def _():
        s_ref[...] = s0_ref[0]

    rows = q_ref.shape[0]
    intra = []
    for ci in range(rows // c):
        rs = slice(ci * c, (ci + 1) * c)
        z = jnp.dot(lrdt_ref[rs], w2_ref[...], preferred_element_type=F32,
                    precision=lax.Precision.HIGHEST) + gb_ref[...]
        g = -_softplus(-z) / GLA_TAU
        sc = _dot_exact_l(cm_ref[...], g)
        intra.append(_gla_intra(q_ref[rs] * (K_C ** -0.5), k_ref[rs], v_ref[rs], sc, c, H_C, K_C, V_C))
    for ci in range(rows // c):
        rs = slice(ci * c, (ci + 1) * c)
        o = _gla_state(intra[ci], s_ref, H_C, K_C, V_C)
        rn = _cat([jnp.broadcast_to(lax.rsqrt(jnp.mean(jnp.square(o[:, h * V_C:(h + 1) * V_C]), -1,
                                                        keepdims=True) + EPS), (c, V_C))
                   for h in range(H_C)], 1)
        y_ref[rs, :] = (o * rn * ng_ref[...] * jax.nn.silu(r_ref[rs])).astype(BF16)

    @pl.when(step == pl.num_programs(1) - 1)
    def _():
        so_ref[0] = s_ref[...]


def _mixer_c(proj, gate_w2, gate_b, norm_g, s0, seq_len, rows):
    t = proj.shape[0]
    bsz = t // seq_len
    c = CHUNK if seq_len % CHUNK == 0 else seq_len
    spb = seq_len // rows
    s0t = jnp.swapaxes(s0, -1, -2)
    hk = H_C * K_C

    def col(width, off):
        return pl.BlockSpec((rows, width), lambda b, s: (b * spb + s, off // width))

    y, st = pl.pallas_call(
        functools.partial(_glac_kernel, c=c), grid=(bsz, spb),
        in_specs=[col(hk, OD_Q), col(hk, OD_K), col(HALF_W, OD_V), col(HALF_W, OD_R),
                  col(LANE, OD_LRDT),
                  pl.BlockSpec((LANE, hk), lambda b, s: (0, 0)),
                  pl.BlockSpec((1, hk), lambda b, s: (0, 0)),
                  pl.BlockSpec((1, HALF_W), lambda b, s: (0, 0)),
                  pl.BlockSpec((1, H_C, V_C, K_C), lambda b, s: (b, 0, 0, 0)),
                  pl.BlockSpec((3 * c, c), lambda b, s: (0, 0))],
        out_specs=[pl.BlockSpec((rows, HALF_W), lambda b, s: (b * spb + s, 0)),
                   pl.BlockSpec((1, H_C, V_C, K_C), lambda b, s: (b, 0, 0, 0))],
        out_shape=[jax.ShapeDtypeStruct((t, HALF_W), BF16),
                   jax.ShapeDtypeStruct((bsz, H_C, V_C, K_C), F32)],
        scratch_shapes=[pltpu.VMEM((H_C, V_C, K_C), F32)],
        compiler_params=_cparams("parallel", "arbitrary"), name="mixer_c_gla")(
            proj, proj, proj, proj, proj,
            jnp.pad(gate_w2, ((0, LANE - GLA_RANK), (0, 0))),
            gate_b.reshape(1, hk),
            norm_g.reshape(1, HALF_W), s0t, _scan_mats(c))
    return y, jnp.swapaxes(st, -1, -2)


GW = HPG_D * P_D


def _ssd_kernel(z_ref, xbc_ref, lrdt_ref, cw_ref, cb_ref, dtb_ref, alog_ref, skip_ref, ng_ref,
                ex_ref, eye_ref, cm_ref, conv0_ref, s0_ref, y_ref, so_ref, s_ref, tail_ref, *, c):
    step = pl.program_id(1)

    @pl.when(step == 0)
    def _():
        s_ref[...] = s0_ref[0]
        tail_ref[...] = conv0_ref[0]

    x = xbc_ref[...]
    rows = x.shape[0]
    tail = tail_ref[...]
    sub8 = lax.broadcasted_iota(I32, (8, CONV_DIM), 0)
    conv = x * cw_ref[D_CONV - 1:D_CONV]
    for sh in range(1, D_CONV):
        rolled = pltpu.roll(x, sh, 0)
        head = jnp.where(sub8 < sh, pltpu.roll(tail, sh, 0), rolled[0:8])
        xk = jnp.concatenate([head, rolled[8:]], 0) if rows > 8 else head
        conv = conv + xk * cw_ref[D_CONV - 1 - sh:D_CONV - sh]
    tail_ref[...] = x[rows - 8:rows]
    xc = jax.nn.silu(conv + cb_ref[...])
    a = -jnp.exp(alog_ref[...])
    ex = ex_ref[...]
    tri = lax.broadcasted_iota(I32, (c, c), 1) <= lax.broadcasted_iota(I32, (c, c), 0)
    group = lambda arr, g, w: arr[:, g * w:(g + 1) * w]
    intra = []
    for ci in range(rows // c):
        rs = slice(ci * c, (ci + 1) * c)
        xs = xc[rs, 0:HALF_W]
        bm = xc[rs, HALF_W:HALF_W + G_D * N_D]
        cmat = xc[rs, HALF_W + G_D * N_D:CONV_DIM].astype(BF16)
        dt = _softplus(lrdt_ref[rs] + dtb_ref[...])
        la = dt * a
        sc = _dot_exact_l(cm_ref[...], la)
        cum = sc[0:c]
        scx = _dot_exact_r(sc, ex)
        cumx = scx[0:c]
        lastx = scx[c:2 * c]
        xdt = xs * _dot_exact_r(dt, ex)
        cum_t = sum(_dot_nt(eye_ref[...], p) for p in _split3(cum))
        cbs = [_dot_nt(group(cmat, g, N_D), group(bm, g, N_D).astype(BF16)) for g in range(G_D)]
        xdtb = xdt.astype(BF16)
        wmats = []
        for h in range(H_D):
            hl = DT_OFF + h
            diff = cum[:, hl:hl + 1] - cum_t[hl:hl + 1, :]
            dec = jnp.where(tri, jnp.exp(jnp.minimum(diff, 0.0)), 0.0)
            wmats.append((cbs[h // HPG_D] * dec).astype(BF16))
        y_intra = _cat([jnp.dot(wmats[h], xdtb[:, h * P_D:(h + 1) * P_D], preferred_element_type=F32)
                        for h in range(H_D)], 1)
        intra.append((y_intra + xs * skip_ref[...], cmat, jnp.exp(cumx),
                      bm.T.astype(BF16), (xdt * jnp.exp(lastx - cumx)).astype(BF16), jnp.exp(lastx[0:1])))
    for ci in range(rows // c):
        rs = slice(ci * c, (ci + 1) * c)
        y_local, cmat, e_in, bm_t, xw, e_last = intra[ci]
        sgs = [s_ref[g] for g in range(G_D)]
        inter = [jnp.dot(group(cmat, g, N_D), sgs[g].astype(BF16), preferred_element_type=F32)
                 for g in range(G_D)]
        upd = [jnp.dot(bm_t[g * N_D:(g + 1) * N_D], group(xw, g, GW), preferred_element_type=F32)
               for g in range(G_D)]
        for g in range(G_D):
            s_ref[g] = sgs[g] * group(e_last, g, GW) + upd[g]
        yg = (y_local + _cat(inter, 1) * e_in) * jax.nn.silu(z_ref[rs])
        rn = _cat([jnp.broadcast_to(lax.rsqrt(jnp.mean(jnp.square(group(yg, g, GW)), -1, keepdims=True) + EPS),
                                    (c, GW)) for g in range(G_D)], 1)
        y_ref[rs, :] = (yg * rn * ng_ref[...]).astype(BF16)

    @pl.when(step == pl.num_programs(1) - 1)
    def _():
        so_ref[0] = s_ref[...]


def _mixer_d(proj, conv_w, conv_b, dt_bias, a_log, skip, norm_g, s0, conv0, seq_len, rows):
    t = proj.shape[0]
    bsz = t // seq_len
    c = CHUNK if seq_len % CHUNK == 0 else seq_len
    spb = seq_len // rows
    s0g = s0.reshape(bsz, G_D, HPG_D, N_D, P_D).transpose(0, 1, 3, 2, 4).reshape(bsz, G_D, N_D, GW)
    conv0p = jnp.pad(conv0, ((0, 0), (8 - (D_CONV - 1), 0), (0, 0)))
    expand_np = np.zeros((LANE, HALF_W), np.float32)
    expand_np[DT_OFF:DT_OFF + H_D] = np.repeat(np.eye(H_D), P_D, axis=1)
    expand = jnp.asarray(expand_np, BF16)
    eye = jnp.asarray(np.eye(LANE), BF16)
    i = np.arange(c)[:, None]
    j = np.arange(c)[None, :]
    cm = jnp.asarray(np.concatenate([j <= i, np.ones((c, c), bool)], 0), BF16)
    skipx = jnp.repeat(skip, P_D)[None, :]
    lane_pad = lambda v: jnp.pad(v.reshape(1, H_D), ((0, 0), (DT_OFF, LANE - DT_OFF - H_D)))

    def col(width, off):
        return pl.BlockSpec((rows, width), lambda b, s: (b * spb + s, off // width))

    def full(shape):
        return pl.BlockSpec(shape, lambda b, s: (0,) * len(shape))

    y, st = pl.pallas_call(
        functools.partial(_ssd_kernel, c=c), grid=(bsz, spb),
        in_specs=[col(HALF_W, OD_Z),
                  col(CONV_DIM, OD_XBC),
                  col(LANE, OD_LRDT),
                  full((D_CONV, CONV_DIM)), full((1, CONV_DIM)), full((1, LANE)), full((1, LANE)),
                  full((1, HALF_W)), full((1, HALF_W)), full((LANE, HALF_W)), full((LANE, LANE)),
                  full((2 * c, c)),
                  pl.BlockSpec((1, 8, CONV_DIM), lambda b, s: (b, 0, 0)),
                  pl.BlockSpec((1, G_D, N_D, GW), lambda b, s: (b, 0, 0, 0))],
        out_specs=[pl.BlockSpec((rows, HALF_W), lambda b, s: (b * spb + s, 0)),
                   pl.BlockSpec((1, G_D, N_D, GW), lambda b, s: (b, 0, 0, 0))],
        out_shape=[jax.ShapeDtypeStruct((t, HALF_W), BF16),
                   jax.ShapeDtypeStruct((bsz, G_D, N_D, GW), F32)],
        scratch_shapes=[pltpu.VMEM((G_D, N_D, GW), F32), pltpu.VMEM((8, CONV_DIM), F32)],
        compiler_params=_cparams("parallel", "arbitrary"), name="mixer_d_ssd")(
            proj, proj, proj, conv_w, conv_b.reshape(1, CONV_DIM), lane_pad(dt_bias),
            lane_pad(a_log), skipx, norm_g.reshape(1, HALF_W), expand, eye, cm, conv0p, s0g)
    st = st.reshape(bsz, G_D, N_D, HPG_D, P_D).transpose(0, 1, 3, 2, 4).reshape(bsz, H_D, N_D, P_D)
    return y, st


def _layer_norm(hpre, g, b):
    mu = jnp.mean(hpre, -1, keepdims=True)
    d = hpre - mu
    var = jnp.mean(d * d, -1, keepdims=True)
    return d * lax.rsqrt(var + EPS) * g + b


U32 = jnp.uint32
HI_MASK = 0xFFFF0000


def _pack_halves(x):
    half = x.shape[1] // 2
    lo = lax.bitcast_convert_type(x[:, :half].astype(BF16).astype(F32), U32) >> 16
    hi = lax.bitcast_convert_type(x[:, half:].astype(BF16).astype(F32), U32) & U32(HI_MASK)
    return lo | hi


def _unpack_halves(w):
    return (lax.bitcast_convert_type(w << 16, F32), lax.bitcast_convert_type(w & U32(HI_MASK), F32))


SUBLANES = 8
assert HALF_W == SUBLANES * LANE


def _store_row_tiles(ref, lead, w):
    m = w.shape[0]
    for s in range(SUBLANES):
        ref[lead + (pl.ds(s, m, stride=SUBLANES), slice(None))] = w[:, s * LANE:(s + 1) * LANE]


def _load_row_tiles(ref, lead, m):
    return jnp.concatenate([ref[lead + (pl.ds(s, m, stride=SUBLANES), slice(None))] for s in range(SUBLANES)], 1)


def _emit_x(xn, o_ref, ob_ref, op_ref):
    o_ref[...] = xn
    ob_ref[...] = xn.astype(BF16)
    _store_row_tiles(op_ref, (), _pack_halves(xn))


def _x_out(t, tm):
    row = lambda width: pl.BlockSpec((tm, width), lambda i, *_: (i, 0))
    return ([row(D_MODEL), row(D_MODEL), pl.BlockSpec((tm * SUBLANES, LANE), lambda i, *_: (i, 0))],
            [jax.ShapeDtypeStruct((t, D_MODEL), F32), jax.ShapeDtypeStruct((t, D_MODEL), BF16),
             jax.ShapeDtypeStruct((t * SUBLANES, LANE), U32)])


def _outproj_kernel(ya_ref, yb_ref, x_ref, w_ref, g_ref, b_ref, o_ref, ob_ref, op_ref):
    acc = jnp.dot(ya_ref[...], w_ref[0:HALF_W, :], preferred_element_type=F32)
    acc = acc + jnp.dot(yb_ref[...], w_ref[HALF_W:D_MODEL, :], preferred_element_type=F32)
    _emit_x(_layer_norm(ALPHA * x_ref[...] + acc, g_ref[...], b_ref[...]), o_ref, ob_ref, op_ref)


def _out_proj_ln(ya, yb, x, w, g, b, tm):
    t = x.shape[0]
    row = lambda width: pl.BlockSpec((tm, width), lambda i: (i, 0))
    out_specs, out_shape = _x_out(t, tm)
    return pl.pallas_call(
        _outproj_kernel, grid=(t // tm,),
        in_specs=[row(HALF_W), row(HALF_W), row(D_MODEL),
                  pl.BlockSpec((D_MODEL, D_MODEL), lambda i: (0, 0)),
                  pl.BlockSpec((1, D_MODEL), lambda i: (0, 0)),
                  pl.BlockSpec((1, D_MODEL), lambda i: (0, 0))],
        out_specs=out_specs, out_shape=out_shape,
        compiler_params=_cparams("parallel"), name="out_proj_ln")(
            ya, yb, x, w, g.reshape(1, D_MODEL), b.reshape(1, D_MODEL))


RANK_BITS = 20
RANK_MASK = (1 << RANK_BITS) - 1


def _router_kernel(x_ref, wr_ref, bias_ref, tri_ref, code_ref, gw_ref, cnt_ref, run_ref):
    step = pl.program_id(0)

    @pl.when(step == 0)
    def _():
        run_ref[...] = jnp.zeros_like(run_ref)

    tm = x_ref.shape[0]
    logits = _dot_nt(wr_ref[...], x_ref[...])
    scores = jax.nn.sigmoid(logits)
    sel = scores + bias_ref[:, 0:1]
    neg = -jnp.inf
    sub = lax.broadcasted_iota(I32, (GROUP_SIZE, tm), 0).astype(F32)
    gsc = []
    for g in range(N_GROUPS):
        blk = sel[g * GROUP_SIZE:(g + 1) * GROUP_SIZE]
        m1 = jnp.max(blk, 0, keepdims=True)
        i1 = jnp.min(jnp.where(blk == m1, sub, float(GROUP_SIZE)), 0, keepdims=True)
        m2 = jnp.max(jnp.where(sub == i1, neg, blk), 0, keepdims=True)
        gsc.append(m1 + m2)
    cur = jnp.concatenate(gsc, 0)
    gio = lax.broadcasted_iota(I32, (N_GROUPS, tm), 0).astype(F32)
    gmask = jnp.zeros((N_GROUPS, tm), F32)
    for _ in range(TOPK_GROUPS):
        m = jnp.max(cur, 0, keepdims=True)
        i = jnp.min(jnp.where(cur == m, gio, float(N_GROUPS)), 0, keepdims=True)
        pick = gio == i
        gmask = jnp.where(pick, 1.0, gmask)
        cur = jnp.where(pick, neg, cur)
    emask = jnp.concatenate(
        [jnp.broadcast_to(gmask[g:g + 1], (GROUP_SIZE, tm)) for g in range(N_GROUPS)], 0)
    cur = jnp.where(emask > 0.5, sel, neg)
    eio = lax.broadcasted_iota(I32, (N_EXPERTS, tm), 0).astype(F32)
    member = jnp.zeros((N_EXPERTS, tm), F32)
    idxs, scs = [], []
    for _ in range(TOP_K):
        m = jnp.max(cur, 0, keepdims=True)
        i = jnp.min(jnp.where(cur == m, eio, float(N_EXPERTS)), 0, keepdims=True)
        pick = eio == i
        idxs.append(i)
        scs.append(jnp.sum(jnp.where(pick, scores, 0.0), 0, keepdims=True))
        member = jnp.where(pick, 1.0, member)
        cur = jnp.where(pick, neg, cur)
    idx = jnp.concatenate(idxs, 0)
    sc = jnp.concatenate(scs, 0)
    gw_ref[...] = sc / jnp.sum(sc, 0, keepdims=True) * ROUTE_SCALE
    before = jnp.dot(member.astype(BF16), tri_ref[...], preferred_element_type=F32) + run_ref[:, 0:1]
    ranks = [jnp.sum(jnp.where(eio == idxs[k], before, 0.0), 0, keepdims=True) for k in range(TOP_K)]
    rank = jnp.concatenate(ranks, 0).astype(I32)
    code_ref[...] = idx.astype(I32) * (1 << RANK_BITS) + rank
    run_ref[...] = run_ref[...] + jnp.sum(member, 1, keepdims=True)
    cnt_ref[...] = run_ref[...]


def _router(x, w_router, bias, tm):
    t = x.shape[0]
    assert t * TOP_K < (1 << RANK_BITS)
    tri = jnp.asarray(np.arange(tm)[:, None] < np.arange(tm)[None, :], BF16)
    kt = lambda dt: jax.ShapeDtypeStruct((TOP_K, t), dt)
    return pl.pallas_call(
        _router_kernel, grid=(t // tm,),
        in_specs=[pl.BlockSpec((tm, D_MODEL), lambda i: (i, 0)),
                  pl.BlockSpec((N_EXPERTS, D_MODEL), lambda i: (0, 0)),
                  pl.BlockSpec((N_EXPERTS, LANE), lambda i: (0, 0)),
                  pl.BlockSpec((tm, tm), lambda i: (0, 0))],
        out_specs=[pl.BlockSpec((TOP_K, tm), lambda i: (0, i)),
                   pl.BlockSpec((TOP_K, tm), lambda i: (0, i)),
                   pl.BlockSpec((N_EXPERTS, LANE), lambda i: (0, 0))],
        out_shape=[kt(I32), kt(F32), jax.ShapeDtypeStruct((N_EXPERTS, LANE), F32)],
        scratch_shapes=[pltpu.VMEM((N_EXPERTS, LANE), F32)],
        compiler_params=_cparams("arbitrary"), name="moe_router")(
            x, w_router.T.astype(BF16), jnp.broadcast_to(bias[:, None], (N_EXPERTS, LANE)), tri)


def _slots_kernel(start_ref, code_ref, pos_ref):
    code = code_ref[...]
    e = lax.shift_right_logical(code, RANK_BITS)
    r = code & RANK_MASK
    pos = r
    for x in range(N_EXPERTS):
        pos = jnp.where(e == x, r + start_ref[x], pos)
    pos_ref[...] = pos * SUBLANES


def _slots(code, slot_start):
    k, t = code.shape
    tile = min(t, 4096)
    gs = pltpu.PrefetchScalarGridSpec(
        num_scalar_prefetch=1, grid=(t // tile,),
        in_specs=[pl.BlockSpec((k, tile), lambda i, *_: (0, i))],
        out_specs=pl.BlockSpec((k, tile), lambda i, *_: (0, i)))
    return pl.pallas_call(_slots_kernel, grid_spec=gs, out_shape=jax.ShapeDtypeStruct((k, t), I32),
                          compiler_params=_cparams("parallel"), name="moe_slots")(slot_start, code)


def _row_tile(ref, row0, lead=()):
    return ref.at[lead + (pl.ds(pl.multiple_of(row0, SUBLANES), SUBLANES),)]


def _dispatch_kernel(fill_ref, pos_ref, x_ref, xs_ref, zero_ref, sem, fill_sem, *, eps, fill_sizes):
    step = pl.program_id(0)
    tm = x_ref.shape[0] // SUBLANES
    zero_ref[...] = jnp.zeros_like(zero_ref)

    def copy(r, k):
        return pltpu.make_async_copy(_row_tile(x_ref, r * SUBLANES), _row_tile(xs_ref, pos_ref[0, 0, k * tm + r]), sem)

    def start(r, carry):
        for k in range(TOP_K):
            copy(r, k).start()
        return carry

    def wait(r, carry):
        for k in range(TOP_K):
            copy(r, k).wait()
        return carry

    def fill_expert(e, carry):
        base = fill_ref[0, e]
        n = fill_ref[1, e]
        for p in fill_sizes:
            first = base + (n & ~(2 * p - 1))
            cp = pltpu.make_async_copy(
                zero_ref.at[pl.ds(0, p * SUBLANES)],
                xs_ref.at[pl.ds(pl.multiple_of(first * SUBLANES, SUBLANES), p * SUBLANES)], fill_sem)

            @pl.when((n & p) != 0)
            def _():
                cp.start()
                cp.wait()
        return carry

    lax.fori_loop(0, tm, start, 0)
    e0 = jnp.minimum(step * eps, N_EXPERTS)
    e1 = jnp.minimum(e0 + eps, N_EXPERTS)
    lax.fori_loop(e0, e1, fill_expert, 0)
    lax.fori_loop(0, tm, wait, 0)


def _dispatch(xp, pos_tiles, fill, nslot, tm, bm):
    t = xp.shape[0] // SUBLANES
    steps = t // tm
    eps = -(-N_EXPERTS // steps)
    fill_sizes = tuple(1 << i for i in reversed(range((bm - 1).bit_length())))
    gs = pltpu.PrefetchScalarGridSpec(
        num_scalar_prefetch=1, grid=(steps,),
        in_specs=[pl.BlockSpec((1, 1, TOP_K * tm), lambda i, *_: (i, 0, 0), memory_space=pltpu.SMEM),
                  pl.BlockSpec((tm * SUBLANES, LANE), lambda i, *_: (i, 0))],
        out_specs=pl.BlockSpec(memory_space=pl.ANY),
        scratch_shapes=[pltpu.VMEM((fill_sizes[0] * SUBLANES, LANE), xp.dtype),
                        pltpu.SemaphoreType.DMA(()), pltpu.SemaphoreType.DMA(())])
    return pl.pallas_call(
        functools.partial(_dispatch_kernel, eps=eps, fill_sizes=fill_sizes), grid_spec=gs,
        out_shape=jax.ShapeDtypeStruct((nslot * SUBLANES, LANE), xp.dtype),
        compiler_params=_cparams("arbitrary"), name="moe_dispatch")(fill, pos_tiles, xp)


def _expert_kernel(be_ref, nu_ref, x_ref, wg_ref, wu_ref, wd_ref, o_ref, *, packed):
    del be_ref

    @pl.when(pl.program_id(0) < nu_ref[0])
    def _():
        if packed:
            lo, hi = _unpack_halves(_load_row_tiles(x_ref, (), x_ref.shape[0] // SUBLANES))
            lo = lo.astype(BF16)
            hi = hi.astype(BF16)
            hg = (jnp.dot(lo, wg_ref[0, 0:HALF_W, :], preferred_element_type=F32)
                  + jnp.dot(hi, wg_ref[0, HALF_W:D_MODEL, :], preferred_element_type=F32))
            hu = (jnp.dot(lo, wu_ref[0, 0:HALF_W, :], preferred_element_type=F32)
                  + jnp.dot(hi, wu_ref[0, HALF_W:D_MODEL, :], preferred_element_type=F32))
        else:
            x = x_ref[...]
            hg = jnp.dot(x, wg_ref[0], preferred_element_type=F32)
            hu = jnp.dot(x, wu_ref[0], preferred_element_type=F32)
        hb = (jax.nn.silu(hg) * hu).astype(BF16)
        out = jnp.dot(hb, wd_ref[0], preferred_element_type=F32)
        if packed:
            _store_row_tiles(o_ref, (), _pack_halves(out))
        else:
            o_ref[...] = out.astype(BF16)

    @pl.when(pl.program_id(0) >= nu_ref[0])
    def _():
        o_ref[...] = jnp.zeros_like(o_ref)


def _expert_ffn(xs, blk_e, nused, wg, wu, wd, layer, bm, packed):
    nrows, width = xs.shape
    brows = bm * SUBLANES if packed else bm
    nb = nrows // brows
    de = wg.shape[-1]
    gs = pltpu.PrefetchScalarGridSpec(
        num_scalar_prefetch=2, grid=(nb,),
        in_specs=[pl.BlockSpec((brows, width), lambda b, be, nu: (jnp.minimum(b, nu[0] - 1), 0)),
                  pl.BlockSpec((None, 1, D_MODEL, de), lambda b, be, nu: (layer, be[b], 0, 0)),
                  pl.BlockSpec((None, 1, D_MODEL, de), lambda b, be, nu: (layer, be[b], 0, 0)),
                  pl.BlockSpec((None, 1, de, D_MODEL), lambda b, be, nu: (layer, be[b], 0, 0))],
        out_specs=pl.BlockSpec((brows, width), lambda b, be, nu: (b, 0)))
    return pl.pallas_call(
        functools.partial(_expert_kernel, packed=packed), grid_spec=gs,
        out_shape=jax.ShapeDtypeStruct((nrows, width), xs.dtype),
        compiler_params=_cparams("arbitrary"), name="moe_expert_ffn")(blk_e, nused, xs, wg, wu, wd)


def _combine_kernel(pos_ref, nxt_ref, gw_ref, x_ref, sh_ref, eo_ref, g_ref, b_ref, o_ref, ob_ref, op_ref,
                    buf, sems):
    step = pl.program_id(0)
    nsteps = pl.num_programs(0)
    tm = x_ref.shape[0]
    cur = lax.rem(step, 2)

    def copy(idx_ref, slot, r, k):
        return pltpu.make_async_copy(_row_tile(eo_ref, idx_ref[0, 0, k * tm + r]),
                                     _row_tile(buf, r * SUBLANES, (slot, k)), sems.at[slot])

    def gather(idx_ref, slot):
        def start(r, carry):
            for k in range(TOP_K):
                copy(idx_ref, slot, r, k).start()
            return carry
        lax.fori_loop(0, tm, start, 0)

    @pl.when(step == 0)
    def _():
        gather(pos_ref, 0)

    @pl.when(step + 1 < nsteps)
    def _():
        gather(nxt_ref, 1 - cur)

    def wait(r, carry):
        for k in range(TOP_K):
            copy(pos_ref, cur, r, k).wait()
        return carry

    lax.fori_loop(0, tm, wait, 0)
    sh = sh_ref[...].astype(F32)
    lo_acc = sh[:, :HALF_W]
    hi_acc = sh[:, HALF_W:]
    for k in range(TOP_K):
        lo, hi = _unpack_halves(_load_row_tiles(buf, (cur, k), tm))
        wk = gw_ref[:, k:k + 1]
        lo_acc = lo_acc + lo * wk
        hi_acc = hi_acc + hi * wk
    ffn = jnp.concatenate([lo_acc, hi_acc], 1)
    _emit_x(_layer_norm(ALPHA * x_ref[...] + ffn, g_ref[...], b_ref[...]), o_ref, ob_ref, op_ref)


def _combine_ln(pos_tiles, gw_t, x, sh, eo, g, b, tm):
    t = x.shape[0]
    steps = t // tm
    row = lambda width: pl.BlockSpec((tm, width), lambda i: (i, 0))
    const = pl.BlockSpec((1, D_MODEL), lambda i: (0, 0))
    idx = lambda f: pl.BlockSpec((1, 1, TOP_K * tm), f, memory_space=pltpu.SMEM)
    out_specs, out_shape = _x_out(t, tm)
    return pl.pallas_call(
        _combine_kernel, grid=(steps,),
        in_specs=[idx(lambda i: (i, 0, 0)), idx(lambda i: (jnp.minimum(i + 1, steps - 1), 0, 0)),
                  row(TOP_K), row(D_MODEL), row(D_MODEL), pl.BlockSpec(memory_space=pl.ANY), const, const],
        out_specs=out_specs, out_shape=out_shape,
        scratch_shapes=[pltpu.VMEM((2, TOP_K, tm * SUBLANES, LANE), U32), pltpu.SemaphoreType.DMA((2,))],
        compiler_params=_cparams("arbitrary"), name="moe_combine_ln")(
            pos_tiles, pos_tiles, gw_t, x, sh, eo, g.reshape(1, D_MODEL), b.reshape(1, D_MODEL))


def _moe_ln(x, xb, xp, w_router, bias, wg, wu, wd, wsg, wsu, wsd, layer, g, b, tm, bm):
    t = x.shape[0]
    code, gw, cnt = _router(xb, w_router, bias, tm)
    counts = cnt[:, 0].astype(I32)
    padded = (counts + bm - 1) // bm * bm
    pad_end = jnp.cumsum(padded)
    slot_start = pad_end - padded
    fill = jnp.stack([slot_start + counts, padded - counts])
    nb = -(-(t * TOP_K + N_EXPERTS * (bm - 1)) // bm)
    blk_first = jnp.arange(nb, dtype=I32) * bm
    blk_e = jnp.minimum(jnp.sum((pad_end[None, :] <= blk_first[:, None]).astype(I32), 1), N_EXPERTS - 1)
    nused = (pad_end[-1] // bm).astype(I32).reshape(1)
    pos = _slots(code, slot_start)
    pos_tiles = pos.reshape(TOP_K, t // tm, tm).transpose(1, 0, 2).reshape(t // tm, 1, TOP_K * tm)
    xs = _dispatch(xp, pos_tiles, fill, nb * bm, tm, bm)
    eo = _expert_ffn(xs, blk_e, nused, wg, wu, wd, layer, bm, True)
    shared = _expert_ffn(xb, jnp.zeros((t // tm,), I32), jnp.full((1,), t // tm, I32),
                         wsg[:, None], wsu[:, None], wsd[:, None], layer, tm, False)
    return _combine_ln(pos_tiles, gw.T, x, shared, eo, g, b, tm)


def _odd_w_in(w):
    sizes = (H_C * K_C, H_C * K_C, H_C * V_C, GLA_RANK, H_C * V_C, HALF_W, CONV_DIM, H_D)
    q, k, v, lr, r, z, xbc, dt = jnp.split(w, [int(s) for s in np.cumsum(sizes)[:-1]], axis=-1)
    pad = jnp.zeros((w.shape[0], OD_XBC - (OD_LRDT + GLA_RANK + H_D)), w.dtype)
    return jnp.concatenate([q, k, v, r, z, lr, dt, pad, xbc], -1)


def _trunk(x3, hgrn0, gla0, ssm0, conv0, p, cfg):
    bsz, seq_len, _ = x3.shape
    t = bsz * seq_len
    x = x3.reshape(t, D_MODEL)
    xb = None
    tm, rows, bm = cfg["tm"], cfg["rows"], cfg["bm"]
    outs = {}
    for l in range(DEPTH):
        j = l // 2
        if l % 2 == 0:
            proj = _proj(x if xb is None else xb, p["ev_w_in"][j], cfg["ptm"], cfg["ptn"])
            ya, v_rows = _mixer_a(proj, p["ev_a_ln_g"][j], p["ev_a_ln_b"][j], p["ev_a_ws"][j],
                                  p["ev_a_bs"][j], seq_len, cfg["arows"], cfg["emit_v"])
            yb, s_b = _mixer_b(proj, p["hgrn_lb_logits"], p["ev_b_norm_g"][j], hgrn0[j], seq_len, rows, l)
            outs["v_rows"] = v_rows
            outs["hgrn"] = s_b
            w_out = p["ev_w_out"][j]
        else:
            proj = _proj(x if xb is None else xb, p["od_w_in"][j], cfg["ptm"], cfg["ptn"])
            ya, s_c = _mixer_c(proj, p["od_c_gate_w2"][j], p["od_c_gate_b"][j], p["od_c_norm_g"][j],
                               gla0[j], seq_len, rows)
            yb, s_d = _mixer_d(proj, p["od_d_conv_w"][j], p["od_d_conv_b"][j], p["od_d_dt_bias"][j],
                               p["od_d_a_log"][j], p["od_d_skip"][j], p["od_d_norm_g"][j],
                               ssm0[j], conv0[j], seq_len, rows)
            keep = min(D_CONV - 1, seq_len)
            xbc = proj.reshape(bsz, seq_len, OD_PAD)[:, seq_len - keep:, OD_XBC:OD_XBC + CONV_DIM]
            outs["conv"] = jnp.concatenate([conv0[j], xbc], 1)[:, -(D_CONV - 1):]
            outs["gla"] = s_c
            outs["ssm"] = s_d
            w_out = p["od_w_out"][j]
        x, xb, xp = _out_proj_ln(ya, yb, x, w_out, p["ln1_g"][l], p["ln1_b"][l], tm)
        x, xb, _ = _moe_ln(x, xb, xp, p["moe_w_router"][l], p["moe_router_bias"][l], p["moe_w_gate"],
                           p["moe_w_up"], p["moe_w_down"], p["moe_ws_gate"], p["moe_ws_up"],
                           p["moe_ws_down"], l, p["ln2_g"][l], p["ln2_b"][l], tm, bm)
    return x.reshape(bsz, seq_len, D_MODEL), outs


def _config(bsz, seq_len):
    t = bsz * seq_len
    if seq_len % CHUNK == 0:
        return dict(tm=256, rows=min(256, seq_len), arows=min(256, seq_len), bm=512,
                    ptm=min(1024, t), ptn=512, emit_v=False)
    return dict(tm=t, rows=seq_len, arows=t, bm=128, ptm=t, ptn=512, emit_v=True)


def kernel(x_prompt, x_sample, state_b_hgrn, state_c_gla, state_d_ssm, state_d_conv, ev_w_in, ev_a_ln_g, ev_a_ln_b, ev_a_ws, ev_a_bs, ev_b_norm_g, ev_w_out, hgrn_lb_logits, od_w_in, od_c_gate_w2, od_c_gate_b, od_c_norm_g, od_d_conv_w, od_d_conv_b, od_d_dt_bias, od_d_a_log, od_d_skip, od_d_norm_g, od_w_out, ln1_g, ln1_b, ln2_g, ln2_b, moe_w_router, moe_router_bias, moe_w_gate, moe_w_up, moe_w_down, moe_ws_gate, moe_ws_up, moe_ws_down):
    p = dict(
        ev_w_in=ev_w_in.astype(BF16), ev_a_ln_g=ev_a_ln_g, ev_a_ln_b=ev_a_ln_b, ev_a_ws=ev_a_ws,
        ev_a_bs=ev_a_bs, ev_b_norm_g=ev_b_norm_g, ev_w_out=ev_w_out.astype(BF16),
        hgrn_lb_logits=hgrn_lb_logits,
        od_w_in=jnp.stack([_odd_w_in(od_w_in[j]) for j in range(od_w_in.shape[0])]).astype(BF16),
        od_c_gate_w2=od_c_gate_w2, od_c_gate_b=od_c_gate_b, od_c_norm_g=od_c_norm_g,
        od_d_conv_w=od_d_conv_w, od_d_conv_b=od_d_conv_b, od_d_dt_bias=od_d_dt_bias,
        od_d_a_log=od_d_a_log, od_d_skip=od_d_skip, od_d_norm_g=od_d_norm_g,
        od_w_out=od_w_out.astype(BF16), ln1_g=ln1_g, ln1_b=ln1_b, ln2_g=ln2_g, ln2_b=ln2_b,
        moe_w_router=moe_w_router, moe_router_bias=moe_router_bias,
        moe_w_gate=moe_w_gate.astype(BF16), moe_w_up=moe_w_up.astype(BF16),
        moe_w_down=moe_w_down.astype(BF16), moe_ws_gate=moe_ws_gate.astype(BF16),
        moe_ws_up=moe_ws_up.astype(BF16), moe_ws_down=moe_ws_down.astype(BF16))
    bp, lp, _ = x_prompt.shape
    n_even = state_b_hgrn.shape[0]
    n_odd = state_c_gla.shape[0]
    zeros = lambda n, *s: jnp.zeros((n, bp) + s, F32)
    y_p, o_p = _trunk(x_prompt, zeros(n_even, H_B, K_B, V_B), zeros(n_odd, H_C, K_C, V_C),
                      zeros(n_odd, H_D, N_D, P_D), zeros(n_odd, D_CONV - 1, CONV_DIM), p,
                      _config(bp, lp))
    bs, ls, _ = x_sample.shape
    y_s, o_s = _trunk(x_sample, state_b_hgrn, state_c_gla, state_d_ssm, state_d_conv, p,
                      _config(bs, ls))
    a_v = o_s["v_rows"].reshape(1, bs, ls, HALF_W)
    return (y_p, y_s, a_v, o_p["hgrn"][None], o_s["hgrn"][None], o_p["gla"][None], o_s["gla"][None],
            o_p["ssm"][None], o_s["ssm"][None], o_p["conv"][None], o_s["conv"][None])
```

```python
import functools
import math

import jax
import jax.numpy as jnp
import numpy as np
from jax import lax
from jax.experimental import pallas as pl
from jax.experimental.pallas import tpu as pltpu

F32 = jnp.float32
BF16 = jnp.bfloat16
I32 = jnp.int32

D_MODEL = 2048
DEPTH = 2
CHUNK = 64
SUB = 16
HALF_W = D_MODEL // 2
A_CHUNK = 128
H_A = 4
DA = HALF_W // H_A
H_B = 8
K_B = 128
V_B = HALF_W // H_B
H_C = 4
V_C = HALF_W // H_C
K_C = V_C // 2
GLA_RANK = 16
GLA_TAU = 16.0
P_D = 64
H_D = HALF_W // P_D
G_D = 2
HPG_D = H_D // G_D
N_D = 128
D_CONV = 4
CONV_DIM = HALF_W + 2 * G_D * N_D
N_EXPERTS = 64
N_GROUPS = 8
GROUP_SIZE = N_EXPERTS // N_GROUPS
TOPK_GROUPS = 4
TOP_K = 8
D_EXPERT = 512
ROUTE_SCALE = 2.5
ALPHA = (2 * DEPTH) ** 0.25
EPS = 1e-5
LANE = 128
VMEM_LIMIT = 56 * 1024 * 1024

OD_Q, OD_K, OD_V, OD_R, OD_Z, OD_LRDT, OD_XBC = 0, 512, 1024, 2048, 3072, 4096, 4608
OD_PAD = 6144
DT_OFF = GLA_RANK


def _cparams(*sem):
    return pltpu.CompilerParams(dimension_semantics=sem, vmem_limit_bytes=VMEM_LIMIT)


def _split3(x):
    hi = x.astype(BF16)
    r = x - hi.astype(F32)
    mid = r.astype(BF16)
    lo = (r - mid.astype(F32)).astype(BF16)
    return hi, mid, lo


def _split2(x):
    hi = x.astype(BF16)
    return hi, (x - hi.astype(F32)).astype(BF16)


def _dot_exact_l(a_bf16, x):
    return sum(jnp.dot(a_bf16, p, preferred_element_type=F32) for p in _split3(x))


def _dot_exact_r(x, b_bf16):
    return sum(jnp.dot(p, b_bf16, preferred_element_type=F32) for p in _split3(x))


def _dot_nt(a, b):
    return lax.dot_general(a, b, (((1,), (1,)), ((), ())), preferred_element_type=F32)


def _softplus(x):
    return jnp.maximum(x, 0.0) + jnp.log1p(jnp.exp(-jnp.abs(x)))


def _proj_kernel(x_ref, w_ref, o_ref, *scratch):
    if scratch:
        xb_ref, = scratch

        @pl.when(pl.program_id(1) == 0)
        def _():
            xb_ref[...] = x_ref[...].astype(BF16)
        x = xb_ref[...]
    else:
        x = x_ref[...]
    o_ref[...] = jnp.dot(x, w_ref[...], preferred_element_type=F32)


def _proj(x, w, tm, tn):
    t, k = x.shape
    n = w.shape[1]
    scratch = [] if x.dtype == BF16 else [pltpu.VMEM((tm, k), BF16)]
    return pl.pallas_call(
        _proj_kernel, grid=(t // tm, n // tn),
        in_specs=[pl.BlockSpec((tm, k), lambda i, j: (i, 0)),
                  pl.BlockSpec((k, tn), lambda i, j: (0, j))],
        out_specs=pl.BlockSpec((tm, tn), lambda i, j: (i, j)),
        out_shape=jax.ShapeDtypeStruct((t, n), F32),
        scratch_shapes=scratch, compiler_params=_cparams("parallel", "arbitrary"),
        name="in_proj")(x, w)


def _mixa_kernel(u_ref, v_ref, lng_ref, lnb_ref, w_ref, bs_ref, ya_ref, *vrows, ca):
    gu = jax.nn.gelu(u_ref[...])
    gv = jax.nn.gelu(v_ref[...])
    rows = gu.shape[0]
    for h in range(H_A):
        sl = slice(h * DA, (h + 1) * DA)
        vh = gv[:, sl]
        mu = jnp.mean(vh, -1, keepdims=True)
        d = vh - mu
        var = jnp.mean(d * d, -1, keepdims=True)
        vn = d * lax.rsqrt(var + EPS) * lng_ref[h] + lnb_ref[h]
        if vrows:
            vrows[0][:, sl] = vn
        vnb = vn.astype(BF16)
        for c in range(rows // ca):
            rs = slice(c * ca, (c + 1) * ca)
            s = jnp.dot(w_ref[h], vnb[rs], preferred_element_type=F32) + bs_ref[h]
            ya_ref[rs, sl] = (gu[rs, sl] * s).astype(BF16)


def _mixer_a(proj, lng, lnb, ws, bs, seq_len, rows, emit_v):
    t = proj.shape[0]
    ca = min(A_CHUNK, seq_len)
    pos = np.arange(A_CHUNK)
    mask = (pos[None, :] // CHUNK) <= (pos[:, None] // CHUNK)
    w = (ws * mask)[:, :ca, :ca].astype(BF16)
    bsb = jnp.broadcast_to(bs[:, :ca, None], (H_A, ca, DA)).astype(F32)
    nb = HALF_W // HALF_W
    out_shape = [jax.ShapeDtypeStruct((t, HALF_W), BF16)]
    out_specs = [pl.BlockSpec((rows, HALF_W), lambda i: (i, 0))]
    if emit_v:
        out_shape.append(jax.ShapeDtypeStruct((t, HALF_W), F32))
        out_specs.append(pl.BlockSpec((rows, HALF_W), lambda i: (i, 0)))
    del nb
    res = pl.pallas_call(
        functools.partial(_mixa_kernel, ca=ca), grid=(t // rows,),
        in_specs=[pl.BlockSpec((rows, HALF_W), lambda i: (i, 0)),
                  pl.BlockSpec((rows, HALF_W), lambda i: (i, 1)),
                  pl.BlockSpec((H_A, 1, DA), lambda i: (0, 0, 0)),
                  pl.BlockSpec((H_A, 1, DA), lambda i: (0, 0, 0)),
                  pl.BlockSpec((H_A, ca, ca), lambda i: (0, 0, 0)),
                  pl.BlockSpec((H_A, ca, DA), lambda i: (0, 0, 0))],
        out_specs=out_specs, out_shape=out_shape,
        compiler_params=_cparams("parallel"), name="mixer_a")(
            proj, proj, lng[:, None, :], lnb[:, None, :], w, bsb)
    return res if emit_v else (res[0], None)


def _scan_mats(c):
    i = np.arange(c)[:, None]
    j = np.arange(c)[None, :]
    tril = (j <= i)
    local = tril & ((i // SUB) == (j // SUB))
    ones = np.ones((c, c), bool)
    return jnp.asarray(np.concatenate([local, tril, ones], 0), BF16)


def _cat(parts, axis):
    return parts[0] if len(parts) == 1 else jnp.concatenate(parts, axis)


def _gla_intra(q, k, v, sc, c, nh, kd, vd):
    lc, cum, last = sc[0:c], sc[c:2 * c], sc[2 * c:3 * c]
    ns = c // SUB
    q_hi, q_lo = _split2(q * jnp.exp(lc))
    pre = cum - lc
    vb = v.astype(BF16)
    att = []
    for blk in range(ns):
        n = SUB * (blk + 1)
        r0 = SUB * blk
        k_hi, k_lo = _split2(k[:n] * jnp.exp(pre[r0:r0 + 1] - cum[:n]))
        rq = slice(r0, r0 + SUB)
        for h in range(nh):
            ks = slice(h * kd, (h + 1) * kd)
            att.append(_dot_nt(q_hi[rq, ks], k_hi[:, ks])
                       + (_dot_nt(q_hi[rq, ks], k_lo[:, ks]) + _dot_nt(q_lo[rq, ks], k_hi[:, ks])))
    attb = []
    for blk in range(ns):
        n = SUB * (blk + 1)
        keep = (lax.broadcasted_iota(I32, (SUB, n), 1) - SUB * blk) <= lax.broadcasted_iota(I32, (SUB, n), 0)
        attb += [jnp.where(keep, att[blk * nh + h], 0.0).astype(BF16) for h in range(nh)]
    rows_out = []
    for blk in range(ns):
        n = SUB * (blk + 1)
        rows_out.append(_cat([jnp.dot(attb[blk * nh + h], vb[:n, h * vd:(h + 1) * vd],
                                      preferred_element_type=F32) for h in range(nh)], 1))
    qc = (q * jnp.exp(cum)).astype(BF16)
    kc = (k * jnp.exp(last - cum)).astype(BF16)
    return _cat(rows_out, 0), qc, kc, v.T.astype(BF16), jnp.exp(last[0:1])


def _gla_state(intra, s_ref, nh, kd, vd):
    o_intra, qc, kc, vt, e_last = intra
    sts = [s_ref[h] for h in range(nh)]
    inter = [_dot_nt(qc[:, h * kd:(h + 1) * kd], sts[h].astype(BF16)) for h in range(nh)]
    upd = [jnp.dot(vt[h * vd:(h + 1) * vd], kc[:, h * kd:(h + 1) * kd], preferred_element_type=F32)
           for h in range(nh)]
    for h in range(nh):
        s_ref[h] = sts[h] * e_last[:, h * kd:(h + 1) * kd] + upd[h]
    return o_intra + _cat(inter, 1)


def _hgrn_kernel(q_ref, f_ref, i_ref, g_ref, lbl_ref, ng_ref, s0_ref, cm_ref, y_ref, so_ref, s_ref,
                 *, c, layer):
    step = pl.program_id(1)

    @pl.when(step == 0)
    def _():
        s_ref[...] = s0_ref[0]

    lg = lbl_ref[...]
    ex = jnp.exp(lg - jnp.max(lg, 0, keepdims=True))
    sm = ex / jnp.sum(ex, 0, keepdims=True)
    lb = jnp.sum(sm[:layer + 1], 0, keepdims=True)
    rows = q_ref.shape[0]
    intra = []
    for ci in range(rows // c):
        rs = slice(ci * c, (ci + 1) * c)
        f = lb + (1.0 - lb) * jax.nn.sigmoid(f_ref[rs])
        g = jnp.log(f)
        sc = _dot_exact_l(cm_ref[...], g)
        intra.append(_gla_intra(jax.nn.silu(q_ref[rs]), 1.0 - f, i_ref[rs], sc, c, H_B, K_B, V_B))
    for ci in range(rows // c):
        rs = slice(ci * c, (ci + 1) * c)
        o = _gla_state(intra[ci], s_ref, H_B, K_B, V_B)
        rn = _cat([jnp.broadcast_to(lax.rsqrt(jnp.mean(jnp.square(o[:, h * V_B:(h + 1) * V_B]), -1,
                                                        keepdims=True) + EPS), (c, V_B))
                   for h in range(H_B)], 1)
        y_ref[rs, :] = (o * rn * ng_ref[...] * jax.nn.sigmoid(g_ref[rs])).astype(BF16)

    @pl.when(step == pl.num_programs(1) - 1)
    def _():
        so_ref[0] = s_ref[...]


def _mixer_b(proj, lb_logits, norm_g, s0, seq_len, rows, layer):
    t = proj.shape[0]
    bsz = t // seq_len
    c = CHUNK if seq_len % CHUNK == 0 else seq_len
    spb = seq_len // rows
    s0t = jnp.swapaxes(s0, -1, -2)
    col = lambda j: pl.BlockSpec((rows, HALF_W), lambda b, s, j=j: (b * spb + s, j))
    nl = lb_logits.shape[0]
    y, st = pl.pallas_call(
        functools.partial(_hgrn_kernel, c=c, layer=layer), grid=(bsz, spb),
        in_specs=[col(2), col(3), col(4), col(5),
                  pl.BlockSpec((nl, HALF_W), lambda b, s: (0, 0)),
                  pl.BlockSpec((1, HALF_W), lambda b, s: (0, 0)),
                  pl.BlockSpec((1, H_B, V_B, K_B), lambda b, s: (b, 0, 0, 0)),
                  pl.BlockSpec((3 * c, c), lambda b, s: (0, 0))],
        out_specs=[pl.BlockSpec((rows, HALF_W), lambda b, s: (b * spb + s, 0)),
                   pl.BlockSpec((1, H_B, V_B, K_B), lambda b, s: (b, 0, 0, 0))],
        out_shape=[jax.ShapeDtypeStruct((t, HALF_W), BF16),
                   jax.ShapeDtypeStruct((bsz, H_B, V_B, K_B), F32)],
        scratch_shapes=[pltpu.VMEM((H_B, V_B, K_B), F32)],
        compiler_params=_cparams("parallel", "arbitrary"), name="mixer_b_hgrn")(
            proj, proj, proj, proj, lb_logits, norm_g.reshape(1, HALF_W), s0t, _scan_mats(c))
    return y, jnp.swapaxes(st, -1, -2)


def _glac_kernel(q_ref, k_ref, v_ref, r_ref, lrdt_ref, w2_ref, gb_ref, ng_ref, s0_ref, cm_ref,
                 y_ref, so_ref, s_ref, *, c):
    step = pl.program_id(1)

    @pl.when(step == 0)
    def _():
        s_ref[...] = s0_ref[0]

    rows = q_ref.shape[0]
    intra = []
    for ci in range(rows // c):
        rs = slice(ci * c, (ci + 1) * c)
        z = jnp.dot(lrdt_ref[rs], w2_ref[...], preferred_element_type=F32,
                    precision=lax.Precision.HIGHEST) + gb_ref[...]
        g = -_softplus(-z) / GLA_TAU
        sc = _dot_exact_l(cm_ref[...], g)
        intra.append(_gla_intra(q_ref[rs] * (K_C ** -0.5), k_ref[rs], v_ref[rs], sc, c, H_C, K_C, V_C))
    for ci in range(rows // c):
        rs = slice(ci * c, (ci + 1) * c)
        o = _gla_state(intra[ci], s_ref, H_C, K_C, V_C)
        rn = _cat([jnp.broadcast_to(lax.rsqrt(jnp.mean(jnp.square(o[:, h * V_C:(h + 1) * V_C]), -1,
                                                        keepdims=True) + EPS), (c, V_C))
                   for h in range(H_C)], 1)
        y_ref[rs, :] = (o * rn * ng_ref[...] * jax.nn.silu(r_ref[rs])).astype(BF16)

    @pl.when(step == pl.num_programs(1) - 1)
    def _():
        so_ref[0] = s_ref[...]


def _mixer_c(proj, gate_w2, gate_b, norm_g, s0, seq_len, rows):
    t = proj.shape[0]
    bsz = t // seq_len
    c = CHUNK if seq_len % CHUNK == 0 else seq_len
    spb = seq_len // rows
    s0t = jnp.swapaxes(s0, -1, -2)
    hk = H_C * K_C

    def col(width, off):
        return pl.BlockSpec((rows, width), lambda b, s: (b * spb + s, off // width))

    y, st = pl.pallas_call(
        functools.partial(_glac_kernel, c=c), grid=(bsz, spb),
        in_specs=[col(hk, OD_Q), col(hk, OD_K), col(HALF_W, OD_V), col(HALF_W, OD_R),
                  col(LANE, OD_LRDT),
                  pl.BlockSpec((LANE, hk), lambda b, s: (0, 0)),
                  pl.BlockSpec((1, hk), lambda b, s: (0, 0)),
                  pl.BlockSpec((1, HALF_W), lambda b, s: (0, 0)),
                  pl.BlockSpec((1, H_C, V_C, K_C), lambda b, s: (b, 0, 0, 0)),
                  pl.BlockSpec((3 * c, c), lambda b, s: (0, 0))],
        out_specs=[pl.BlockSpec((rows, HALF_W), lambda b, s: (b * spb + s, 0)),
                   pl.BlockSpec((1, H_C, V_C, K_C), lambda b, s: (b, 0, 0, 0))],
        out_shape=[jax.ShapeDtypeStruct((t, HALF_W), BF16),
                   jax.ShapeDtypeStruct((bsz, H_C, V_C, K_C), F32)],
        scratch_shapes=[pltpu.VMEM((H_C, V_C, K_C), F32)],
        compiler_params=_cparams("parallel", "arbitrary"), name="mixer_c_gla")(
            proj, proj, proj, proj, proj,
            jnp.pad(gate_w2, ((0, LANE - GLA_RANK), (0, 0))),
            gate_b.reshape(1, hk),
            norm_g.reshape(1, HALF_W), s0t, _scan_mats(c))
    return y, jnp.swapaxes(st, -1, -2)


GW = HPG_D * P_D


def _ssd_kernel(z_ref, xbc_ref, lrdt_ref, cw_ref, cb_ref, dtb_ref, alog_ref, skip_ref, ng_ref,
                ex_ref, eye_ref, cm_ref, conv0_ref, s0_ref, y_ref, so_ref, s_ref, tail_ref, *, c):
    step = pl.program_id(1)

    @pl.when(step == 0)
    def _():
        s_ref[...] = s0_ref[0]
        tail_ref[...] = conv0_ref[0]

    x = xbc_ref[...]
    rows = x.shape[0]
    tail = tail_ref[...]
    sub8 = lax.broadcasted_iota(I32, (8, CONV_DIM), 0)
    conv = x * cw_ref[D_CONV - 1:D_CONV]
    for sh in range(1, D_CONV):
        rolled = pltpu.roll(x, sh, 0)
        head = jnp.where(sub8 < sh, pltpu.roll(tail, sh, 0), rolled[0:8])
        xk = jnp.concatenate([head, rolled[8:]], 0) if rows > 8 else head
        conv = conv + xk * cw_ref[D_CONV - 1 - sh:D_CONV - sh]
    tail_ref[...] = x[rows - 8:rows]
    xc = jax.nn.silu(conv + cb_ref[...])
    a = -jnp.exp(alog_ref[...])
    ex = ex_ref[...]
    tri = lax.broadcasted_iota(I32, (c, c), 1) <= lax.broadcasted_iota(I32, (c, c), 0)
    group = lambda arr, g, w: arr[:, g * w:(g + 1) * w]
    intra = []
    for ci in range(rows // c):
        rs = slice(ci * c, (ci + 1) * c)
        xs = xc[rs, 0:HALF_W]
        bm = xc[rs, HALF_W:HALF_W + G_D * N_D]
        cmat = xc[rs, HALF_W + G_D * N_D:CONV_DIM].astype(BF16)
        dt = _softplus(lrdt_ref[rs] + dtb_ref[...])
        la = dt * a
        sc = _dot_exact_l(cm_ref[...], la)
        cum = sc[0:c]
        scx = _dot_exact_r(sc, ex)
        cumx = scx[0:c]
        lastx = scx[c:2 * c]
        xdt = xs * _dot_exact_r(dt, ex)
        cum_t = sum(_dot_nt(eye_ref[...], p) for p in _split3(cum))
        cbs = [_dot_nt(group(cmat, g, N_D), group(bm, g, N_D).astype(BF16)) for g in range(G_D)]
        xdtb = xdt.astype(BF16)
        wmats = []
        for h in range(H_D):
            hl = DT_OFF + h
            diff = cum[:, hl:hl + 1] - cum_t[hl:hl + 1, :]
            dec = jnp.where(tri, jnp.exp(jnp.minimum(diff, 0.0)), 0.0)
            wmats.append((cbs[h // HPG_D] * dec).astype(BF16))
        y_intra = _cat([jnp.dot(wmats[h], xdtb[:, h * P_D:(h + 1) * P_D], preferred_element_type=F32)
                        for h in range(H_D)], 1)
        intra.append((y_intra + xs * skip_ref[...], cmat, jnp.exp(cumx),
                      bm.T.astype(BF16), (xdt * jnp.exp(lastx - cumx)).astype(BF16), jnp.exp(lastx[0:1])))
    for ci in range(rows // c):
        rs = slice(ci * c, (ci + 1) * c)
        y_local, cmat, e_in, bm_t, xw, e_last = intra[ci]
        sgs = [s_ref[g] for g in range(G_D)]
        inter = [jnp.dot(group(cmat, g, N_D), sgs[g].astype(BF16), preferred_element_type=F32)
                 for g in range(G_D)]
        upd = [jnp.dot(bm_t[g * N_D:(g + 1) * N_D], group(xw, g, GW), preferred_element_type=F32)
               for g in range(G_D)]
        for g in range(G_D):
            s_ref[g] = sgs[g] * group(e_last, g, GW) + upd[g]
        yg = (y_local + _cat(inter, 1) * e_in) * jax.nn.silu(z_ref[rs])
        rn = _cat([jnp.broadcast_to(lax.rsqrt(jnp.mean(jnp.square(group(yg, g, GW)), -1, keepdims=True) + EPS),
                                    (c, GW)) for g in range(G_D)], 1)
        y_ref[rs, :] = (yg * rn * ng_ref[...]).astype(BF16)

    @pl.when(step == pl.num_programs(1) - 1)
    def _():
        so_ref[0] = s_ref[...]


def _mixer_d(proj, conv_w, conv_b, dt_bias, a_log, skip, norm_g, s0, conv0, seq_len, rows):
    t = proj.shape[0]
    bsz = t // seq_len
    c = CHUNK if seq_len % CHUNK == 0 else seq_len
    spb = seq_len // rows
    s0g = s0.reshape(bsz, G_D, HPG_D, N_D, P_D).transpose(0, 1, 3, 2, 4).reshape(bsz, G_D, N_D, GW)
    conv0p = jnp.pad(conv0, ((0, 0), (8 - (D_CONV - 1), 0), (0, 0)))
    expand_np = np.zeros((LANE, HALF_W), np.float32)
    expand_np[DT_OFF:DT_OFF + H_D] = np.repeat(np.eye(H_D), P_D, axis=1)
    expand = jnp.asarray(expand_np, BF16)
    eye = jnp.asarray(np.eye(LANE), BF16)
    i = np.arange(c)[:, None]
    j = np.arange(c)[None, :]
    cm = jnp.asarray(np.concatenate([j <= i, np.ones((c, c), bool)], 0), BF16)
    skipx = jnp.repeat(skip, P_D)[None, :]
    lane_pad = lambda v: jnp.pad(v.reshape(1, H_D), ((0, 0), (DT_OFF, LANE - DT_OFF - H_D)))

    def col(width, off):
        return pl.BlockSpec((rows, width), lambda b, s: (b * spb + s, off // width))

    def full(shape):
        return pl.BlockSpec(shape, lambda b, s: (0,) * len(shape))

    y, st = pl.pallas_call(
        functools.partial(_ssd_kernel, c=c), grid=(bsz, spb),
        in_specs=[col(HALF_W, OD_Z),
                  col(CONV_DIM, OD_XBC),
                  col(LANE, OD_LRDT),
                  full((D_CONV, CONV_DIM)), full((1, CONV_DIM)), full((1, LANE)), full((1, LANE)),
                  full((1, HALF_W)), full((1, HALF_W)), full((LANE, HALF_W)), full((LANE, LANE)),
                  full((2 * c, c)),
                  pl.BlockSpec((1, 8, CONV_DIM), lambda b, s: (b, 0, 0)),
                  pl.BlockSpec((1, G_D, N_D, GW), lambda b, s: (b, 0, 0, 0))],
        out_specs=[pl.BlockSpec((rows, HALF_W), lambda b, s: (b * spb + s, 0)),
                   pl.BlockSpec((1, G_D, N_D, GW), lambda b, s: (b, 0, 0, 0))],
        out_shape=[jax.ShapeDtypeStruct((t, HALF_W), BF16),
                   jax.ShapeDtypeStruct((bsz, G_D, N_D, GW), F32)],
        scratch_shapes=[pltpu.VMEM((G_D, N_D, GW), F32), pltpu.VMEM((8, CONV_DIM), F32)],
        compiler_params=_cparams("parallel", "arbitrary"), name="mixer_d_ssd")(
            proj, proj, proj, conv_w, conv_b.reshape(1, CONV_DIM), lane_pad(dt_bias),
            lane_pad(a_log), skipx, norm_g.reshape(1, HALF_W), expand, eye, cm, conv0p, s0g)
    st = st.reshape(bsz, G_D, N_D, HPG_D, P_D).transpose(0, 1, 3, 2, 4).reshape(bsz, H_D, N_D, P_D)
    return y, st


def _layer_norm(hpre, g, b):
    mu = jnp.mean(hpre, -1, keepdims=True)
    d = hpre - mu
    var = jnp.mean(d * d, -1, keepdims=True)
    return d * lax.rsqrt(var + EPS) * g + b


U32 = jnp.uint32
HI_MASK = 0xFFFF0000


def _pack_halves(x):
    half = x.shape[1] // 2
    lo = lax.bitcast_convert_type(x[:, :half].astype(BF16).astype(F32), U32) >> 16
    hi = lax.bitcast_convert_type(x[:, half:].astype(BF16).astype(F32), U32) & U32(HI_MASK)
    return lo | hi


def _unpack_halves(w):
    return (lax.bitcast_convert_type(w << 16, F32), lax.bitcast_convert_type(w & U32(HI_MASK), F32))


SUBLANES = 8
assert HALF_W == SUBLANES * LANE


def _store_row_tiles(ref, lead, w):
    m = w.shape[0]
    for s in range(SUBLANES):
        ref[lead + (pl.ds(s, m, stride=SUBLANES), slice(None))] = w[:, s * LANE:(s + 1) * LANE]


def _load_row_tiles(ref, lead, m):
    return jnp.concatenate([ref[lead + (pl.ds(s, m, stride=SUBLANES), slice(None))] for s in range(SUBLANES)], 1)


def _emit_x(xn, o_ref, ob_ref, op_ref):
    o_ref[...] = xn
    ob_ref[...] = xn.astype(BF16)
    _store_row_tiles(op_ref, (), _pack_halves(xn))


def _x_out(t, tm):
    row = lambda width: pl.BlockSpec((tm, width), lambda i, *_: (i, 0))
    return ([row(D_MODEL), row(D_MODEL), pl.BlockSpec((tm * SUBLANES, LANE), lambda i, *_: (i, 0))],
            [jax.ShapeDtypeStruct((t, D_MODEL), F32), jax.ShapeDtypeStruct((t, D_MODEL), BF16),
             jax.ShapeDtypeStruct((t * SUBLANES, LANE), U32)])


def _outproj_kernel(ya_ref, yb_ref, x_ref, w_ref, g_ref, b_ref, o_ref, ob_ref, op_ref):
    acc = jnp.dot(ya_ref[...], w_ref[0:HALF_W, :], preferred_element_type=F32)
    acc = acc + jnp.dot(yb_ref[...], w_ref[HALF_W:D_MODEL, :], preferred_element_type=F32)
    _emit_x(_layer_norm(ALPHA * x_ref[...] + acc, g_ref[...], b_ref[...]), o_ref, ob_ref, op_ref)


def _out_proj_ln(ya, yb, x, w, g, b, tm):
    t = x.shape[0]
    row = lambda width: pl.BlockSpec((tm, width), lambda i: (i, 0))
    out_specs, out_shape = _x_out(t, tm)
    return pl.pallas_call(
        _outproj_kernel, grid=(t // tm,),
        in_specs=[row(HALF_W), row(HALF_W), row(D_MODEL),
                  pl.BlockSpec((D_MODEL, D_MODEL), lambda i: (0, 0)),
                  pl.BlockSpec((1, D_MODEL), lambda i: (0, 0)),
                  pl.BlockSpec((1, D_MODEL), lambda i: (0, 0))],
        out_specs=out_specs, out_shape=out_shape,
        compiler_params=_cparams("parallel"), name="out_proj_ln")(
            ya, yb, x, w, g.reshape(1, D_MODEL), b.reshape(1, D_MODEL))


RANK_BITS = 20
RANK_MASK = (1 << RANK_BITS) - 1


def _router_kernel(x_ref, wr_ref, bias_ref, tri_ref, cnt0_ref, code_ref, gw_ref, cnt_ref, run_ref):
    step = pl.program_id(0)

    @pl.when(step == 0)
    def _():
        run_ref[...] = cnt0_ref[...]

    tm = x_ref.shape[0]
    logits = _dot_nt(wr_ref[...], x_ref[...])
    scores = jax.nn.sigmoid(logits)
    sel = scores + bias_ref[:, 0:1]
    neg = -jnp.inf
    sub = lax.broadcasted_iota(I32, (GROUP_SIZE, tm), 0).astype(F32)
    gsc = []
    for g in range(N_GROUPS):
        blk = sel[g * GROUP_SIZE:(g + 1) * GROUP_SIZE]
        m1 = jnp.max(blk, 0, keepdims=True)
        i1 = jnp.min(jnp.where(blk == m1, sub, float(GROUP_SIZE)), 0, keepdims=True)
        m2 = jnp.max(jnp.where(sub == i1, neg, blk), 0, keepdims=True)
        gsc.append(m1 + m2)
    cur = jnp.concatenate(gsc, 0)
    gio = lax.broadcasted_iota(I32, (N_GROUPS, tm), 0).astype(F32)
    gmask = jnp.zeros((N_GROUPS, tm), F32)
    for _ in range(TOPK_GROUPS):
        m = jnp.max(cur, 0, keepdims=True)
        i = jnp.min(jnp.where(cur == m, gio, float(N_GROUPS)), 0, keepdims=True)
        pick = gio == i
        gmask = jnp.where(pick, 1.0, gmask)
        cur = jnp.where(pick, neg, cur)
    emask = jnp.concatenate(
        [jnp.broadcast_to(gmask[g:g + 1], (GROUP_SIZE, tm)) for g in range(N_GROUPS)], 0)
    cur = jnp.where(emask > 0.5, sel, neg)
    eio = lax.broadcasted_iota(I32, (N_EXPERTS, tm), 0).astype(F32)
    member = jnp.zeros((N_EXPERTS, tm), F32)
    idxs, scs = [], []
    for _ in range(TOP_K):
        m = jnp.max(cur, 0, keepdims=True)
        i = jnp.min(jnp.where(cur == m, eio, float(N_EXPERTS)), 0, keepdims=True)
        pick = eio == i
        idxs.append(i)
        scs.append(jnp.sum(jnp.where(pick, scores, 0.0), 0, keepdims=True))
        member = jnp.where(pick, 1.0, member)
        cur = jnp.where(pick, neg, cur)
    idx = jnp.concatenate(idxs, 0)
    sc = jnp.concatenate(scs, 0)
    gw_ref[...] = sc / jnp.sum(sc, 0, keepdims=True) * ROUTE_SCALE
    before = jnp.dot(member.astype(BF16), tri_ref[...], preferred_element_type=F32) + run_ref[:, 0:1]
    ranks = [jnp.sum(jnp.where(eio == idxs[k], before, 0.0), 0, keepdims=True) for k in range(TOP_K)]
    rank = jnp.concatenate(ranks, 0).astype(I32)
    code_ref[...] = idx.astype(I32) * (1 << RANK_BITS) + rank
    run_ref[...] = run_ref[...] + jnp.sum(member, 1, keepdims=True)
    cnt_ref[...] = run_ref[...]


def _router(x, w_router, bias, tm, cnt0):
    t = x.shape[0]
    tri = jnp.asarray(np.arange(tm)[:, None] < np.arange(tm)[None, :], BF16)
    kt = lambda dt: jax.ShapeDtypeStruct((TOP_K, t), dt)
    return pl.pallas_call(
        _router_kernel, grid=(t // tm,),
        in_specs=[pl.BlockSpec((tm, D_MODEL), lambda i: (i, 0)),
                  pl.BlockSpec((N_EXPERTS, D_MODEL), lambda i: (0, 0)),
                  pl.BlockSpec((N_EXPERTS, LANE), lambda i: (0, 0)),
                  pl.BlockSpec((tm, tm), lambda i: (0, 0)),
                  pl.BlockSpec((N_EXPERTS, LANE), lambda i: (0, 0))],
        out_specs=[pl.BlockSpec((TOP_K, tm), lambda i: (0, i)),
                   pl.BlockSpec((TOP_K, tm), lambda i: (0, i)),
                   pl.BlockSpec((N_EXPERTS, LANE), lambda i: (0, 0))],
        out_shape=[kt(I32), kt(F32), jax.ShapeDtypeStruct((N_EXPERTS, LANE), F32)],
        scratch_shapes=[pltpu.VMEM((N_EXPERTS, LANE), F32)],
        compiler_params=_cparams("arbitrary"), name="moe_router")(
            x, w_router.T.astype(BF16), jnp.broadcast_to(bias[:, None], (N_EXPERTS, LANE)), tri, cnt0)


def _slots_kernel(start_ref, code_ref, pos_ref):
    code = code_ref[...]
    e = lax.shift_right_logical(code, RANK_BITS)
    r = code & RANK_MASK
    pos = r
    for x in range(N_EXPERTS):
        pos = jnp.where(e == x, r + start_ref[x], pos)
    pos_ref[...] = pos * SUBLANES


def _slots(code, slot_start):
    k, t = code.shape
    tile = min(t, 4096)
    gs = pltpu.PrefetchScalarGridSpec(
        num_scalar_prefetch=1, grid=(t // tile,),
        in_specs=[pl.BlockSpec((k, tile), lambda i, *_: (0, i))],
        out_specs=pl.BlockSpec((k, tile), lambda i, *_: (0, i)))
    return pl.pallas_call(_slots_kernel, grid_spec=gs, out_shape=jax.ShapeDtypeStruct((k, t), I32),
                          compiler_params=_cparams("parallel"), name="moe_slots")(slot_start, code)


def _row_tile(ref, row0, lead=()):
    return ref.at[lead + (pl.ds(pl.multiple_of(row0, SUBLANES), SUBLANES),)]


def _dispatch_kernel(fill_ref, pos_ref, x_ref, xb_ref, wsg_ref, wsu_ref, wsd_ref, *rest, eps, fill_sizes):
    xs_ref, sh_ref, zero_ref, sem, fill_sem = rest[-5:]
    step = pl.program_id(0)
    tm = x_ref.shape[0] // SUBLANES
    zero_ref[...] = jnp.zeros_like(zero_ref)

    def copy(r, k):
        return pltpu.make_async_copy(_row_tile(x_ref, r * SUBLANES), _row_tile(xs_ref, pos_ref[0, 0, k * tm + r]), sem)

    def start(r, carry):
        for k in range(TOP_K):
            copy(r, k).start()
        return carry

    def wait(r, carry):
        for k in range(TOP_K):
            copy(r, k).wait()
        return carry

    def fill_expert(e, carry):
        base = fill_ref[0, e]
        n = fill_ref[1, e]
        def zero_copy(p):
            first = base + (n & ~(2 * p - 1))
            return pltpu.make_async_copy(
                zero_ref.at[pl.ds(0, p * SUBLANES)],
                xs_ref.at[pl.ds(pl.multiple_of(first * SUBLANES, SUBLANES), p * SUBLANES)], fill_sem)

        for p in fill_sizes:
            pl.when((n & p) != 0)(zero_copy(p).start)
        for p in fill_sizes:
            pl.when((n & p) != 0)(zero_copy(p).wait)
        return carry

    lax.fori_loop(0, tm, start, 0)
    e0 = jnp.minimum(step * eps, N_EXPERTS)
    e1 = jnp.minimum(e0 + eps, N_EXPERTS)
    lax.fori_loop(e0, e1, fill_expert, 0)
    x = xb_ref[...]
    hg = jnp.dot(x, wsg_ref[...], preferred_element_type=F32)
    hu = jnp.dot(x, wsu_ref[...], preferred_element_type=F32)
    hb = (jax.nn.silu(hg) * hu).astype(BF16)
    sh_ref[...] = jnp.dot(hb, wsd_ref[...], preferred_element_type=F32).astype(BF16)
    lax.fori_loop(0, tm, wait, 0)


def _dispatch_shared(xp, xb, pos_tiles, fill, wsg, wsu, wsd, layer, nslot, tm, bm, into=None):
    t = xb.shape[0]
    steps = t // tm
    de = wsg.shape[-1]
    eps = -(-N_EXPERTS // steps)
    fill_sizes = tuple(1 << i for i in reversed(range((bm - 1).bit_length())))
    in_specs = [pl.BlockSpec((1, 1, TOP_K * tm), lambda i, *_: (i, 0, 0), memory_space=pltpu.SMEM),
                pl.BlockSpec((tm * SUBLANES, LANE), lambda i, *_: (i, 0)),
                pl.BlockSpec((tm, D_MODEL), lambda i, *_: (i, 0)),
                pl.BlockSpec((None, D_MODEL, de), lambda i, *_: (layer, 0, 0)),
                pl.BlockSpec((None, D_MODEL, de), lambda i, *_: (layer, 0, 0)),
                pl.BlockSpec((None, de, D_MODEL), lambda i, *_: (layer, 0, 0))]
    args = [fill, pos_tiles, xp, xb, wsg, wsu, wsd]
    if into is not None:
        in_specs.append(pl.BlockSpec(memory_space=pl.ANY))
        args.append(into)
    gs = pltpu.PrefetchScalarGridSpec(
        num_scalar_prefetch=1, grid=(steps,), in_specs=in_specs,
        out_specs=[pl.BlockSpec(memory_space=pl.ANY), pl.BlockSpec((tm, D_MODEL), lambda i, *_: (i, 0))],
        scratch_shapes=[pltpu.VMEM((fill_sizes[0] * SUBLANES, LANE), xp.dtype),
                        pltpu.SemaphoreType.DMA(()), pltpu.SemaphoreType.DMA(())])
    return pl.pallas_call(
        functools.partial(_dispatch_kernel, eps=eps, fill_sizes=fill_sizes), grid_spec=gs,
        out_shape=[jax.ShapeDtypeStruct((nslot * SUBLANES, LANE), xp.dtype),
                   jax.ShapeDtypeStruct((t, D_MODEL), BF16)],
        input_output_aliases={} if into is None else {len(args) - 1: 0},
        compiler_params=_cparams("arbitrary"), name="moe_dispatch_shared")(*args)


def _expert_kernel(be_ref, nu_ref, x_ref, wg_ref, wu_ref, wd_ref, o_ref):
    del be_ref

    @pl.when(pl.program_id(0) < nu_ref[0])
    def _():
        lo, hi = _unpack_halves(_load_row_tiles(x_ref, (), x_ref.shape[0] // SUBLANES))
        lo = lo.astype(BF16)
        hi = hi.astype(BF16)
        hg = (jnp.dot(lo, wg_ref[0, 0:HALF_W, :], preferred_element_type=F32)
              + jnp.dot(hi, wg_ref[0, HALF_W:D_MODEL, :], preferred_element_type=F32))
        hu = (jnp.dot(lo, wu_ref[0, 0:HALF_W, :], preferred_element_type=F32)
              + jnp.dot(hi, wu_ref[0, HALF_W:D_MODEL, :], preferred_element_type=F32))
        hb = (jax.nn.silu(hg) * hu).astype(BF16)
        _store_row_tiles(o_ref, (), _pack_halves(jnp.dot(hb, wd_ref[0], preferred_element_type=F32)))

    @pl.when(pl.program_id(0) >= nu_ref[0])
    def _():
        o_ref[...] = jnp.zeros_like(o_ref)


def _expert_ffn(xs, blk_e, nused, wg, wu, wd, layer, bm):
    nrows, width = xs.shape
    brows = bm * SUBLANES
    nb = nrows // brows
    de = wg.shape[-1]
    gs = pltpu.PrefetchScalarGridSpec(
        num_scalar_prefetch=2, grid=(nb,),
        in_specs=[pl.BlockSpec((brows, width), lambda b, be, nu: (jnp.minimum(b, nu[0] - 1), 0)),
                  pl.BlockSpec((None, 1, D_MODEL, de), lambda b, be, nu: (layer, be[b], 0, 0)),
                  pl.BlockSpec((None, 1, D_MODEL, de), lambda b, be, nu: (layer, be[b], 0, 0)),
                  pl.BlockSpec((None, 1, de, D_MODEL), lambda b, be, nu: (layer, be[b], 0, 0))],
        out_specs=pl.BlockSpec((brows, width), lambda b, be, nu: (b, 0)))
    return pl.pallas_call(
        _expert_kernel, grid_spec=gs,
        out_shape=jax.ShapeDtypeStruct((nrows, width), xs.dtype),
        compiler_params=_cparams("arbitrary"), name="moe_expert_ffn")(blk_e, nused, xs, wg, wu, wd)


def _combine_kernel(pos_ref, nxt_ref, gw_ref, x_ref, sh_ref, eo_ref, g_ref, b_ref, o_ref, ob_ref, op_ref,
                    buf, sems):
    step = pl.program_id(0)
    nsteps = pl.num_programs(0)
    tm = x_ref.shape[0]
    cur = lax.rem(step, 2)

    def copy(idx_ref, slot, r, k):
        return pltpu.make_async_copy(_row_tile(eo_ref, idx_ref[0, 0, k * tm + r]),
                                     _row_tile(buf, r * SUBLANES, (slot, k)), sems.at[slot])

    def gather(idx_ref, slot):
        def start(r, carry):
            for k in range(TOP_K):
                copy(idx_ref, slot, r, k).start()
            return carry
        lax.fori_loop(0, tm, start, 0)

    @pl.when(step == 0)
    def _():
        gather(pos_ref, 0)

    @pl.when(step + 1 < nsteps)
    def _():
        gather(nxt_ref, 1 - cur)

    def wait(r, carry):
        for k in range(TOP_K):
            copy(pos_ref, cur, r, k).wait()
        return carry

    lax.fori_loop(0, tm, wait, 0)
    sh = sh_ref[...].astype(F32)
    lo_acc = sh[:, :HALF_W]
    hi_acc = sh[:, HALF_W:]
    for k in range(TOP_K):
        lo, hi = _unpack_halves(_load_row_tiles(buf, (cur, k), tm))
        wk = gw_ref[:, k:k + 1]
        lo_acc = lo_acc + lo * wk
        hi_acc = hi_acc + hi * wk
    ffn = jnp.concatenate([lo_acc, hi_acc], 1)
    _emit_x(_layer_norm(ALPHA * x_ref[...] + ffn, g_ref[...], b_ref[...]), o_ref, ob_ref, op_ref)


def _combine_ln(pos_tiles, gw_t, x, sh, eo, g, b, tm):
    t = x.shape[0]
    steps = t // tm
    row = lambda width: pl.BlockSpec((tm, width), lambda i: (i, 0))
    const = pl.BlockSpec((1, D_MODEL), lambda i: (0, 0))
    idx = lambda f: pl.BlockSpec((1, 1, TOP_K * tm), f, memory_space=pltpu.SMEM)
    out_specs, out_shape = _x_out(t, tm)
    return pl.pallas_call(
        _combine_kernel, grid=(steps,),
        in_specs=[idx(lambda i: (i, 0, 0)), idx(lambda i: (jnp.minimum(i + 1, steps - 1), 0, 0)),
                  row(TOP_K), row(D_MODEL), row(D_MODEL), pl.BlockSpec(memory_space=pl.ANY), const, const],
        out_specs=out_specs, out_shape=out_shape,
        scratch_shapes=[pltpu.VMEM((2, TOP_K, tm * SUBLANES, LANE), U32), pltpu.SemaphoreType.DMA((2,))],
        compiler_params=_cparams("arbitrary"), name="moe_combine_ln")(
            pos_tiles, pos_tiles, gw_t, x, sh, eo, g.reshape(1, D_MODEL), b.reshape(1, D_MODEL))


def _moe_ln(groups, w_router, bias, wg, wu, wd, wsg, wsu, wsd, layer, g, b, bm):
    cnt = jnp.zeros((N_EXPERTS, LANE), F32)
    routed = []
    for x, xb, xp, tm in groups:
        code, gw, cnt = _router(xb, w_router, bias, tm, cnt)
        routed.append((code, gw))
    t_all = sum(x.shape[0] for x, _, _, _ in groups)
    assert t_all * TOP_K < (1 << RANK_BITS)
    counts = cnt[:, 0].astype(I32)
    padded = (counts + bm - 1) // bm * bm
    pad_end = jnp.cumsum(padded)
    slot_start = pad_end - padded
    fill = jnp.stack([slot_start + counts, padded - counts])
    nb = -(-(t_all * TOP_K + N_EXPERTS * (bm - 1)) // bm)
    blk_first = jnp.arange(nb, dtype=I32) * bm
    blk_e = jnp.minimum(jnp.sum((pad_end[None, :] <= blk_first[:, None]).astype(I32), 1), N_EXPERTS - 1)
    nused = (pad_end[-1] // bm).astype(I32).reshape(1)
    xs = None
    pos_tiles, shared = [], []
    for (x, xb, xp, tm), (code, _) in zip(groups, routed):
        nt = x.shape[0] // tm
        pos = _slots(code, slot_start)
        pos_tiles.append(pos.reshape(TOP_K, nt, tm).transpose(1, 0, 2).reshape(nt, 1, TOP_K * tm))
        xs, sh = _dispatch_shared(xp, xb, pos_tiles[-1], fill if xs is None else jnp.zeros_like(fill),
                                  wsg, wsu, wsd, layer, nb * bm, tm, bm, xs)
        shared.append(sh)
    eo = _expert_ffn(xs, blk_e, nused, wg, wu, wd, layer, bm)
    return [_combine_ln(pt, gw.T, x, sh, eo, g, b, tm)
            for (x, _, _, tm), (_, gw), pt, sh in zip(groups, routed, pos_tiles, shared)]


def _odd_w_in(w):
    sizes = (H_C * K_C, H_C * K_C, H_C * V_C, GLA_RANK, H_C * V_C, HALF_W, CONV_DIM, H_D)
    q, k, v, lr, r, z, xbc, dt = jnp.split(w, [int(s) for s in np.cumsum(sizes)[:-1]], axis=-1)
    pad = jnp.zeros((w.shape[0], OD_XBC - (OD_LRDT + GLA_RANK + H_D)), w.dtype)
    return jnp.concatenate([q, k, v, r, z, lr, dt, pad, xbc], -1)


def _mix_layer(grp, l, p):
    cfg, seq_len, bsz = grp["cfg"], grp["seq_len"], grp["bsz"]
    x, xb, outs = grp["x"], grp["xb"], grp["outs"]
    j = l // 2
    if l % 2 == 0:
        proj = _proj(x if xb is None else xb, p["ev_w_in"][j], cfg["ptm"], cfg["ptn"])
        ya, v_rows = _mixer_a(proj, p["ev_a_ln_g"][j], p["ev_a_ln_b"][j], p["ev_a_ws"][j],
                              p["ev_a_bs"][j], seq_len, cfg["arows"], cfg["emit_v"])
        yb, s_b = _mixer_b(proj, p["hgrn_lb_logits"], p["ev_b_norm_g"][j], grp["hgrn0"][j], seq_len,
                           cfg["rows"], l)
        outs["v_rows"] = v_rows
        outs["hgrn"] = s_b
        w_out = p["ev_w_out"][j]
    else:
        proj = _proj(x if xb is None else xb, p["od_w_in"][j], cfg["ptm"], cfg["ptn"])
        ya, s_c = _mixer_c(proj, p["od_c_gate_w2"][j], p["od_c_gate_b"][j], p["od_c_norm_g"][j],
                           grp["gla0"][j], seq_len, cfg["rows"])
        yb, s_d = _mixer_d(proj, p["od_d_conv_w"][j], p["od_d_conv_b"][j], p["od_d_dt_bias"][j],
                           p["od_d_a_log"][j], p["od_d_skip"][j], p["od_d_norm_g"][j],
                           grp["ssm0"][j], grp["conv0"][j], seq_len, cfg["rows"])
        keep = min(D_CONV - 1, seq_len)
        xbc = proj.reshape(bsz, seq_len, OD_PAD)[:, seq_len - keep:, OD_XBC:OD_XBC + CONV_DIM]
        outs["conv"] = jnp.concatenate([grp["conv0"][j], xbc], 1)[:, -(D_CONV - 1):]
        outs["gla"] = s_c
        outs["ssm"] = s_d
        w_out = p["od_w_out"][j]
    return _out_proj_ln(ya, yb, x, w_out, p["ln1_g"][l], p["ln1_b"][l], cfg["tm"])


def _group(x3, hgrn0, gla0, ssm0, conv0):
    bsz, seq_len, _ = x3.shape
    return dict(x=x3.reshape(bsz * seq_len, D_MODEL), xb=None, bsz=bsz, seq_len=seq_len, outs={},
                cfg=_config(bsz, seq_len), hgrn0=hgrn0, gla0=gla0, ssm0=ssm0, conv0=conv0)


def _config(bsz, seq_len):
    t = bsz * seq_len
    if seq_len % CHUNK == 0:
        return dict(tm=256, rows=min(256, seq_len), arows=min(256, seq_len),
                    ptm=min(1024, t), ptn=512, emit_v=False)
    return dict(tm=t, rows=seq_len, arows=t, ptm=t, ptn=512, emit_v=True)


MOE_BLOCK_ROWS = 512


def kernel(x_prompt, x_sample, state_b_hgrn, state_c_gla, state_d_ssm, state_d_conv, ev_w_in, ev_a_ln_g, ev_a_ln_b, ev_a_ws, ev_a_bs, ev_b_norm_g, ev_w_out, hgrn_lb_logits, od_w_in, od_c_gate_w2, od_c_gate_b, od_c_norm_g, od_d_conv_w, od_d_conv_b, od_d_dt_bias, od_d_a_log, od_d_skip, od_d_norm_g, od_w_out, ln1_g, ln1_b, ln2_g, ln2_b, moe_w_router, moe_router_bias, moe_w_gate, moe_w_up, moe_w_down, moe_ws_gate, moe_ws_up, moe_ws_down):
    p = dict(
        ev_w_in=ev_w_in.astype(BF16), ev_a_ln_g=ev_a_ln_g, ev_a_ln_b=ev_a_ln_b, ev_a_ws=ev_a_ws,
        ev_a_bs=ev_a_bs, ev_b_norm_g=ev_b_norm_g, ev_w_out=ev_w_out.astype(BF16),
        hgrn_lb_logits=hgrn_lb_logits,
        od_w_in=jnp.stack([_odd_w_in(od_w_in[j]) for j in range(od_w_in.shape[0])]).astype(BF16),
        od_c_gate_w2=od_c_gate_w2, od_c_gate_b=od_c_gate_b, od_c_norm_g=od_c_norm_g,
        od_d_conv_w=od_d_conv_w, od_d_conv_b=od_d_conv_b, od_d_dt_bias=od_d_dt_bias,
        od_d_a_log=od_d_a_log, od_d_skip=od_d_skip, od_d_norm_g=od_d_norm_g,
        od_w_out=od_w_out.astype(BF16), ln1_g=ln1_g, ln1_b=ln1_b, ln2_g=ln2_g, ln2_b=ln2_b,
        moe_w_router=moe_w_router, moe_router_bias=moe_router_bias,
        moe_w_gate=moe_w_gate.astype(BF16), moe_w_up=moe_w_up.astype(BF16),
        moe_w_down=moe_w_down.astype(BF16), moe_ws_gate=moe_ws_gate.astype(BF16),
        moe_ws_up=moe_ws_up.astype(BF16), moe_ws_down=moe_ws_down.astype(BF16))
    bp, lp, _ = x_prompt.shape
    n_even = state_b_hgrn.shape[0]
    n_odd = state_c_gla.shape[0]
    zeros = lambda n, *s: jnp.zeros((n, bp) + s, F32)
    bs, ls, _ = x_sample.shape
    groups = [_group(x_prompt, zeros(n_even, H_B, K_B, V_B), zeros(n_odd, H_C, K_C, V_C),
                     zeros(n_odd, H_D, N_D, P_D), zeros(n_odd, D_CONV - 1, CONV_DIM)),
              _group(x_sample, state_b_hgrn, state_c_gla, state_d_ssm, state_d_conv)]
    for l in range(DEPTH):
        mixed = [_mix_layer(grp, l, p) for grp in groups]
        res = _moe_ln([(x, xb, xp, grp["cfg"]["tm"]) for (x, xb, xp), grp in zip(mixed, groups)],
                      p["moe_w_router"][l], p["moe_router_bias"][l], p["moe_w_gate"], p["moe_w_up"],
                      p["moe_w_down"], p["moe_ws_gate"], p["moe_ws_up"], p["moe_ws_down"], l,
                      p["ln2_g"][l], p["ln2_b"][l], MOE_BLOCK_ROWS)
        for grp, (x, xb, _) in zip(groups, res):
            grp["x"], grp["xb"] = x, xb
    y_p = groups[0]["x"].reshape(bp, lp, D_MODEL)
    y_s = groups[1]["x"].reshape(bs, ls, D_MODEL)
    o_p, o_s = groups[0]["outs"], groups[1]["outs"]
    a_v = o_s["v_rows"].reshape(1, bs, ls, HALF_W)
    return (y_p, y_s, a_v, o_p["hgrn"][None], o_s["hgrn"][None], o_p["gla"][None], o_s["gla"][None],
            o_p["ssm"][None], o_s["ssm"][None], o_p["conv"][None], o_s["conv"][None])
```

```python
import functools
import math

import jax
import jax.numpy as jnp
import numpy as np
from jax import lax
from jax.experimental import pallas as pl
from jax.experimental.pallas import tpu as pltpu

F32 = jnp.float32
BF16 = jnp.bfloat16
I32 = jnp.int32

D_MODEL = 2048
DEPTH = 2
CHUNK = 64
SUB = 16
HALF_W = D_MODEL // 2
A_CHUNK = 128
H_A = 4
DA = HALF_W // H_A
H_B = 8
K_B = 128
V_B = HALF_W // H_B
H_C = 4
V_C = HALF_W // H_C
K_C = V_C // 2
GLA_RANK = 16
GLA_TAU = 16.0
P_D = 64
H_D = HALF_W // P_D
G_D = 2
HPG_D = H_D // G_D
N_D = 128
D_CONV = 4
CONV_DIM = HALF_W + 2 * G_D * N_D
N_EXPERTS = 64
N_GROUPS = 8
GROUP_SIZE = N_EXPERTS // N_GROUPS
TOPK_GROUPS = 4
TOP_K = 8
D_EXPERT = 512
ROUTE_SCALE = 2.5
ALPHA = (2 * DEPTH) ** 0.25
EPS = 1e-5
LANE = 128
VMEM_LIMIT = 56 * 1024 * 1024

OD_Q, OD_K, OD_V, OD_R, OD_Z, OD_LRDT, OD_XBC = 0, 512, 1024, 2048, 3072, 4096, 4608
OD_PAD = 6144
DT_OFF = GLA_RANK


def _cparams(*sem):
    return pltpu.CompilerParams(dimension_semantics=sem, vmem_limit_bytes=VMEM_LIMIT)


def _split3(x):
    hi = x.astype(BF16)
    r = x - hi.astype(F32)
    mid = r.astype(BF16)
    lo = (r - mid.astype(F32)).astype(BF16)
    return hi, mid, lo


def _split2(x):
    hi = x.astype(BF16)
    return hi, (x - hi.astype(F32)).astype(BF16)


def _dot_exact_l(a_bf16, x):
    return sum(jnp.dot(a_bf16, p, preferred_element_type=F32) for p in _split3(x))


def _dot_exact_r(x, b_bf16):
    return sum(jnp.dot(p, b_bf16, preferred_element_type=F32) for p in _split3(x))


def _dot_nt(a, b):
    return lax.dot_general(a, b, (((1,), (1,)), ((), ())), preferred_element_type=F32)


def _softplus(x):
    return jnp.maximum(x, 0.0) + jnp.log1p(jnp.exp(-jnp.abs(x)))


def _proj_kernel(x_ref, w_ref, o_ref, *scratch):
    if scratch:
        xb_ref, = scratch

        @pl.when(pl.program_id(1) == 0)
        def _():
            xb_ref[...] = x_ref[...].astype(BF16)
        x = xb_ref[...]
    else:
        x = x_ref[...]
    o_ref[...] = jnp.dot(x, w_ref[...], preferred_element_type=F32)


def _proj(x, w, tm, tn):
    t, k = x.shape
    n = w.shape[1]
    scratch = [] if x.dtype == BF16 else [pltpu.VMEM((tm, k), BF16)]
    return pl.pallas_call(
        _proj_kernel, grid=(t // tm, n // tn),
        in_specs=[pl.BlockSpec((tm, k), lambda i, j: (i, 0)),
                  pl.BlockSpec((k, tn), lambda i, j: (0, j))],
        out_specs=pl.BlockSpec((tm, tn), lambda i, j: (i, j)),
        out_shape=jax.ShapeDtypeStruct((t, n), F32),
        scratch_shapes=scratch, compiler_params=_cparams("parallel", "arbitrary"),
        name="in_proj")(x, w)


def _mixa_kernel(u_ref, v_ref, lng_ref, lnb_ref, w_ref, bs_ref, ya_ref, *vrows, ca):
    gu = jax.nn.gelu(u_ref[...])
    gv = jax.nn.gelu(v_ref[...])
    rows = gu.shape[0]
    for h in range(H_A):
        sl = slice(h * DA, (h + 1) * DA)
        vh = gv[:, sl]
        mu = jnp.mean(vh, -1, keepdims=True)
        d = vh - mu
        var = jnp.mean(d * d, -1, keepdims=True)
        vn = d * lax.rsqrt(var + EPS) * lng_ref[h] + lnb_ref[h]
        if vrows:
            vrows[0][:, sl] = vn
        vnb = vn.astype(BF16)
        for c in range(rows // ca):
            rs = slice(c * ca, (c + 1) * ca)
            s = jnp.dot(w_ref[h], vnb[rs], preferred_element_type=F32) + bs_ref[h]
            ya_ref[rs, sl] = (gu[rs, sl] * s).astype(BF16)


def _mixer_a(proj, lng, lnb, ws, bs, seq_len, rows, emit_v):
    t = proj.shape[0]
    ca = min(A_CHUNK, seq_len)
    pos = np.arange(A_CHUNK)
    mask = (pos[None, :] // CHUNK) <= (pos[:, None] // CHUNK)
    w = (ws * mask)[:, :ca, :ca].astype(BF16)
    bsb = jnp.broadcast_to(bs[:, :ca, None], (H_A, ca, DA)).astype(F32)
    nb = HALF_W // HALF_W
    out_shape = [jax.ShapeDtypeStruct((t, HALF_W), BF16)]
    out_specs = [pl.BlockSpec((rows, HALF_W), lambda i: (i, 0))]
    if emit_v:
        out_shape.append(jax.ShapeDtypeStruct((t, HALF_W), F32))
        out_specs.append(pl.BlockSpec((rows, HALF_W), lambda i: (i, 0)))
    del nb
    res = pl.pallas_call(
        functools.partial(_mixa_kernel, ca=ca), grid=(t // rows,),
        in_specs=[pl.BlockSpec((rows, HALF_W), lambda i: (i, 0)),
                  pl.BlockSpec((rows, HALF_W), lambda i: (i, 1)),
                  pl.BlockSpec((H_A, 1, DA), lambda i: (0, 0, 0)),
                  pl.BlockSpec((H_A, 1, DA), lambda i: (0, 0, 0)),
                  pl.BlockSpec((H_A, ca, ca), lambda i: (0, 0, 0)),
                  pl.BlockSpec((H_A, ca, DA), lambda i: (0, 0, 0))],
        out_specs=out_specs, out_shape=out_shape,
        compiler_params=_cparams("parallel"), name="mixer_a")(
            proj, proj, lng[:, None, :], lnb[:, None, :], w, bsb)
    return res if emit_v else (res[0], None)


def _scan_mats(c):
    i = np.arange(c)[:, None]
    j = np.arange(c)[None, :]
    tril = (j <= i)
    local = tril & ((i // SUB) == (j // SUB))
    ones = np.ones((c, c), bool)
    return jnp.asarray(np.concatenate([local, tril, ones], 0), BF16)


def _cat(parts, axis):
    return parts[0] if len(parts) == 1 else jnp.concatenate(parts, axis)


def _gla_intra(q, k, v, sc, c, nh, kd, vd):
    lc, cum, last = sc[0:c], sc[c:2 * c], sc[2 * c:3 * c]
    ns = c // SUB
    q_hi, q_lo = _split2(q * jnp.exp(lc))
    pre = cum - lc
    vb = v.astype(BF16)
    att = []
    for blk in range(ns):
        n = SUB * (blk + 1)
        r0 = SUB * blk
        k_hi, k_lo = _split2(k[:n] * jnp.exp(pre[r0:r0 + 1] - cum[:n]))
        rq = slice(r0, r0 + SUB)
        for h in range(nh):
            ks = slice(h * kd, (h + 1) * kd)
            att.append(_dot_nt(q_hi[rq, ks], k_hi[:, ks])
                       + (_dot_nt(q_hi[rq, ks], k_lo[:, ks]) + _dot_nt(q_lo[rq, ks], k_hi[:, ks])))
    attb = []
    for blk in range(ns):
        n = SUB * (blk + 1)
        keep = (lax.broadcasted_iota(I32, (SUB, n), 1) - SUB * blk) <= lax.broadcasted_iota(I32, (SUB, n), 0)
        attb += [jnp.where(keep, att[blk * nh + h], 0.0).astype(BF16) for h in range(nh)]
    rows_out = []
    for blk in range(ns):
        n = SUB * (blk + 1)
        rows_out.append(_cat([jnp.dot(attb[blk * nh + h], vb[:n, h * vd:(h + 1) * vd],
                                      preferred_element_type=F32) for h in range(nh)], 1))
    qc = (q * jnp.exp(cum)).astype(BF16)
    kc = (k * jnp.exp(last - cum)).astype(BF16)
    return _cat(rows_out, 0), qc, kc, v.T.astype(BF16), jnp.exp(last[0:1])


def _gla_state(intra, s_ref, nh, kd, vd):
    o_intra, qc, kc, vt, e_last = intra
    sts = [s_ref[h] for h in range(nh)]
    inter = [_dot_nt(qc[:, h * kd:(h + 1) * kd], sts[h].astype(BF16)) for h in range(nh)]
    upd = [jnp.dot(vt[h * vd:(h + 1) * vd], kc[:, h * kd:(h + 1) * kd], preferred_element_type=F32)
           for h in range(nh)]
    for h in range(nh):
        s_ref[h] = sts[h] * e_last[:, h * kd:(h + 1) * kd] + upd[h]
    return o_intra + _cat(inter, 1)


def _hgrn_kernel(q_ref, f_ref, i_ref, g_ref, lbl_ref, ng_ref, s0_ref, cm_ref, y_ref, so_ref, s_ref,
                 *, c, layer):
    step = pl.program_id(1)

    @pl.when(step == 0)
    def _():
        s_ref[...] = s0_ref[0]

    lg = lbl_ref[...]
    ex = jnp.exp(lg - jnp.max(lg, 0, keepdims=True))
    sm = ex / jnp.sum(ex, 0, keepdims=True)
    lb = jnp.sum(sm[:layer + 1], 0, keepdims=True)
    rows = q_ref.shape[0]
    intra = []
    for ci in range(rows // c):
        rs = slice(ci * c, (ci + 1) * c)
        f = lb + (1.0 - lb) * jax.nn.sigmoid(f_ref[rs])
        g = jnp.log(f)
        sc = _dot_exact_l(cm_ref[...], g)
        intra.append(_gla_intra(jax.nn.silu(q_ref[rs]), 1.0 - f, i_ref[rs], sc, c, H_B, K_B, V_B))
    for ci in range(rows // c):
        rs = slice(ci * c, (ci + 1) * c)
        o = _gla_state(intra[ci], s_ref, H_B, K_B, V_B)
        rn = _cat([jnp.broadcast_to(lax.rsqrt(jnp.mean(jnp.square(o[:, h * V_B:(h + 1) * V_B]), -1,
                                                        keepdims=True) + EPS), (c, V_B))
                   for h in range(H_B)], 1)
        y_ref[rs, :] = (o * rn * ng_ref[...] * jax.nn.sigmoid(g_ref[rs])).astype(BF16)

    @pl.when(step == pl.num_programs(1) - 1)
    def _():
        so_ref[0] = s_ref[...]


def _mixer_b(proj, lb_logits, norm_g, s0, seq_len, rows, layer):
    t = proj.shape[0]
    bsz = t // seq_len
    c = CHUNK if seq_len % CHUNK == 0 else seq_len
    spb = seq_len // rows
    s0t = jnp.swapaxes(s0, -1, -2)
    col = lambda j: pl.BlockSpec((rows, HALF_W), lambda b, s, j=j: (b * spb + s, j))
    nl = lb_logits.shape[0]
    y, st = pl.pallas_call(
        functools.partial(_hgrn_kernel, c=c, layer=layer), grid=(bsz, spb),
        in_specs=[col(2), col(3), col(4), col(5),
                  pl.BlockSpec((nl, HALF_W), lambda b, s: (0, 0)),
                  pl.BlockSpec((1, HALF_W), lambda b, s: (0, 0)),
                  pl.BlockSpec((1, H_B, V_B, K_B), lambda b, s: (b, 0, 0, 0)),
                  pl.BlockSpec((3 * c, c), lambda b, s: (0, 0))],
        out_specs=[pl.BlockSpec((rows, HALF_W), lambda b, s: (b * spb + s, 0)),
                   pl.BlockSpec((1, H_B, V_B, K_B), lambda b, s: (b, 0, 0, 0))],
        out_shape=[jax.ShapeDtypeStruct((t, HALF_W), BF16),
                   jax.ShapeDtypeStruct((bsz, H_B, V_B, K_B), F32)],
        scratch_shapes=[pltpu.VMEM((H_B, V_B, K_B), F32)],
        compiler_params=_cparams("parallel", "arbitrary"), name="mixer_b_hgrn")(
            proj, proj, proj, proj, lb_logits, norm_g.reshape(1, HALF_W), s0t, _scan_mats(c))
    return y, jnp.swapaxes(st, -1, -2)


def _glac_kernel(q_ref, k_ref, v_ref, r_ref, lrdt_ref, w2_ref, gb_ref, ng_ref, s0_ref, cm_ref,
                 y_ref, so_ref, s_ref, *, c):
    step = pl.program_id(1)

    @pl.when(step == 0)
    def _():
        s_ref[...] = s0_ref[0]

    rows = q_ref.shape[0]
    intra = []
    for ci in range(rows // c):
        rs = slice(ci * c, (ci + 1) * c)
        z = jnp.dot(lrdt_ref[rs], w2_ref[...], preferred_element_type=F32,
                    precision=lax.Precision.HIGHEST) + gb_ref[...]
        g = -_softplus(-z) / GLA_TAU
        sc = _dot_exact_l(cm_ref[...], g)
        intra.append(_gla_intra(q_ref[rs] * (K_C ** -0.5), k_ref[rs], v_ref[rs], sc, c, H_C, K_C, V_C))
    for ci in range(rows // c):
        rs = slice(ci * c, (ci + 1) * c)
        o = _gla_state(intra[ci], s_ref, H_C, K_C, V_C)
        rn = _cat([jnp.broadcast_to(lax.rsqrt(jnp.mean(jnp.square(o[:, h * V_C:(h + 1) * V_C]), -1,
                                                        keepdims=True) + EPS), (c, V_C))
                   for h in range(H_C)], 1)
        y_ref[rs, :] = (o * rn * ng_ref[...] * jax.nn.silu(r_ref[rs])).astype(BF16)

    @pl.when(step == pl.num_programs(1) - 1)
    def _():
        so_ref[0] = s_ref[...]


def _mixer_c(proj, gate_w2, gate_b, norm_g, s0, seq_len, rows):
    t = proj.shape[0]
    bsz = t // seq_len
    c = CHUNK if seq_len % CHUNK == 0 else seq_len
    spb = seq_len // rows
    s0t = jnp.swapaxes(s0, -1, -2)
    hk = H_C * K_C

    def col(width, off):
        return pl.BlockSpec((rows, width), lambda b, s: (b * spb + s, off // width))

    y, st = pl.pallas_call(
        functools.partial(_glac_kernel, c=c), grid=(bsz, spb),
        in_specs=[col(hk, OD_Q), col(hk, OD_K), col(HALF_W, OD_V), col(HALF_W, OD_R),
                  col(LANE, OD_LRDT),
                  pl.BlockSpec((LANE, hk), lambda b, s: (0, 0)),
                  pl.BlockSpec((1, hk), lambda b, s: (0, 0)),
                  pl.BlockSpec((1, HALF_W), lambda b, s: (0, 0)),
                  pl.BlockSpec((1, H_C, V_C, K_C), lambda b, s: (b, 0, 0, 0)),
                  pl.BlockSpec((3 * c, c), lambda b, s: (0, 0))],
        out_specs=[pl.BlockSpec((rows, HALF_W), lambda b, s: (b * spb + s, 0)),
                   pl.BlockSpec((1, H_C, V_C, K_C), lambda b, s: (b, 0, 0, 0))],
        out_shape=[jax.ShapeDtypeStruct((t, HALF_W), BF16),
                   jax.ShapeDtypeStruct((bsz, H_C, V_C, K_C), F32)],
        scratch_shapes=[pltpu.VMEM((H_C, V_C, K_C), F32)],
        compiler_params=_cparams("parallel", "arbitrary"), name="mixer_c_gla")(
            proj, proj, proj, proj, proj,
            jnp.pad(gate_w2, ((0, LANE - GLA_RANK), (0, 0))),
            gate_b.reshape(1, hk),
            norm_g.reshape(1, HALF_W), s0t, _scan_mats(c))
    return y, jnp.swapaxes(st, -1, -2)


GW = HPG_D * P_D


def _ssd_kernel(z_ref, xbc_ref, lrdt_ref, cw_ref, cb_ref, dtb_ref, alog_ref, skip_ref, ng_ref,
                ex_ref, eye_ref, cm_ref, conv0_ref, s0_ref, y_ref, so_ref, s_ref, tail_ref, *, c):
    step = pl.program_id(1)

    @pl.when(step == 0)
    def _():
        s_ref[...] = s0_ref[0]
        tail_ref[...] = conv0_ref[0]

    x = xbc_ref[...]
    rows = x.shape[0]
    tail = tail_ref[...]
    sub8 = lax.broadcasted_iota(I32, (8, CONV_DIM), 0)
    conv = x * cw_ref[D_CONV - 1:D_CONV]
    for sh in range(1, D_CONV):
        rolled = pltpu.roll(x, sh, 0)
        head = jnp.where(sub8 < sh, pltpu.roll(tail, sh, 0), rolled[0:8])
        xk = jnp.concatenate([head, rolled[8:]], 0) if rows > 8 else head
        conv = conv + xk * cw_ref[D_CONV - 1 - sh:D_CONV - sh]
    tail_ref[...] = x[rows - 8:rows]
    xc = jax.nn.silu(conv + cb_ref[...])
    a = -jnp.exp(alog_ref[...])
    ex = ex_ref[...]
    tri = lax.broadcasted_iota(I32, (c, c), 1) <= lax.broadcasted_iota(I32, (c, c), 0)
    group = lambda arr, g, w: arr[:, g * w:(g + 1) * w]
    intra = []
    for ci in range(rows // c):
        rs = slice(ci * c, (ci + 1) * c)
        xs = xc[rs, 0:HALF_W]
        bm = xc[rs, HALF_W:HALF_W + G_D * N_D]
        cmat = xc[rs, HALF_W + G_D * N_D:CONV_DIM].astype(BF16)
        dt = _softplus(lrdt_ref[rs] + dtb_ref[...])
        la = dt * a
        sc = _dot_exact_l(cm_ref[...], la)
        cum = sc[0:c]
        scx = _dot_exact_r(sc, ex)
        cumx = scx[0:c]
        lastx = scx[c:2 * c]
        xdt = xs * _dot_exact_r(dt, ex)
        cum_t = sum(_dot_nt(eye_ref[...], p) for p in _split3(cum))
        cbs = [_dot_nt(group(cmat, g, N_D), group(bm, g, N_D).astype(BF16)) for g in range(G_D)]
        xdtb = xdt.astype(BF16)
        wmats = []
        for h in range(H_D):
            hl = DT_OFF + h
            diff = cum[:, hl:hl + 1] - cum_t[hl:hl + 1, :]
            dec = jnp.where(tri, jnp.exp(jnp.minimum(diff, 0.0)), 0.0)
            wmats.append((cbs[h // HPG_D] * dec).astype(BF16))
        y_intra = _cat([jnp.dot(wmats[h], xdtb[:, h * P_D:(h + 1) * P_D], preferred_element_type=F32)
                        for h in range(H_D)], 1)
        intra.append((y_intra + xs * skip_ref[...], cmat, jnp.exp(cumx),
                      bm.T.astype(BF16), (xdt * jnp.exp(lastx - cumx)).astype(BF16), jnp.exp(lastx[0:1])))
    for ci in range(rows // c):
        rs = slice(ci * c, (ci + 1) * c)
        y_local, cmat, e_in, bm_t, xw, e_last = intra[ci]
        sgs = [s_ref[g] for g in range(G_D)]
        inter = [jnp.dot(group(cmat, g, N_D), sgs[g].astype(BF16), preferred_element_type=F32)
                 for g in range(G_D)]
        upd = [jnp.dot(bm_t[g * N_D:(g + 1) * N_D], group(xw, g, GW), preferred_element_type=F32)
               for g in range(G_D)]
        for g in range(G_D):
            s_ref[g] = sgs[g] * group(e_last, g, GW) + upd[g]
        yg = (y_local + _cat(inter, 1) * e_in) * jax.nn.silu(z_ref[rs])
        rn = _cat([jnp.broadcast_to(lax.rsqrt(jnp.mean(jnp.square(group(yg, g, GW)), -1, keepdims=True) + EPS),
                                    (c, GW)) for g in range(G_D)], 1)
        y_ref[rs, :] = (yg * rn * ng_ref[...]).astype(BF16)

    @pl.when(step == pl.num_programs(1) - 1)
    def _():
        so_ref[0] = s_ref[...]


def _mixer_d(proj, conv_w, conv_b, dt_bias, a_log, skip, norm_g, s0, conv0, seq_len, rows):
    t = proj.shape[0]
    bsz = t // seq_len
    c = CHUNK if seq_len % CHUNK == 0 else seq_len
    spb = seq_len // rows
    s0g = s0.reshape(bsz, G_D, HPG_D, N_D, P_D).transpose(0, 1, 3, 2, 4).reshape(bsz, G_D, N_D, GW)
    conv0p = jnp.pad(conv0, ((0, 0), (8 - (D_CONV - 1), 0), (0, 0)))
    expand_np = np.zeros((LANE, HALF_W), np.float32)
    expand_np[DT_OFF:DT_OFF + H_D] = np.repeat(np.eye(H_D), P_D, axis=1)
    expand = jnp.asarray(expand_np, BF16)
    eye = jnp.asarray(np.eye(LANE), BF16)
    i = np.arange(c)[:, None]
    j = np.arange(c)[None, :]
    cm = jnp.asarray(np.concatenate([j <= i, np.ones((c, c), bool)], 0), BF16)
    skipx = jnp.repeat(skip, P_D)[None, :]
    lane_pad = lambda v: jnp.pad(v.reshape(1, H_D), ((0, 0), (DT_OFF, LANE - DT_OFF - H_D)))

    def col(width, off):
        return pl.BlockSpec((rows, width), lambda b, s: (b * spb + s, off // width))

    def full(shape):
        return pl.BlockSpec(shape, lambda b, s: (0,) * len(shape))

    y, st = pl.pallas_call(
        functools.partial(_ssd_kernel, c=c), grid=(bsz, spb),
        in_specs=[col(HALF_W, OD_Z),
                  col(CONV_DIM, OD_XBC),
                  col(LANE, OD_LRDT),
                  full((D_CONV, CONV_DIM)), full((1, CONV_DIM)), full((1, LANE)), full((1, LANE)),
                  full((1, HALF_W)), full((1, HALF_W)), full((LANE, HALF_W)), full((LANE, LANE)),
                  full((2 * c, c)),
                  pl.BlockSpec((1, 8, CONV_DIM), lambda b, s: (b, 0, 0)),
                  pl.BlockSpec((1, G_D, N_D, GW), lambda b, s: (b, 0, 0, 0))],
        out_specs=[pl.BlockSpec((rows, HALF_W), lambda b, s: (b * spb + s, 0)),
                   pl.BlockSpec((1, G_D, N_D, GW), lambda b, s: (b, 0, 0, 0))],
        out_shape=[jax.ShapeDtypeStruct((t, HALF_W), BF16),
                   jax.ShapeDtypeStruct((bsz, G_D, N_D, GW), F32)],
        scratch_shapes=[pltpu.VMEM((G_D, N_D, GW), F32), pltpu.VMEM((8, CONV_DIM), F32)],
        compiler_params=_cparams("parallel", "arbitrary"), name="mixer_d_ssd")(
            proj, proj, proj, conv_w, conv_b.reshape(1, CONV_DIM), lane_pad(dt_bias),
            lane_pad(a_log), skipx, norm_g.reshape(1, HALF_W), expand, eye, cm, conv0p, s0g)
    st = st.reshape(bsz, G_D, N_D, HPG_D, P_D).transpose(0, 1, 3, 2, 4).reshape(bsz, H_D, N_D, P_D)
    return y, st


def _layer_norm(hpre, g, b):
    mu = jnp.mean(hpre, -1, keepdims=True)
    d = hpre - mu
    var = jnp.mean(d * d, -1, keepdims=True)
    return d * lax.rsqrt(var + EPS) * g + b


U32 = jnp.uint32
HI_MASK = 0xFFFF0000


def _pack_halves(x):
    half = x.shape[1] // 2
    lo = lax.bitcast_convert_type(x[:, :half].astype(BF16).astype(F32), U32) >> 16
    hi = lax.bitcast_convert_type(x[:, half:].astype(BF16).astype(F32), U32) & U32(HI_MASK)
    return lo | hi


def _unpack_halves(w):
    return (lax.bitcast_convert_type(w << 16, F32), lax.bitcast_convert_type(w & U32(HI_MASK), F32))


SUBLANES = 8
assert HALF_W == SUBLANES * LANE


def _store_row_tiles(ref, lead, w):
    m = w.shape[0]
    for s in range(SUBLANES):
        ref[lead + (pl.ds(s, m, stride=SUBLANES), slice(None))] = w[:, s * LANE:(s + 1) * LANE]


def _load_row_tiles(ref, lead, m):
    return jnp.concatenate([ref[lead + (pl.ds(s, m, stride=SUBLANES), slice(None))] for s in range(SUBLANES)], 1)


def _emit_x(xn, o_ref, ob_ref, op_ref):
    o_ref[...] = xn
    ob_ref[...] = xn.astype(BF16)
    _store_row_tiles(op_ref, (), _pack_halves(xn))


def _x_out(t, tm):
    row = lambda width: pl.BlockSpec((tm, width), lambda i, *_: (i, 0))
    return ([row(D_MODEL), row(D_MODEL), pl.BlockSpec((tm * SUBLANES, LANE), lambda i, *_: (i, 0))],
            [jax.ShapeDtypeStruct((t, D_MODEL), F32), jax.ShapeDtypeStruct((t, D_MODEL), BF16),
             jax.ShapeDtypeStruct((t * SUBLANES, LANE), U32)])


def _outproj_kernel(ya_ref, yb_ref, x_ref, w_ref, g_ref, b_ref, o_ref, ob_ref, op_ref):
    acc = jnp.dot(ya_ref[...], w_ref[0:HALF_W, :], preferred_element_type=F32)
    acc = acc + jnp.dot(yb_ref[...], w_ref[HALF_W:D_MODEL, :], preferred_element_type=F32)
    _emit_x(_layer_norm(ALPHA * x_ref[...] + acc, g_ref[...], b_ref[...]), o_ref, ob_ref, op_ref)


def _out_proj_ln(ya, yb, x, w, g, b, tm):
    t = x.shape[0]
    row = lambda width: pl.BlockSpec((tm, width), lambda i: (i, 0))
    out_specs, out_shape = _x_out(t, tm)
    return pl.pallas_call(
        _outproj_kernel, grid=(t // tm,),
        in_specs=[row(HALF_W), row(HALF_W), row(D_MODEL),
                  pl.BlockSpec((D_MODEL, D_MODEL), lambda i: (0, 0)),
                  pl.BlockSpec((1, D_MODEL), lambda i: (0, 0)),
                  pl.BlockSpec((1, D_MODEL), lambda i: (0, 0))],
        out_specs=out_specs, out_shape=out_shape,
        compiler_params=_cparams("parallel"), name="out_proj_ln")(
            ya, yb, x, w, g.reshape(1, D_MODEL), b.reshape(1, D_MODEL))


RANK_BITS = 20
RANK_MASK = (1 << RANK_BITS) - 1


def _router_kernel(x_ref, wr_ref, bias_ref, tri_ref, cnt0_ref, code_ref, gw_ref, cnt_ref, run_ref):
    step = pl.program_id(0)

    @pl.when(step == 0)
    def _():
        run_ref[...] = cnt0_ref[...]

    tm = x_ref.shape[0]
    logits = _dot_nt(wr_ref[...], x_ref[...])
    scores = jax.nn.sigmoid(logits)
    sel = scores + bias_ref[:, 0:1]
    neg = -jnp.inf
    sub = lax.broadcasted_iota(I32, (GROUP_SIZE, tm), 0).astype(F32)
    gsc = []
    for g in range(N_GROUPS):
        blk = sel[g * GROUP_SIZE:(g + 1) * GROUP_SIZE]
        m1 = jnp.max(blk, 0, keepdims=True)
        i1 = jnp.min(jnp.where(blk == m1, sub, float(GROUP_SIZE)), 0, keepdims=True)
        m2 = jnp.max(jnp.where(sub == i1, neg, blk), 0, keepdims=True)
        gsc.append(m1 + m2)
    cur = jnp.concatenate(gsc, 0)
    gio = lax.broadcasted_iota(I32, (N_GROUPS, tm), 0).astype(F32)
    gmask = jnp.zeros((N_GROUPS, tm), F32)
    for _ in range(TOPK_GROUPS):
        m = jnp.max(cur, 0, keepdims=True)
        i = jnp.min(jnp.where(cur == m, gio, float(N_GROUPS)), 0, keepdims=True)
        pick = gio == i
        gmask = jnp.where(pick, 1.0, gmask)
        cur = jnp.where(pick, neg, cur)
    emask = jnp.concatenate(
        [jnp.broadcast_to(gmask[g:g + 1], (GROUP_SIZE, tm)) for g in range(N_GROUPS)], 0)
    cur = jnp.where(emask > 0.5, sel, neg)
    eio = lax.broadcasted_iota(I32, (N_EXPERTS, tm), 0).astype(F32)
    member = jnp.zeros((N_EXPERTS, tm), F32)
    idxs, scs = [], []
    for _ in range(TOP_K):
        m = jnp.max(cur, 0, keepdims=True)
        i = jnp.min(jnp.where(cur == m, eio, float(N_EXPERTS)), 0, keepdims=True)
        pick = eio == i
        idxs.append(i)
        scs.append(jnp.sum(jnp.where(pick, scores, 0.0), 0, keepdims=True))
        member = jnp.where(pick, 1.0, member)
        cur = jnp.where(pick, neg, cur)
    idx = jnp.concatenate(idxs, 0)
    sc = jnp.concatenate(scs, 0)
    gw_ref[...] = sc / jnp.sum(sc, 0, keepdims=True) * ROUTE_SCALE
    before = jnp.dot(member.astype(BF16), tri_ref[...], preferred_element_type=F32) + run_ref[:, 0:1]
    ranks = [jnp.sum(jnp.where(eio == idxs[k], before, 0.0), 0, keepdims=True) for k in range(TOP_K)]
    rank = jnp.concatenate(ranks, 0).astype(I32)
    code_ref[...] = idx.astype(I32) * (1 << RANK_BITS) + rank
    run_ref[...] = run_ref[...] + jnp.sum(member, 1, keepdims=True)
    cnt_ref[...] = run_ref[...]


def _router(x, w_router, bias, tm, cnt0):
    t = x.shape[0]
    tri = jnp.asarray(np.arange(tm)[:, None] < np.arange(tm)[None, :], BF16)
    kt = lambda dt: jax.ShapeDtypeStruct((TOP_K, t), dt)
    return pl.pallas_call(
        _router_kernel, grid=(t // tm,),
        in_specs=[pl.BlockSpec((tm, D_MODEL), lambda i: (i, 0)),
                  pl.BlockSpec((N_EXPERTS, D_MODEL), lambda i: (0, 0)),
                  pl.BlockSpec((N_EXPERTS, LANE), lambda i: (0, 0)),
                  pl.BlockSpec((tm, tm), lambda i: (0, 0)),
                  pl.BlockSpec((N_EXPERTS, LANE), lambda i: (0, 0))],
        out_specs=[pl.BlockSpec((TOP_K, tm), lambda i: (0, i)),
                   pl.BlockSpec((TOP_K, tm), lambda i: (0, i)),
                   pl.BlockSpec((N_EXPERTS, LANE), lambda i: (0, 0))],
        out_shape=[kt(I32), kt(F32), jax.ShapeDtypeStruct((N_EXPERTS, LANE), F32)],
        scratch_shapes=[pltpu.VMEM((N_EXPERTS, LANE), F32)],
        compiler_params=_cparams("arbitrary"), name="moe_router")(
            x, w_router.T.astype(BF16), jnp.broadcast_to(bias[:, None], (N_EXPERTS, LANE)), tri, cnt0)


def _slots_kernel(start_ref, code_ref, pos_ref):
    code = code_ref[...]
    e = lax.shift_right_logical(code, RANK_BITS)
    r = code & RANK_MASK
    pos = r
    for x in range(N_EXPERTS):
        pos = jnp.where(e == x, r + start_ref[x], pos)
    pos_ref[...] = pos * SUBLANES


def _slots(code, slot_start):
    k, t = code.shape
    tile = min(t, 4096)
    gs = pltpu.PrefetchScalarGridSpec(
        num_scalar_prefetch=1, grid=(t // tile,),
        in_specs=[pl.BlockSpec((k, tile), lambda i, *_: (0, i))],
        out_specs=pl.BlockSpec((k, tile), lambda i, *_: (0, i)))
    return pl.pallas_call(_slots_kernel, grid_spec=gs, out_shape=jax.ShapeDtypeStruct((k, t), I32),
                          compiler_params=_cparams("parallel"), name="moe_slots")(slot_start, code)


def _row_tile(ref, row0, lead=()):
    return ref.at[lead + (pl.ds(pl.multiple_of(row0, SUBLANES), SUBLANES),)]


def _dispatch_kernel(fill_ref, pos_ref, x_ref, xb_ref, wsg_ref, wsu_ref, wsd_ref, *rest, eps, fill_sizes):
    xs_ref, sh_ref, zero_ref, sem, fill_sem = rest[-5:]
    step = pl.program_id(0)
    tm = x_ref.shape[0] // SUBLANES
    zero_ref[...] = jnp.zeros_like(zero_ref)

    def copy(r, k):
        return pltpu.make_async_copy(_row_tile(x_ref, r * SUBLANES), _row_tile(xs_ref, pos_ref[0, 0, k * tm + r]), sem)

    def start(r, carry):
        for k in range(TOP_K):
            copy(r, k).start()
        return carry

    def wait(r, carry):
        for k in range(TOP_K):
            copy(r, k).wait()
        return carry

    def fill_expert(e, carry):
        base = fill_ref[0, e]
        n = fill_ref[1, e]
        def zero_copy(p):
            first = base + (n & ~(2 * p - 1))
            return pltpu.make_async_copy(
                zero_ref.at[pl.ds(0, p * SUBLANES)],
                xs_ref.at[pl.ds(pl.multiple_of(first * SUBLANES, SUBLANES), p * SUBLANES)], fill_sem)

        for p in fill_sizes:
            pl.when((n & p) != 0)(zero_copy(p).start)
        for p in fill_sizes:
            pl.when((n & p) != 0)(zero_copy(p).wait)
        return carry

    lax.fori_loop(0, tm, start, 0)
    e0 = jnp.minimum(step * eps, N_EXPERTS)
    e1 = jnp.minimum(e0 + eps, N_EXPERTS)
    lax.fori_loop(e0, e1, fill_expert, 0)
    x = xb_ref[...]
    hg = jnp.dot(x, wsg_ref[...], preferred_element_type=F32)
    hu = jnp.dot(x, wsu_ref[...], preferred_element_type=F32)
    hb = (jax.nn.silu(hg) * hu).astype(BF16)
    sh_ref[...] = jnp.dot(hb, wsd_ref[...], preferred_element_type=F32).astype(BF16)
    lax.fori_loop(0, tm, wait, 0)


def _dispatch_shared(xp, xb, pos_tiles, fill, wsg, wsu, wsd, layer, nslot, tm, bm, into=None):
    t = xb.shape[0]
    steps = t // tm
    de = wsg.shape[-1]
    eps = -(-N_EXPERTS // steps)
    fill_sizes = tuple(1 << i for i in reversed(range((bm - 1).bit_length())))
    in_specs = [pl.BlockSpec((1, 1, TOP_K * tm), lambda i, *_: (i, 0, 0), memory_space=pltpu.SMEM),
                pl.BlockSpec((tm * SUBLANES, LANE), lambda i, *_: (i, 0)),
                pl.BlockSpec((tm, D_MODEL), lambda i, *_: (i, 0)),
                pl.BlockSpec((None, D_MODEL, de), lambda i, *_: (layer, 0, 0)),
                pl.BlockSpec((None, D_MODEL, de), lambda i, *_: (layer, 0, 0)),
                pl.BlockSpec((None, de, D_MODEL), lambda i, *_: (layer, 0, 0))]
    args = [fill, pos_tiles, xp, xb, wsg, wsu, wsd]
    if into is not None:
        in_specs.append(pl.BlockSpec(memory_space=pl.ANY))
        args.append(into)
    gs = pltpu.PrefetchScalarGridSpec(
        num_scalar_prefetch=1, grid=(steps,), in_specs=in_specs,
        out_specs=[pl.BlockSpec(memory_space=pl.ANY), pl.BlockSpec((tm, D_MODEL), lambda i, *_: (i, 0))],
        scratch_shapes=[pltpu.VMEM((fill_sizes[0] * SUBLANES, LANE), xp.dtype),
                        pltpu.SemaphoreType.DMA(()), pltpu.SemaphoreType.DMA(())])
    return pl.pallas_call(
        functools.partial(_dispatch_kernel, eps=eps, fill_sizes=fill_sizes), grid_spec=gs,
        out_shape=[jax.ShapeDtypeStruct((nslot * SUBLANES, LANE), xp.dtype),
                   jax.ShapeDtypeStruct((t, D_MODEL), BF16)],
        input_output_aliases={} if into is None else {len(args) - 1: 0},
        compiler_params=_cparams("arbitrary"), name="moe_dispatch_shared")(*args)


def _expert_kernel(be_ref, nu_ref, x_ref, wg_ref, wu_ref, wd_ref, o_ref):
    del be_ref

    @pl.when(pl.program_id(0) < nu_ref[0])
    def _():
        lo, hi = _unpack_halves(_load_row_tiles(x_ref, (), x_ref.shape[0] // SUBLANES))
        lo = lo.astype(BF16)
        hi = hi.astype(BF16)
        w = lambda ref, r0, r1: ref[0, r0:r1, :].astype(BF16)
        hg = (jnp.dot(lo, w(wg_ref, 0, HALF_W), preferred_element_type=F32)
              + jnp.dot(hi, w(wg_ref, HALF_W, D_MODEL), preferred_element_type=F32))
        hu = (jnp.dot(lo, w(wu_ref, 0, HALF_W), preferred_element_type=F32)
              + jnp.dot(hi, w(wu_ref, HALF_W, D_MODEL), preferred_element_type=F32))
        hb = (jax.nn.silu(hg) * hu).astype(BF16)
        out = jnp.dot(hb, w(wd_ref, 0, wd_ref.shape[1]), preferred_element_type=F32)
        _store_row_tiles(o_ref, (), _pack_halves(out))

    @pl.when(pl.program_id(0) >= nu_ref[0])
    def _():
        o_ref[...] = jnp.zeros_like(o_ref)


def _expert_ffn(xs, blk_e, nused, wg, wu, wd, layer, bm):
    nrows, width = xs.shape
    brows = bm * SUBLANES
    nb = nrows // brows
    de = wg.shape[-1]
    gs = pltpu.PrefetchScalarGridSpec(
        num_scalar_prefetch=2, grid=(nb,),
        in_specs=[pl.BlockSpec((brows, width), lambda b, be, nu: (jnp.minimum(b, nu[0] - 1), 0)),
                  pl.BlockSpec((None, 1, D_MODEL, de), lambda b, be, nu: (layer, be[b], 0, 0)),
                  pl.BlockSpec((None, 1, D_MODEL, de), lambda b, be, nu: (layer, be[b], 0, 0)),
                  pl.BlockSpec((None, 1, de, D_MODEL), lambda b, be, nu: (layer, be[b], 0, 0))],
        out_specs=pl.BlockSpec((brows, width), lambda b, be, nu: (b, 0)))
    return pl.pallas_call(
        _expert_kernel, grid_spec=gs,
        out_shape=jax.ShapeDtypeStruct((nrows, width), xs.dtype),
        compiler_params=_cparams("arbitrary"), name="moe_expert_ffn")(blk_e, nused, xs, wg, wu, wd)


def _combine_kernel(pos_ref, nxt_ref, gw_ref, x_ref, sh_ref, eo_ref, g_ref, b_ref, o_ref, ob_ref, op_ref,
                    buf, sems):
    step = pl.program_id(0)
    nsteps = pl.num_programs(0)
    tm = x_ref.shape[0]
    cur = lax.rem(step, 2)

    def copy(idx_ref, slot, r, k):
        return pltpu.make_async_copy(_row_tile(eo_ref, idx_ref[0, 0, k * tm + r]),
                                     _row_tile(buf, r * SUBLANES, (slot, k)), sems.at[slot])

    def gather(idx_ref, slot):
        def start(r, carry):
            for k in range(TOP_K):
                copy(idx_ref, slot, r, k).start()
            return carry
        lax.fori_loop(0, tm, start, 0)

    @pl.when(step == 0)
    def _():
        gather(pos_ref, 0)

    @pl.when(step + 1 < nsteps)
    def _():
        gather(nxt_ref, 1 - cur)

    def wait(r, carry):
        for k in range(TOP_K):
            copy(pos_ref, cur, r, k).wait()
        return carry

    lax.fori_loop(0, tm, wait, 0)
    sh = sh_ref[...].astype(F32)
    lo_acc = sh[:, :HALF_W]
    hi_acc = sh[:, HALF_W:]
    for k in range(TOP_K):
        lo, hi = _unpack_halves(_load_row_tiles(buf, (cur, k), tm))
        wk = gw_ref[:, k:k + 1]
        lo_acc = lo_acc + lo * wk
        hi_acc = hi_acc + hi * wk
    ffn = jnp.concatenate([lo_acc, hi_acc], 1)
    _emit_x(_layer_norm(ALPHA * x_ref[...] + ffn, g_ref[...], b_ref[...]), o_ref, ob_ref, op_ref)


def _combine_ln(pos_tiles, gw_t, x, sh, eo, g, b, tm):
    t = x.shape[0]
    steps = t // tm
    row = lambda width: pl.BlockSpec((tm, width), lambda i: (i, 0))
    const = pl.BlockSpec((1, D_MODEL), lambda i: (0, 0))
    idx = lambda f: pl.BlockSpec((1, 1, TOP_K * tm), f, memory_space=pltpu.SMEM)
    out_specs, out_shape = _x_out(t, tm)
    return pl.pallas_call(
        _combine_kernel, grid=(steps,),
        in_specs=[idx(lambda i: (i, 0, 0)), idx(lambda i: (jnp.minimum(i + 1, steps - 1), 0, 0)),
                  row(TOP_K), row(D_MODEL), row(D_MODEL), pl.BlockSpec(memory_space=pl.ANY), const, const],
        out_specs=out_specs, out_shape=out_shape,
        scratch_shapes=[pltpu.VMEM((2, TOP_K, tm * SUBLANES, LANE), U32), pltpu.SemaphoreType.DMA((2,))],
        compiler_params=_cparams("arbitrary"), name="moe_combine_ln")(
            pos_tiles, pos_tiles, gw_t, x, sh, eo, g.reshape(1, D_MODEL), b.reshape(1, D_MODEL))


def _moe_ln(groups, w_router, bias, wg, wu, wd, wsg, wsu, wsd, layer, g, b, bm):
    cnt = jnp.zeros((N_EXPERTS, LANE), F32)
    routed = []
    for x, xb, xp, tm in groups:
        code, gw, cnt = _router(xb, w_router, bias, tm, cnt)
        routed.append((code, gw))
    t_all = sum(x.shape[0] for x, _, _, _ in groups)
    assert t_all * TOP_K < (1 << RANK_BITS)
    counts = cnt[:, 0].astype(I32)
    padded = (counts + bm - 1) // bm * bm
    pad_end = jnp.cumsum(padded)
    slot_start = pad_end - padded
    fill = jnp.stack([slot_start + counts, padded - counts])
    nb = -(-(t_all * TOP_K + N_EXPERTS * (bm - 1)) // bm)
    blk_first = jnp.arange(nb, dtype=I32) * bm
    blk_e = jnp.minimum(jnp.sum((pad_end[None, :] <= blk_first[:, None]).astype(I32), 1), N_EXPERTS - 1)
    nused = (pad_end[-1] // bm).astype(I32).reshape(1)
    xs = None
    pos_tiles, shared = [], []
    for (x, xb, xp, tm), (code, _) in zip(groups, routed):
        nt = x.shape[0] // tm
        pos = _slots(code, slot_start)
        pos_tiles.append(pos.reshape(TOP_K, nt, tm).transpose(1, 0, 2).reshape(nt, 1, TOP_K * tm))
        xs, sh = _dispatch_shared(xp, xb, pos_tiles[-1], fill if xs is None else jnp.zeros_like(fill),
                                  wsg, wsu, wsd, layer, nb * bm, tm, bm, xs)
        shared.append(sh)
    eo = _expert_ffn(xs, blk_e, nused, wg, wu, wd, layer, bm)
    return [_combine_ln(pt, gw.T, x, sh, eo, g, b, tm)
            for (x, _, _, tm), (_, gw), pt, sh in zip(groups, routed, pos_tiles, shared)]


def _odd_w_in(w):
    sizes = (H_C * K_C, H_C * K_C, H_C * V_C, GLA_RANK, H_C * V_C, HALF_W, CONV_DIM, H_D)
    q, k, v, lr, r, z, xbc, dt = jnp.split(w, [int(s) for s in np.cumsum(sizes)[:-1]], axis=-1)
    pad = jnp.zeros((w.shape[0], OD_XBC - (OD_LRDT + GLA_RANK + H_D)), w.dtype)
    return jnp.concatenate([q, k, v, r, z, lr, dt, pad, xbc], -1)


def _mix_layer(grp, l, p):
    cfg, seq_len, bsz = grp["cfg"], grp["seq_len"], grp["bsz"]
    x, xb, outs = grp["x"], grp["xb"], grp["outs"]
    j = l // 2
    if l % 2 == 0:
        proj = _proj(x if xb is None else xb, p["ev_w_in"][j], cfg["ptm"], cfg["ptn"])
        ya, v_rows = _mixer_a(proj, p["ev_a_ln_g"][j], p["ev_a_ln_b"][j], p["ev_a_ws"][j],
                              p["ev_a_bs"][j], seq_len, cfg["arows"], cfg["emit_v"])
        yb, s_b = _mixer_b(proj, p["hgrn_lb_logits"], p["ev_b_norm_g"][j], grp["hgrn0"][j], seq_len,
                           cfg["rows"], l)
        outs["v_rows"] = v_rows
        outs["hgrn"] = s_b
        w_out = p["ev_w_out"][j]
    else:
        proj = _proj(x if xb is None else xb, p["od_w_in"][j], cfg["ptm"], cfg["ptn"])
        ya, s_c = _mixer_c(proj, p["od_c_gate_w2"][j], p["od_c_gate_b"][j], p["od_c_norm_g"][j],
                           grp["gla0"][j], seq_len, cfg["rows"])
        yb, s_d = _mixer_d(proj, p["od_d_conv_w"][j], p["od_d_conv_b"][j], p["od_d_dt_bias"][j],
                           p["od_d_a_log"][j], p["od_d_skip"][j], p["od_d_norm_g"][j],
                           grp["ssm0"][j], grp["conv0"][j], seq_len, cfg["rows"])
        keep = min(D_CONV - 1, seq_len)
        xbc = proj.reshape(bsz, seq_len, OD_PAD)[:, seq_len - keep:, OD_XBC:OD_XBC + CONV_DIM]
        outs["conv"] = jnp.concatenate([grp["conv0"][j], xbc], 1)[:, -(D_CONV - 1):]
        outs["gla"] = s_c
        outs["ssm"] = s_d
        w_out = p["od_w_out"][j]
    return _out_proj_ln(ya, yb, x, w_out, p["ln1_g"][l], p["ln1_b"][l], cfg["tm"])


def _group(x3, hgrn0, gla0, ssm0, conv0):
    bsz, seq_len, _ = x3.shape
    return dict(x=x3.reshape(bsz * seq_len, D_MODEL), xb=None, bsz=bsz, seq_len=seq_len, outs={},
                cfg=_config(bsz, seq_len), hgrn0=hgrn0, gla0=gla0, ssm0=ssm0, conv0=conv0)


def _config(bsz, seq_len):
    t = bsz * seq_len
    if seq_len % CHUNK == 0:
        return dict(tm=256, rows=min(256, seq_len), arows=min(256, seq_len),
                    ptm=min(1024, t), ptn=1024, emit_v=False)
    return dict(tm=t, rows=seq_len, arows=t, ptm=t, ptn=512, emit_v=True)


MOE_BLOCK_ROWS = 512


def kernel(x_prompt, x_sample, state_b_hgrn, state_c_gla, state_d_ssm, state_d_conv, ev_w_in, ev_a_ln_g, ev_a_ln_b, ev_a_ws, ev_a_bs, ev_b_norm_g, ev_w_out, hgrn_lb_logits, od_w_in, od_c_gate_w2, od_c_gate_b, od_c_norm_g, od_d_conv_w, od_d_conv_b, od_d_dt_bias, od_d_a_log, od_d_skip, od_d_norm_g, od_w_out, ln1_g, ln1_b, ln2_g, ln2_b, moe_w_router, moe_router_bias, moe_w_gate, moe_w_up, moe_w_down, moe_ws_gate, moe_ws_up, moe_ws_down):
    p = dict(
        ev_w_in=ev_w_in.astype(BF16), ev_a_ln_g=ev_a_ln_g, ev_a_ln_b=ev_a_ln_b, ev_a_ws=ev_a_ws,
        ev_a_bs=ev_a_bs, ev_b_norm_g=ev_b_norm_g, ev_w_out=ev_w_out.astype(BF16),
        hgrn_lb_logits=hgrn_lb_logits,
        od_w_in=jnp.stack([_odd_w_in(od_w_in[j]) for j in range(od_w_in.shape[0])]).astype(BF16),
        od_c_gate_w2=od_c_gate_w2, od_c_gate_b=od_c_gate_b, od_c_norm_g=od_c_norm_g,
        od_d_conv_w=od_d_conv_w, od_d_conv_b=od_d_conv_b, od_d_dt_bias=od_d_dt_bias,
        od_d_a_log=od_d_a_log, od_d_skip=od_d_skip, od_d_norm_g=od_d_norm_g,
        od_w_out=od_w_out.astype(BF16), ln1_g=ln1_g, ln1_b=ln1_b, ln2_g=ln2_g, ln2_b=ln2_b,
        moe_w_router=moe_w_router, moe_router_bias=moe_router_bias,
        moe_w_gate=moe_w_gate, moe_w_up=moe_w_up, moe_w_down=moe_w_down,
        moe_ws_gate=moe_ws_gate.astype(BF16),
        moe_ws_up=moe_ws_up.astype(BF16), moe_ws_down=moe_ws_down.astype(BF16))
    bp, lp, _ = x_prompt.shape
    n_even = state_b_hgrn.shape[0]
    n_odd = state_c_gla.shape[0]
    zeros = lambda n, *s: jnp.zeros((n, bp) + s, F32)
    bs, ls, _ = x_sample.shape
    groups = [_group(x_prompt, zeros(n_even, H_B, K_B, V_B), zeros(n_odd, H_C, K_C, V_C),
                     zeros(n_odd, H_D, N_D, P_D), zeros(n_odd, D_CONV - 1, CONV_DIM)),
              _group(x_sample, state_b_hgrn, state_c_gla, state_d_ssm, state_d_conv)]
    for l in range(DEPTH):
        mixed = [_mix_layer(grp, l, p) for grp in groups]
        res = _moe_ln([(x, xb, xp, grp["cfg"]["tm"]) for (x, xb, xp), grp in zip(mixed, groups)],
                      p["moe_w_router"][l], p["moe_router_bias"][l], p["moe_w_gate"], p["moe_w_up"],
                      p["moe_w_down"], p["moe_ws_gate"], p["moe_ws_up"], p["moe_ws_down"], l,
                      p["ln2_g"][l], p["ln2_b"][l], MOE_BLOCK_ROWS)
        for grp, (x, xb, _) in zip(groups, res):
            grp["x"], grp["xb"] = x, xb
    y_p = groups[0]["x"].reshape(bp, lp, D_MODEL)
    y_s = groups[1]["x"].reshape(bs, ls, D_MODEL)
    o_p, o_s = groups[0]["outs"], groups[1]["outs"]
    a_v = o_s["v_rows"].reshape(1, bs, ls, HALF_W)
    return (y_p, y_s, a_v, o_p["hgrn"][None], o_s["hgrn"][None], o_p["gla"][None], o_s["gla"][None],
            o_p["ssm"][None], o_s["ssm"][None], o_p["conv"][None], o_s["conv"][None])
```

```python
import functools
import math

import jax
import jax.numpy as jnp
import numpy as np
from jax import lax
from jax.experimental import pallas as pl
from jax.experimental.pallas import tpu as pltpu

F32 = jnp.float32
BF16 = jnp.bfloat16
I32 = jnp.int32

D_MODEL = 2048
DEPTH = 2
CHUNK = 64
SUB = 16
HALF_W = D_MODEL // 2
A_CHUNK = 128
H_A = 4
DA = HALF_W // H_A
H_B = 8
K_B = 128
V_B = HALF_W // H_B
H_C = 4
V_C = HALF_W // H_C
K_C = V_C // 2
GLA_RANK = 16
GLA_TAU = 16.0
P_D = 64
H_D = HALF_W // P_D
G_D = 2
HPG_D = H_D // G_D
N_D = 128
D_CONV = 4
CONV_DIM = HALF_W + 2 * G_D * N_D
N_EXPERTS = 64
N_GROUPS = 8
GROUP_SIZE = N_EXPERTS // N_GROUPS
TOPK_GROUPS = 4
TOP_K = 8
D_EXPERT = 512
ROUTE_SCALE = 2.5
ALPHA = (2 * DEPTH) ** 0.25
EPS = 1e-5
LANE = 128
VMEM_LIMIT = 56 * 1024 * 1024

OD_Q, OD_K, OD_V, OD_R, OD_Z, OD_LRDT, OD_XBC = 0, 512, 1024, 2048, 3072, 4096, 4608
OD_PAD = 6144
DT_OFF = GLA_RANK


def _cparams(*sem):
    return pltpu.CompilerParams(dimension_semantics=sem, vmem_limit_bytes=VMEM_LIMIT)


def _split3(x):
    hi = x.astype(BF16)
    r = x - hi.astype(F32)
    mid = r.astype(BF16)
    lo = (r - mid.astype(F32)).astype(BF16)
    return hi, mid, lo


def _split2(x):
    hi = x.astype(BF16)
    return hi, (x - hi.astype(F32)).astype(BF16)


def _dot_exact_l(a_bf16, x):
    return sum(jnp.dot(a_bf16, p, preferred_element_type=F32) for p in _split3(x))


def _dot_exact_r(x, b_bf16):
    return sum(jnp.dot(p, b_bf16, preferred_element_type=F32) for p in _split3(x))


def _dot_nt(a, b):
    return lax.dot_general(a, b, (((1,), (1,)), ((), ())), preferred_element_type=F32)


def _softplus(x):
    return jnp.maximum(x, 0.0) + jnp.log1p(jnp.exp(-jnp.abs(x)))


def _proj_kernel(x_ref, w_ref, o_ref, *scratch):
    if scratch:
        xb_ref, = scratch

        @pl.when(pl.program_id(1) == 0)
        def _():
            xb_ref[...] = x_ref[...].astype(BF16)
        x = xb_ref[...]
    else:
        x = x_ref[...]
    o_ref[...] = jnp.dot(x, w_ref[...], preferred_element_type=F32)


def _proj(x, w, tm, tn):
    t, k = x.shape
    n = w.shape[1]
    scratch = [] if x.dtype == BF16 else [pltpu.VMEM((tm, k), BF16)]
    return pl.pallas_call(
        _proj_kernel, grid=(t // tm, n // tn),
        in_specs=[pl.BlockSpec((tm, k), lambda i, j: (i, 0)),
                  pl.BlockSpec((k, tn), lambda i, j: (0, j))],
        out_specs=pl.BlockSpec((tm, tn), lambda i, j: (i, j)),
        out_shape=jax.ShapeDtypeStruct((t, n), F32),
        scratch_shapes=scratch, compiler_params=_cparams("parallel", "arbitrary"),
        name="in_proj")(x, w)


def _mixa_kernel(u_ref, v_ref, lng_ref, lnb_ref, w_ref, bs_ref, ya_ref, *vrows, ca):
    gu = jax.nn.gelu(u_ref[...])
    gv = jax.nn.gelu(v_ref[...])
    rows = gu.shape[0]
    for h in range(H_A):
        sl = slice(h * DA, (h + 1) * DA)
        vh = gv[:, sl]
        mu = jnp.mean(vh, -1, keepdims=True)
        d = vh - mu
        var = jnp.mean(d * d, -1, keepdims=True)
        vn = d * lax.rsqrt(var + EPS) * lng_ref[h] + lnb_ref[h]
        if vrows:
            vrows[0][:, sl] = vn
        vnb = vn.astype(BF16)
        for c in range(rows // ca):
            rs = slice(c * ca, (c + 1) * ca)
            s = jnp.dot(w_ref[h], vnb[rs], preferred_element_type=F32) + bs_ref[h]
            ya_ref[rs, sl] = (gu[rs, sl] * s).astype(BF16)


def _mixer_a(proj, lng, lnb, ws, bs, seq_len, rows, emit_v):
    t = proj.shape[0]
    ca = min(A_CHUNK, seq_len)
    pos = np.arange(A_CHUNK)
    mask = (pos[None, :] // CHUNK) <= (pos[:, None] // CHUNK)
    w = (ws * mask)[:, :ca, :ca].astype(BF16)
    bsb = jnp.broadcast_to(bs[:, :ca, None], (H_A, ca, DA)).astype(F32)
    nb = HALF_W // HALF_W
    out_shape = [jax.ShapeDtypeStruct((t, HALF_W), BF16)]
    out_specs = [pl.BlockSpec((rows, HALF_W), lambda i: (i, 0))]
    if emit_v:
        out_shape.append(jax.ShapeDtypeStruct((t, HALF_W), F32))
        out_specs.append(pl.BlockSpec((rows, HALF_W), lambda i: (i, 0)))
    del nb
    res = pl.pallas_call(
        functools.partial(_mixa_kernel, ca=ca), grid=(t // rows,),
        in_specs=[pl.BlockSpec((rows, HALF_W), lambda i: (i, 0)),
                  pl.BlockSpec((rows, HALF_W), lambda i: (i, 1)),
                  pl.BlockSpec((H_A, 1, DA), lambda i: (0, 0, 0)),
                  pl.BlockSpec((H_A, 1, DA), lambda i: (0, 0, 0)),
                  pl.BlockSpec((H_A, ca, ca), lambda i: (0, 0, 0)),
                  pl.BlockSpec((H_A, ca, DA), lambda i: (0, 0, 0))],
        out_specs=out_specs, out_shape=out_shape,
        compiler_params=_cparams("parallel"), name="mixer_a")(
            proj, proj, lng[:, None, :], lnb[:, None, :], w, bsb)
    return res if emit_v else (res[0], None)


def _scan_mats(c):
    i = np.arange(c)[:, None]
    j = np.arange(c)[None, :]
    tril = (j <= i)
    local = tril & ((i // SUB) == (j // SUB))
    ones = np.ones((c, c), bool)
    return jnp.asarray(np.concatenate([local, tril, ones], 0), BF16)


def _cat(parts, axis):
    return parts[0] if len(parts) == 1 else jnp.concatenate(parts, axis)


def _gla_intra(q, k, v, sc, c, nh, kd, vd):
    lc, cum, last = sc[0:c], sc[c:2 * c], sc[2 * c:3 * c]
    ns = c // SUB
    q_hi, q_lo = _split2(q * jnp.exp(lc))
    pre = cum - lc
    vb = v.astype(BF16)
    att = []
    for blk in range(ns):
        n = SUB * (blk + 1)
        r0 = SUB * blk
        k_hi, k_lo = _split2(k[:n] * jnp.exp(pre[r0:r0 + 1] - cum[:n]))
        rq = slice(r0, r0 + SUB)
        for h in range(nh):
            ks = slice(h * kd, (h + 1) * kd)
            att.append(_dot_nt(q_hi[rq, ks], k_hi[:, ks])
                       + (_dot_nt(q_hi[rq, ks], k_lo[:, ks]) + _dot_nt(q_lo[rq, ks], k_hi[:, ks])))
    attb = []
    for blk in range(ns):
        n = SUB * (blk + 1)
        keep = (lax.broadcasted_iota(I32, (SUB, n), 1) - SUB * blk) <= lax.broadcasted_iota(I32, (SUB, n), 0)
        attb += [jnp.where(keep, att[blk * nh + h], 0.0).astype(BF16) for h in range(nh)]
    rows_out = []
    for blk in range(ns):
        n = SUB * (blk + 1)
        rows_out.append(_cat([jnp.dot(attb[blk * nh + h], vb[:n, h * vd:(h + 1) * vd],
                                      preferred_element_type=F32) for h in range(nh)], 1))
    qc = (q * jnp.exp(cum)).astype(BF16)
    kc = (k * jnp.exp(last - cum)).astype(BF16)
    return _cat(rows_out, 0), qc, kc, v.T.astype(BF16), jnp.exp(last[0:1])


def _gla_state(intra, s_ref, nh, kd, vd):
    o_intra, qc, kc, vt, e_last = intra
    sts = [s_ref[h] for h in range(nh)]
    inter = [_dot_nt(qc[:, h * kd:(h + 1) * kd], sts[h].astype(BF16)) for h in range(nh)]
    upd = [jnp.dot(vt[h * vd:(h + 1) * vd], kc[:, h * kd:(h + 1) * kd], preferred_element_type=F32)
           for h in range(nh)]
    for h in range(nh):
        s_ref[h] = sts[h] * e_last[:, h * kd:(h + 1) * kd] + upd[h]
    return o_intra + _cat(inter, 1)


def _hgrn_kernel(q_ref, f_ref, i_ref, g_ref, lbl_ref, ng_ref, s0_ref, cm_ref, y_ref, so_ref, s_ref,
                 *, c, layer):
    step = pl.program_id(1)

    @pl.when(step == 0)
    def _():
        s_ref[...] = s0_ref[0]

    lg = lbl_ref[...]
    ex = jnp.exp(lg - jnp.max(lg, 0, keepdims=True))
    sm = ex / jnp.sum(ex, 0, keepdims=True)
    lb = jnp.sum(sm[:layer + 1], 0, keepdims=True)
    rows = q_ref.shape[0]
    intra = []
    for ci in range(rows // c):
        rs = slice(ci * c, (ci + 1) * c)
        f = lb + (1.0 - lb) * jax.nn.sigmoid(f_ref[rs])
        g = jnp.log(f)
        sc = _dot_exact_l(cm_ref[...], g)
        intra.append(_gla_intra(jax.nn.silu(q_ref[rs]), 1.0 - f, i_ref[rs], sc, c, H_B, K_B, V_B))
    for ci in range(rows // c):
        rs = slice(ci * c, (ci + 1) * c)
        o = _gla_state(intra[ci], s_ref, H_B, K_B, V_B)
        rn = _cat([jnp.broadcast_to(lax.rsqrt(jnp.mean(jnp.square(o[:, h * V_B:(h + 1) * V_B]), -1,
                                                        keepdims=True) + EPS), (c, V_B))
                   for h in range(H_B)], 1)
        y_ref[rs, :] = (o * rn * ng_ref[...] * jax.nn.sigmoid(g_ref[rs])).astype(BF16)

    @pl.when(step == pl.num_programs(1) - 1)
    def _():
        so_ref[0] = s_ref[...]


def _mixer_b(proj, lb_logits, norm_g, s0, seq_len, rows, layer):
    t = proj.shape[0]
    bsz = t // seq_len
    c = CHUNK if seq_len % CHUNK == 0 else seq_len
    spb = seq_len // rows
    s0t = jnp.swapaxes(s0, -1, -2)
    col = lambda j: pl.BlockSpec((rows, HALF_W), lambda b, s, j=j: (b * spb + s, j))
    nl = lb_logits.shape[0]
    y, st = pl.pallas_call(
        functools.partial(_hgrn_kernel, c=c, layer=layer), grid=(bsz, spb),
        in_specs=[col(2), col(3), col(4), col(5),
                  pl.BlockSpec((nl, HALF_W), lambda b, s: (0, 0)),
                  pl.BlockSpec((1, HALF_W), lambda b, s: (0, 0)),
                  pl.BlockSpec((1, H_B, V_B, K_B), lambda b, s: (b, 0, 0, 0)),
                  pl.BlockSpec((3 * c, c), lambda b, s: (0, 0))],
        out_specs=[pl.BlockSpec((rows, HALF_W), lambda b, s: (b * spb + s, 0)),
                   pl.BlockSpec((1, H_B, V_B, K_B), lambda b, s: (b, 0, 0, 0))],
        out_shape=[jax.ShapeDtypeStruct((t, HALF_W), BF16),
                   jax.ShapeDtypeStruct((bsz, H_B, V_B, K_B), F32)],
        scratch_shapes=[pltpu.VMEM((H_B, V_B, K_B), F32)],
        compiler_params=_cparams("parallel", "arbitrary"), name="mixer_b_hgrn")(
            proj, proj, proj, proj, lb_logits, norm_g.reshape(1, HALF_W), s0t, _scan_mats(c))
    return y, jnp.swapaxes(st, -1, -2)


def _glac_kernel(q_ref, k_ref, v_ref, r_ref, lrdt_ref, w2_ref, gb_ref, ng_ref, s0_ref, cm_ref,
                 y_ref, so_ref, s_ref, *, c):
    step = pl.program_id(1)

    @pl.when(step == 0)
    def _():
        s_ref[...] = s0_ref[0]

    rows = q_ref.shape[0]
    intra = []
    for ci in range(rows // c):
        rs = slice(ci * c, (ci + 1) * c)
        z = jnp.dot(lrdt_ref[rs], w2_ref[...], preferred_element_type=F32,
                    precision=lax.Precision.HIGHEST) + gb_ref[...]
        g = -_softplus(-z) / GLA_TAU
        sc = _dot_exact_l(cm_ref[...], g)
        intra.append(_gla_intra(q_ref[rs] * (K_C ** -0.5), k_ref[rs], v_ref[rs], sc, c, H_C, K_C, V_C))
    for ci in range(rows // c):
        rs = slice(ci * c, (ci + 1) * c)
        o = _gla_state(intra[ci], s_ref, H_C, K_C, V_C)
        rn = _cat([jnp.broadcast_to(lax.rsqrt(jnp.mean(jnp.square(o[:, h * V_C:(h + 1) * V_C]), -1,
                                                        keepdims=True) + EPS), (c, V_C))
                   for h in range(H_C)], 1)
        y_ref[rs, :] = (o * rn * ng_ref[...] * jax.nn.silu(r_ref[rs])).astype(BF16)

    @pl.when(step == pl.num_programs(1) - 1)
    def _():
        so_ref[0] = s_ref[...]


def _mixer_c(proj, gate_w2, gate_b, norm_g, s0, seq_len, rows):
    t = proj.shape[0]
    bsz = t // seq_len
    c = CHUNK if seq_len % CHUNK == 0 else seq_len
    spb = seq_len // rows
    s0t = jnp.swapaxes(s0, -1, -2)
    hk = H_C * K_C

    def col(width, off):
        return pl.BlockSpec((rows, width), lambda b, s: (b * spb + s, off // width))

    y, st = pl.pallas_call(
        functools.partial(_glac_kernel, c=c), grid=(bsz, spb),
        in_specs=[col(hk, OD_Q), col(hk, OD_K), col(HALF_W, OD_V), col(HALF_W, OD_R),
                  col(LANE, OD_LRDT),
                  pl.BlockSpec((LANE, hk), lambda b, s: (0, 0)),
                  pl.BlockSpec((1, hk), lambda b, s: (0, 0)),
                  pl.BlockSpec((1, HALF_W), lambda b, s: (0, 0)),
                  pl.BlockSpec((1, H_C, V_C, K_C), lambda b, s: (b, 0, 0, 0)),
                  pl.BlockSpec((3 * c, c), lambda b, s: (0, 0))],
        out_specs=[pl.BlockSpec((rows, HALF_W), lambda b, s: (b * spb + s, 0)),
                   pl.BlockSpec((1, H_C, V_C, K_C), lambda b, s: (b, 0, 0, 0))],
        out_shape=[jax.ShapeDtypeStruct((t, HALF_W), BF16),
                   jax.ShapeDtypeStruct((bsz, H_C, V_C, K_C), F32)],
        scratch_shapes=[pltpu.VMEM((H_C, V_C, K_C), F32)],
        compiler_params=_cparams("parallel", "arbitrary"), name="mixer_c_gla")(
            proj, proj, proj, proj, proj,
            jnp.pad(gate_w2, ((0, LANE - GLA_RANK), (0, 0))),
            gate_b.reshape(1, hk),
            norm_g.reshape(1, HALF_W), s0t, _scan_mats(c))
    return y, jnp.swapaxes(st, -1, -2)


GW = HPG_D * P_D


def _ssd_kernel(z_ref, xbc_ref, lrdt_ref, cw_ref, cb_ref, dtb_ref, alog_ref, skip_ref, ng_ref,
                ex_ref, eye_ref, cm_ref, conv0_ref, s0_ref, y_ref, so_ref, s_ref, tail_ref, *, c):
    step = pl.program_id(1)

    @pl.when(step == 0)
    def _():
        s_ref[...] = s0_ref[0]
        tail_ref[...] = conv0_ref[0]

    x = xbc_ref[...]
    rows = x.shape[0]
    tail = tail_ref[...]
    sub8 = lax.broadcasted_iota(I32, (8, CONV_DIM), 0)
    conv = x * cw_ref[D_CONV - 1:D_CONV]
    for sh in range(1, D_CONV):
        rolled = pltpu.roll(x, sh, 0)
        head = jnp.where(sub8 < sh, pltpu.roll(tail, sh, 0), rolled[0:8])
        xk = jnp.concatenate([head, rolled[8:]], 0) if rows > 8 else head
        conv = conv + xk * cw_ref[D_CONV - 1 - sh:D_CONV - sh]
    tail_ref[...] = x[rows - 8:rows]
    xc = jax.nn.silu(conv + cb_ref[...])
    a = -jnp.exp(alog_ref[...])
    ex = ex_ref[...]
    tri = lax.broadcasted_iota(I32, (c, c), 1) <= lax.broadcasted_iota(I32, (c, c), 0)
    group = lambda arr, g, w: arr[:, g * w:(g + 1) * w]
    intra = []
    for ci in range(rows // c):
        rs = slice(ci * c, (ci + 1) * c)
        xs = xc[rs, 0:HALF_W]
        bm = xc[rs, HALF_W:HALF_W + G_D * N_D]
        cmat = xc[rs, HALF_W + G_D * N_D:CONV_DIM].astype(BF16)
        dt = _softplus(lrdt_ref[rs] + dtb_ref[...])
        la = dt * a
        sc = _dot_exact_l(cm_ref[...], la)
        cum = sc[0:c]
        scx = _dot_exact_r(sc, ex)
        cumx = scx[0:c]
        lastx = scx[c:2 * c]
        xdt = xs * _dot_exact_r(dt, ex)
        cum_t = sum(_dot_nt(eye_ref[...], p) for p in _split3(cum))
        cbs = [_dot_nt(group(cmat, g, N_D), group(bm, g, N_D).astype(BF16)) for g in range(G_D)]
        xdtb = xdt.astype(BF16)
        wmats = []
        for h in range(H_D):
            hl = DT_OFF + h
            diff = cum[:, hl:hl + 1] - cum_t[hl:hl + 1, :]
            dec = jnp.where(tri, jnp.exp(jnp.minimum(diff, 0.0)), 0.0)
            wmats.append((cbs[h // HPG_D] * dec).astype(BF16))
        y_intra = _cat([jnp.dot(wmats[h], xdtb[:, h * P_D:(h + 1) * P_D], preferred_element_type=F32)
                        for h in range(H_D)], 1)
        intra.append((y_intra + xs * skip_ref[...], cmat, jnp.exp(cumx),
                      bm.T.astype(BF16), (xdt * jnp.exp(lastx - cumx)).astype(BF16), jnp.exp(lastx[0:1])))
    for ci in range(rows // c):
        rs = slice(ci * c, (ci + 1) * c)
        y_local, cmat, e_in, bm_t, xw, e_last = intra[ci]
        sgs = [s_ref[g] for g in range(G_D)]
        inter = [jnp.dot(group(cmat, g, N_D), sgs[g].astype(BF16), preferred_element_type=F32)
                 for g in range(G_D)]
        upd = [jnp.dot(bm_t[g * N_D:(g + 1) * N_D], group(xw, g, GW), preferred_element_type=F32)
               for g in range(G_D)]
        for g in range(G_D):
            s_ref[g] = sgs[g] * group(e_last, g, GW) + upd[g]
        yg = (y_local + _cat(inter, 1) * e_in) * jax.nn.silu(z_ref[rs])
        rn = _cat([jnp.broadcast_to(lax.rsqrt(jnp.mean(jnp.square(group(yg, g, GW)), -1, keepdims=True) + EPS),
                                    (c, GW)) for g in range(G_D)], 1)
        y_ref[rs, :] = (yg * rn * ng_ref[...]).astype(BF16)

    @pl.when(step == pl.num_programs(1) - 1)
    def _():
        so_ref[0] = s_ref[...]


def _mixer_d(proj, conv_w, conv_b, dt_bias, a_log, skip, norm_g, s0, conv0, seq_len, rows):
    t = proj.shape[0]
    bsz = t // seq_len
    c = CHUNK if seq_len % CHUNK == 0 else seq_len
    spb = seq_len // rows
    s0g = s0.reshape(bsz, G_D, HPG_D, N_D, P_D).transpose(0, 1, 3, 2, 4).reshape(bsz, G_D, N_D, GW)
    conv0p = jnp.pad(conv0, ((0, 0), (8 - (D_CONV - 1), 0), (0, 0)))
    expand_np = np.zeros((LANE, HALF_W), np.float32)
    expand_np[DT_OFF:DT_OFF + H_D] = np.repeat(np.eye(H_D), P_D, axis=1)
    expand = jnp.asarray(expand_np, BF16)
    eye = jnp.asarray(np.eye(LANE), BF16)
    i = np.arange(c)[:, None]
    j = np.arange(c)[None, :]
    cm = jnp.asarray(np.concatenate([j <= i, np.ones((c, c), bool)], 0), BF16)
    skipx = jnp.repeat(skip, P_D)[None, :]
    lane_pad = lambda v: jnp.pad(v.reshape(1, H_D), ((0, 0), (DT_OFF, LANE - DT_OFF - H_D)))

    def col(width, off):
        return pl.BlockSpec((rows, width), lambda b, s: (b * spb + s, off // width))

    def full(shape):
        return pl.BlockSpec(shape, lambda b, s: (0,) * len(shape))

    y, st = pl.pallas_call(
        functools.partial(_ssd_kernel, c=c), grid=(bsz, spb),
        in_specs=[col(HALF_W, OD_Z),
                  col(CONV_DIM, OD_XBC),
                  col(LANE, OD_LRDT),
                  full((D_CONV, CONV_DIM)), full((1, CONV_DIM)), full((1, LANE)), full((1, LANE)),
                  full((1, HALF_W)), full((1, HALF_W)), full((LANE, HALF_W)), full((LANE, LANE)),
                  full((2 * c, c)),
                  pl.BlockSpec((1, 8, CONV_DIM), lambda b, s: (b, 0, 0)),
                  pl.BlockSpec((1, G_D, N_D, GW), lambda b, s: (b, 0, 0, 0))],
        out_specs=[pl.BlockSpec((rows, HALF_W), lambda b, s: (b * spb + s, 0)),
                   pl.BlockSpec((1, G_D, N_D, GW), lambda b, s: (b, 0, 0, 0))],
        out_shape=[jax.ShapeDtypeStruct((t, HALF_W), BF16),
                   jax.ShapeDtypeStruct((bsz, G_D, N_D, GW), F32)],
        scratch_shapes=[pltpu.VMEM((G_D, N_D, GW), F32), pltpu.VMEM((8, CONV_DIM), F32)],
        compiler_params=_cparams("parallel", "arbitrary"), name="mixer_d_ssd")(
            proj, proj, proj, conv_w, conv_b.reshape(1, CONV_DIM), lane_pad(dt_bias),
            lane_pad(a_log), skipx, norm_g.reshape(1, HALF_W), expand, eye, cm, conv0p, s0g)
    st = st.reshape(bsz, G_D, N_D, HPG_D, P_D).transpose(0, 1, 3, 2, 4).reshape(bsz, H_D, N_D, P_D)
    return y, st


def _layer_norm(hpre, g, b):
    mu = jnp.mean(hpre, -1, keepdims=True)
    d = hpre - mu
    var = jnp.mean(d * d, -1, keepdims=True)
    return d * lax.rsqrt(var + EPS) * g + b


U32 = jnp.uint32
HI_MASK = 0xFFFF0000


def _pack_halves(x):
    half = x.shape[1] // 2
    lo = lax.bitcast_convert_type(x[:, :half].astype(BF16).astype(F32), U32) >> 16
    hi = lax.bitcast_convert_type(x[:, half:].astype(BF16).astype(F32), U32) & U32(HI_MASK)
    return lo | hi


def _unpack_halves(w):
    return (lax.bitcast_convert_type(w << 16, F32), lax.bitcast_convert_type(w & U32(HI_MASK), F32))


SUBLANES = 8
assert HALF_W == SUBLANES * LANE


def _store_row_tiles(ref, lead, w):
    m = w.shape[0]
    for s in range(SUBLANES):
        ref[lead + (pl.ds(s, m, stride=SUBLANES), slice(None))] = w[:, s * LANE:(s + 1) * LANE]


def _load_row_tiles(ref, lead, m):
    return jnp.concatenate([ref[lead + (pl.ds(s, m, stride=SUBLANES), slice(None))] for s in range(SUBLANES)], 1)


def _emit_x(xn, o_ref, ob_ref, op_ref):
    o_ref[...] = xn
    ob_ref[...] = xn.astype(BF16)
    _store_row_tiles(op_ref, (), _pack_halves(xn))


def _x_out(t, tm):
    row = lambda width: pl.BlockSpec((tm, width), lambda i, *_: (i, 0))
    return ([row(D_MODEL), row(D_MODEL), pl.BlockSpec((tm * SUBLANES, LANE), lambda i, *_: (i, 0))],
            [jax.ShapeDtypeStruct((t, D_MODEL), F32), jax.ShapeDtypeStruct((t, D_MODEL), BF16),
             jax.ShapeDtypeStruct((t * SUBLANES, LANE), U32)])


def _outproj_kernel(ya_ref, yb_ref, x_ref, w_ref, g_ref, b_ref, o_ref, ob_ref, op_ref):
    acc = jnp.dot(ya_ref[...], w_ref[0:HALF_W, :], preferred_element_type=F32)
    acc = acc + jnp.dot(yb_ref[...], w_ref[HALF_W:D_MODEL, :], preferred_element_type=F32)
    _emit_x(_layer_norm(ALPHA * x_ref[...] + acc, g_ref[...], b_ref[...]), o_ref, ob_ref, op_ref)


def _out_proj_ln(ya, yb, x, w, g, b, tm):
    t = x.shape[0]
    row = lambda width: pl.BlockSpec((tm, width), lambda i: (i, 0))
    out_specs, out_shape = _x_out(t, tm)
    return pl.pallas_call(
        _outproj_kernel, grid=(t // tm,),
        in_specs=[row(HALF_W), row(HALF_W), row(D_MODEL),
                  pl.BlockSpec((D_MODEL, D_MODEL), lambda i: (0, 0)),
                  pl.BlockSpec((1, D_MODEL), lambda i: (0, 0)),
                  pl.BlockSpec((1, D_MODEL), lambda i: (0, 0))],
        out_specs=out_specs, out_shape=out_shape,
        compiler_params=_cparams("parallel"), name="out_proj_ln")(
            ya, yb, x, w, g.reshape(1, D_MODEL), b.reshape(1, D_MODEL))


RANK_BITS = 20
RANK_MASK = (1 << RANK_BITS) - 1


def _router_kernel(x_ref, wr_ref, bias_ref, tri_ref, cnt0_ref, code_ref, gw_ref, cnt_ref, run_ref):
    step = pl.program_id(0)

    @pl.when(step == 0)
    def _():
        run_ref[...] = cnt0_ref[...]

    tm = x_ref.shape[0]
    logits = _dot_nt(wr_ref[...], x_ref[...])
    scores = jax.nn.sigmoid(logits)
    sel = scores + bias_ref[:, 0:1]
    neg = -jnp.inf
    sub = lax.broadcasted_iota(I32, (GROUP_SIZE, tm), 0).astype(F32)
    gsc = []
    for g in range(N_GROUPS):
        blk = sel[g * GROUP_SIZE:(g + 1) * GROUP_SIZE]
        m1 = jnp.max(blk, 0, keepdims=True)
        i1 = jnp.min(jnp.where(blk == m1, sub, float(GROUP_SIZE)), 0, keepdims=True)
        m2 = jnp.max(jnp.where(sub == i1, neg, blk), 0, keepdims=True)
        gsc.append(m1 + m2)
    cur = jnp.concatenate(gsc, 0)
    gio = lax.broadcasted_iota(I32, (N_GROUPS, tm), 0).astype(F32)
    gmask = jnp.zeros((N_GROUPS, tm), F32)
    for _ in range(TOPK_GROUPS):
        m = jnp.max(cur, 0, keepdims=True)
        i = jnp.min(jnp.where(cur == m, gio, float(N_GROUPS)), 0, keepdims=True)
        pick = gio == i
        gmask = jnp.where(pick, 1.0, gmask)
        cur = jnp.where(pick, neg, cur)
    emask = jnp.concatenate(
        [jnp.broadcast_to(gmask[g:g + 1], (GROUP_SIZE, tm)) for g in range(N_GROUPS)], 0)
    cur = jnp.where(emask > 0.5, sel, neg)
    eio = lax.broadcasted_iota(I32, (N_EXPERTS, tm), 0).astype(F32)
    member = jnp.zeros((N_EXPERTS, tm), F32)
    idxs, scs = [], []
    for _ in range(TOP_K):
        m = jnp.max(cur, 0, keepdims=True)
        i = jnp.min(jnp.where(cur == m, eio, float(N_EXPERTS)), 0, keepdims=True)
        pick = eio == i
        idxs.append(i)
        scs.append(jnp.sum(jnp.where(pick, scores, 0.0), 0, keepdims=True))
        member = jnp.where(pick, 1.0, member)
        cur = jnp.where(pick, neg, cur)
    idx = jnp.concatenate(idxs, 0)
    sc = jnp.concatenate(scs, 0)
    gw_ref[...] = sc / jnp.sum(sc, 0, keepdims=True) * ROUTE_SCALE
    before = jnp.dot(member.astype(BF16), tri_ref[...], preferred_element_type=F32) + run_ref[:, 0:1]
    ranks = [jnp.sum(jnp.where(eio == idxs[k], before, 0.0), 0, keepdims=True) for k in range(TOP_K)]
    rank = jnp.concatenate(ranks, 0).astype(I32)
    code_ref[...] = idx.astype(I32) * (1 << RANK_BITS) + rank
    run_ref[...] = run_ref[...] + jnp.sum(member, 1, keepdims=True)
    cnt_ref[...] = run_ref[...]


def _router(x, w_router, bias, tm, cnt0):
    t = x.shape[0]
    tri = jnp.asarray(np.arange(tm)[:, None] < np.arange(tm)[None, :], BF16)
    kt = lambda dt: jax.ShapeDtypeStruct((TOP_K, t), dt)
    return pl.pallas_call(
        _router_kernel, grid=(t // tm,),
        in_specs=[pl.BlockSpec((tm, D_MODEL), lambda i: (i, 0)),
                  pl.BlockSpec((N_EXPERTS, D_MODEL), lambda i: (0, 0)),
                  pl.BlockSpec((N_EXPERTS, LANE), lambda i: (0, 0)),
                  pl.BlockSpec((tm, tm), lambda i: (0, 0)),
                  pl.BlockSpec((N_EXPERTS, LANE), lambda i: (0, 0))],
        out_specs=[pl.BlockSpec((TOP_K, tm), lambda i: (0, i)),
                   pl.BlockSpec((TOP_K, tm), lambda i: (0, i)),
                   pl.BlockSpec((N_EXPERTS, LANE), lambda i: (0, 0))],
        out_shape=[kt(I32), kt(F32), jax.ShapeDtypeStruct((N_EXPERTS, LANE), F32)],
        scratch_shapes=[pltpu.VMEM((N_EXPERTS, LANE), F32)],
        compiler_params=_cparams("arbitrary"), name="moe_router")(
            x, w_router.T.astype(BF16), jnp.broadcast_to(bias[:, None], (N_EXPERTS, LANE)), tri, cnt0)


def _slots_kernel(start_ref, code_ref, pos_ref):
    code = code_ref[...]
    e = lax.shift_right_logical(code, RANK_BITS)
    r = code & RANK_MASK
    pos = r
    for x in range(N_EXPERTS):
        pos = jnp.where(e == x, r + start_ref[x], pos)
    pos_ref[...] = pos * SUBLANES


def _slots(code, slot_start):
    k, t = code.shape
    tile = min(t, 4096)
    gs = pltpu.PrefetchScalarGridSpec(
        num_scalar_prefetch=1, grid=(t // tile,),
        in_specs=[pl.BlockSpec((k, tile), lambda i, *_: (0, i))],
        out_specs=pl.BlockSpec((k, tile), lambda i, *_: (0, i)))
    return pl.pallas_call(_slots_kernel, grid_spec=gs, out_shape=jax.ShapeDtypeStruct((k, t), I32),
                          compiler_params=_cparams("parallel"), name="moe_slots")(slot_start, code)


def _row_tile(ref, row0, lead=()):
    return ref.at[lead + (pl.ds(pl.multiple_of(row0, SUBLANES), SUBLANES),)]


def _dispatch_kernel(fill_ref, pos_ref, x_ref, xb_ref, wsg_ref, wsu_ref, wsd_ref, *rest, eps, fill_sizes):
    xs_ref, sh_ref, zero_ref, sem, fill_sem = rest[-5:]
    step = pl.program_id(0)
    tm = x_ref.shape[0] // SUBLANES
    zero_ref[...] = jnp.zeros_like(zero_ref)

    def copy(r, k):
        return pltpu.make_async_copy(_row_tile(x_ref, r * SUBLANES), _row_tile(xs_ref, pos_ref[0, 0, k * tm + r]), sem)

    def start(r, carry):
        for k in range(TOP_K):
            copy(r, k).start(priority=k % 2)
        return carry

    def wait(r, carry):
        for k in range(TOP_K):
            copy(r, k).wait()
        return carry

    def fill_expert(e, carry):
        base = fill_ref[0, e]
        n = fill_ref[1, e]
        def zero_copy(p):
            first = base + (n & ~(2 * p - 1))
            return pltpu.make_async_copy(
                zero_ref.at[pl.ds(0, p * SUBLANES)],
                xs_ref.at[pl.ds(pl.multiple_of(first * SUBLANES, SUBLANES), p * SUBLANES)], fill_sem)

        for p in fill_sizes:
            pl.when((n & p) != 0)(zero_copy(p).start)
        for p in fill_sizes:
            pl.when((n & p) != 0)(zero_copy(p).wait)
        return carry

    lax.fori_loop(0, tm, start, 0)
    e0 = jnp.minimum(step * eps, N_EXPERTS)
    e1 = jnp.minimum(e0 + eps, N_EXPERTS)
    lax.fori_loop(e0, e1, fill_expert, 0)
    x = xb_ref[...]
    hg = jnp.dot(x, wsg_ref[...], preferred_element_type=F32)
    hu = jnp.dot(x, wsu_ref[...], preferred_element_type=F32)
    hb = (jax.nn.silu(hg) * hu).astype(BF16)
    sh_ref[...] = jnp.dot(hb, wsd_ref[...], preferred_element_type=F32).astype(BF16)
    lax.fori_loop(0, tm, wait, 0)


def _dispatch_shared(xp, xb, pos_tiles, fill, wsg, wsu, wsd, layer, nslot, tm, bm, into=None):
    t = xb.shape[0]
    steps = t // tm
    de = wsg.shape[-1]
    eps = -(-N_EXPERTS // steps)
    fill_sizes = tuple(1 << i for i in reversed(range((bm - 1).bit_length())))
    in_specs = [pl.BlockSpec((1, 1, TOP_K * tm), lambda i, *_: (i, 0, 0), memory_space=pltpu.SMEM),
                pl.BlockSpec((tm * SUBLANES, LANE), lambda i, *_: (i, 0)),
                pl.BlockSpec((tm, D_MODEL), lambda i, *_: (i, 0)),
                pl.BlockSpec((None, D_MODEL, de), lambda i, *_: (layer, 0, 0)),
                pl.BlockSpec((None, D_MODEL, de), lambda i, *_: (layer, 0, 0)),
                pl.BlockSpec((None, de, D_MODEL), lambda i, *_: (layer, 0, 0))]
    args = [fill, pos_tiles, xp, xb, wsg, wsu, wsd]
    if into is not None:
        in_specs.append(pl.BlockSpec(memory_space=pl.ANY))
        args.append(into)
    gs = pltpu.PrefetchScalarGridSpec(
        num_scalar_prefetch=1, grid=(steps,), in_specs=in_specs,
        out_specs=[pl.BlockSpec(memory_space=pl.ANY), pl.BlockSpec((tm, D_MODEL), lambda i, *_: (i, 0))],
        scratch_shapes=[pltpu.VMEM((fill_sizes[0] * SUBLANES, LANE), xp.dtype),
                        pltpu.SemaphoreType.DMA(()), pltpu.SemaphoreType.DMA(())])
    return pl.pallas_call(
        functools.partial(_dispatch_kernel, eps=eps, fill_sizes=fill_sizes), grid_spec=gs,
        out_shape=[jax.ShapeDtypeStruct((nslot * SUBLANES, LANE), xp.dtype),
                   jax.ShapeDtypeStruct((t, D_MODEL), BF16)],
        input_output_aliases={} if into is None else {len(args) - 1: 0},
        compiler_params=_cparams("arbitrary"), name="moe_dispatch_shared")(*args)


def _expert_kernel(be_ref, nu_ref, first_ref, nxt_ref, slot_ref, x_ref, wg_hbm, wu_hbm, wd_hbm, o_ref,
                   wg_buf, wu_buf, wd_buf, sems, *, layer):
    b = pl.program_id(0)
    used = b < nu_ref[0]
    slot = slot_ref[b]

    def fetch(e, s):
        return (pltpu.make_async_copy(wg_hbm.at[layer, e], wg_buf.at[s], sems.at[s, 0]),
                pltpu.make_async_copy(wu_hbm.at[layer, e], wu_buf.at[s], sems.at[s, 1]),
                pltpu.make_async_copy(wd_hbm.at[layer, e], wd_buf.at[s], sems.at[s, 2]))

    @pl.when(b == 0)
    def _():
        for cp in fetch(be_ref[0], 0):
            cp.start()

    @pl.when(jnp.logical_and(used, first_ref[b] == 1))
    def _():
        for cp in fetch(be_ref[b], slot):
            cp.wait()

        @pl.when(nxt_ref[b] >= 0)
        def _():
            for cp in fetch(nxt_ref[b], 1 - slot):
                cp.start()

    @pl.when(used)
    def _():
        lo, hi = _unpack_halves(_load_row_tiles(x_ref, (), x_ref.shape[0] // SUBLANES))
        lo = lo.astype(BF16)
        hi = hi.astype(BF16)
        w = lambda buf, r0, r1: buf[slot, r0:r1, :].astype(BF16)
        hg = (jnp.dot(lo, w(wg_buf, 0, HALF_W), preferred_element_type=F32)
              + jnp.dot(hi, w(wg_buf, HALF_W, D_MODEL), preferred_element_type=F32))
        hu = (jnp.dot(lo, w(wu_buf, 0, HALF_W), preferred_element_type=F32)
              + jnp.dot(hi, w(wu_buf, HALF_W, D_MODEL), preferred_element_type=F32))
        hb = (jax.nn.silu(hg) * hu).astype(BF16)
        out = jnp.dot(hb, w(wd_buf, 0, wd_buf.shape[1]), preferred_element_type=F32)
        _store_row_tiles(o_ref, (), _pack_halves(out))

    @pl.when(jnp.logical_not(used))
    def _():
        o_ref[...] = jnp.zeros_like(o_ref)


def _expert_ffn(xs, blk_e, nused, wg, wu, wd, layer, bm):
    nrows, width = xs.shape
    brows = bm * SUBLANES
    nb = nrows // brows
    de = wg.shape[-1]
    idx = jnp.arange(nb, dtype=I32)
    first = jnp.concatenate([jnp.ones((1,), bool), blk_e[1:] != blk_e[:-1]]) & (idx < nused[0])
    slot = lax.rem(jnp.cumsum(first.astype(I32)) - 1, 2)
    nxt_first = lax.cummin(jnp.where(first, idx, nb), axis=0, reverse=True)
    nxt_after = jnp.concatenate([nxt_first[1:], jnp.full((1,), nb, I32)])
    nxt_e = jnp.where(nxt_after < nb, blk_e[jnp.minimum(nxt_after, nb - 1)], -1)
    hbm = pl.BlockSpec(memory_space=pl.ANY)
    gs = pltpu.PrefetchScalarGridSpec(
        num_scalar_prefetch=5, grid=(nb,),
        in_specs=[pl.BlockSpec((brows, width), lambda b, be, nu, *_: (jnp.minimum(b, nu[0] - 1), 0)),
                  hbm, hbm, hbm],
        out_specs=pl.BlockSpec((brows, width), lambda b, *_: (b, 0)),
        scratch_shapes=[pltpu.VMEM((2, D_MODEL, de), wg.dtype), pltpu.VMEM((2, D_MODEL, de), wu.dtype),
                        pltpu.VMEM((2, de, D_MODEL), wd.dtype), pltpu.SemaphoreType.DMA((2, 3))])
    return pl.pallas_call(
        functools.partial(_expert_kernel, layer=layer), grid_spec=gs,
        out_shape=jax.ShapeDtypeStruct((nrows, width), xs.dtype),
        compiler_params=_cparams("arbitrary"), name="moe_expert_ffn")(
            blk_e, nused, first.astype(I32), nxt_e.astype(I32), slot.astype(I32), xs, wg, wu, wd)


def _combine_kernel(pos_ref, nxt_ref, gw_ref, x_ref, sh_ref, eo_ref, g_ref, b_ref, o_ref, ob_ref, op_ref,
                    buf, sems):
    step = pl.program_id(0)
    nsteps = pl.num_programs(0)
    tm = x_ref.shape[0]
    cur = lax.rem(step, 2)

    def copy(idx_ref, slot, r, k):
        return pltpu.make_async_copy(_row_tile(eo_ref, idx_ref[0, 0, k * tm + r]),
                                     _row_tile(buf, r * SUBLANES, (slot, k)), sems.at[slot])

    def gather(idx_ref, slot):
        def start(r, carry):
            for k in range(TOP_K):
                copy(idx_ref, slot, r, k).start(priority=k % 2)
            return carry
        lax.fori_loop(0, tm, start, 0)

    @pl.when(step == 0)
    def _():
        gather(pos_ref, 0)

    @pl.when(step + 1 < nsteps)
    def _():
        gather(nxt_ref, 1 - cur)

    def wait(r, carry):
        for k in range(TOP_K):
            copy(pos_ref, cur, r, k).wait()
        return carry

    lax.fori_loop(0, tm, wait, 0)
    sh = sh_ref[...].astype(F32)
    lo_acc = sh[:, :HALF_W]
    hi_acc = sh[:, HALF_W:]
    for k in range(TOP_K):
        lo, hi = _unpack_halves(_load_row_tiles(buf, (cur, k), tm))
        wk = gw_ref[:, k:k + 1]
        lo_acc = lo_acc + lo * wk
        hi_acc = hi_acc + hi * wk
    ffn = jnp.concatenate([lo_acc, hi_acc], 1)
    _emit_x(_layer_norm(ALPHA * x_ref[...] + ffn, g_ref[...], b_ref[...]), o_ref, ob_ref, op_ref)


def _combine_ln(pos_tiles, gw_t, x, sh, eo, g, b, tm):
    t = x.shape[0]
    steps = t // tm
    row = lambda width: pl.BlockSpec((tm, width), lambda i: (i, 0))
    const = pl.BlockSpec((1, D_MODEL), lambda i: (0, 0))
    idx = lambda f: pl.BlockSpec((1, 1, TOP_K * tm), f, memory_space=pltpu.SMEM)
    out_specs, out_shape = _x_out(t, tm)
    return pl.pallas_call(
        _combine_kernel, grid=(steps,),
        in_specs=[idx(lambda i: (i, 0, 0)), idx(lambda i: (jnp.minimum(i + 1, steps - 1), 0, 0)),
                  row(TOP_K), row(D_MODEL), row(D_MODEL), pl.BlockSpec(memory_space=pl.ANY), const, const],
        out_specs=out_specs, out_shape=out_shape,
        scratch_shapes=[pltpu.VMEM((2, TOP_K, tm * SUBLANES, LANE), U32), pltpu.SemaphoreType.DMA((2,))],
        compiler_params=_cparams("arbitrary"), name="moe_combine_ln")(
            pos_tiles, pos_tiles, gw_t, x, sh, eo, g.reshape(1, D_MODEL), b.reshape(1, D_MODEL))


def _moe_ln(groups, w_router, bias, wg, wu, wd, wsg, wsu, wsd, layer, g, b, bm):
    cnt = jnp.zeros((N_EXPERTS, LANE), F32)
    routed = []
    for x, xb, xp, tm in groups:
        code, gw, cnt = _router(xb, w_router, bias, tm, cnt)
        routed.append((code, gw))
    t_all = sum(x.shape[0] for x, _, _, _ in groups)
    assert t_all * TOP_K < (1 << RANK_BITS)
    counts = cnt[:, 0].astype(I32)
    padded = (counts + bm - 1) // bm * bm
    pad_end = jnp.cumsum(padded)
    slot_start = pad_end - padded
    fill = jnp.stack([slot_start + counts, padded - counts])
    nb = -(-(t_all * TOP_K + N_EXPERTS * (bm - 1)) // bm)
    blk_first = jnp.arange(nb, dtype=I32) * bm
    blk_e = jnp.minimum(jnp.sum((pad_end[None, :] <= blk_first[:, None]).astype(I32), 1), N_EXPERTS - 1)
    nused = (pad_end[-1] // bm).astype(I32).reshape(1)
    xs = None
    pos_tiles, shared = [], []
    for (x, xb, xp, tm), (code, _) in zip(groups, routed):
        nt = x.shape[0] // tm
        pos = _slots(code, slot_start)
        pos_tiles.append(pos.reshape(TOP_K, nt, tm).transpose(1, 0, 2).reshape(nt, 1, TOP_K * tm))
        xs, sh = _dispatch_shared(xp, xb, pos_tiles[-1], fill if xs is None else jnp.zeros_like(fill),
                                  wsg, wsu, wsd, layer, nb * bm, tm, bm, xs)
        shared.append(sh)
    eo = _expert_ffn(xs, blk_e, nused, wg, wu, wd, layer, bm)
    return [_combine_ln(pt, gw.T, x, sh, eo, g, b, tm)
            for (x, _, _, tm), (_, gw), pt, sh in zip(groups, routed, pos_tiles, shared)]


def _odd_w_in(w):
    sizes = (H_C * K_C, H_C * K_C, H_C * V_C, GLA_RANK, H_C * V_C, HALF_W, CONV_DIM, H_D)
    q, k, v, lr, r, z, xbc, dt = jnp.split(w, [int(s) for s in np.cumsum(sizes)[:-1]], axis=-1)
    pad = jnp.zeros((w.shape[0], OD_XBC - (OD_LRDT + GLA_RANK + H_D)), w.dtype)
    return jnp.concatenate([q, k, v, r, z, lr, dt, pad, xbc], -1)


def _mix_layer(grp, l, p):
    cfg, seq_len, bsz = grp["cfg"], grp["seq_len"], grp["bsz"]
    x, xb, outs = grp["x"], grp["xb"], grp["outs"]
    j = l // 2
    if l % 2 == 0:
        proj = _proj(x if xb is None else xb, p["ev_w_in"][j], cfg["ptm"], cfg["ptn"])
        ya, v_rows = _mixer_a(proj, p["ev_a_ln_g"][j], p["ev_a_ln_b"][j], p["ev_a_ws"][j],
                              p["ev_a_bs"][j], seq_len, cfg["arows"], cfg["emit_v"])
        yb, s_b = _mixer_b(proj, p["hgrn_lb_logits"], p["ev_b_norm_g"][j], grp["hgrn0"][j], seq_len,
                           cfg["rows"], l)
        outs["v_rows"] = v_rows
        outs["hgrn"] = s_b
        w_out = p["ev_w_out"][j]
    else:
        proj = _proj(x if xb is None else xb, p["od_w_in"][j], cfg["ptm"], cfg["ptn"])
        ya, s_c = _mixer_c(proj, p["od_c_gate_w2"][j], p["od_c_gate_b"][j], p["od_c_norm_g"][j],
                           grp["gla0"][j], seq_len, cfg["rows"])
        yb, s_d = _mixer_d(proj, p["od_d_conv_w"][j], p["od_d_conv_b"][j], p["od_d_dt_bias"][j],
                           p["od_d_a_log"][j], p["od_d_skip"][j], p["od_d_norm_g"][j],
                           grp["ssm0"][j], grp["conv0"][j], seq_len, cfg["rows"])
        keep = min(D_CONV - 1, seq_len)
        xbc = proj.reshape(bsz, seq_len, OD_PAD)[:, seq_len - keep:, OD_XBC:OD_XBC + CONV_DIM]
        outs["conv"] = jnp.concatenate([grp["conv0"][j], xbc], 1)[:, -(D_CONV - 1):]
        outs["gla"] = s_c
        outs["ssm"] = s_d
        w_out = p["od_w_out"][j]
    return _out_proj_ln(ya, yb, x, w_out, p["ln1_g"][l], p["ln1_b"][l], cfg["tm"])


def _group(x3, hgrn0, gla0, ssm0, conv0):
    bsz, seq_len, _ = x3.shape
    return dict(x=x3.reshape(bsz * seq_len, D_MODEL), xb=None, bsz=bsz, seq_len=seq_len, outs={},
                cfg=_config(bsz, seq_len), hgrn0=hgrn0, gla0=gla0, ssm0=ssm0, conv0=conv0)


def _config(bsz, seq_len):
    t = bsz * seq_len
    if seq_len % CHUNK == 0:
        return dict(tm=256, rows=min(256, seq_len), arows=min(256, seq_len),
                    ptm=min(1024, t), ptn=1024, emit_v=False)
    return dict(tm=t, rows=seq_len, arows=t, ptm=t, ptn=512, emit_v=True)


MOE_BLOCK_ROWS = 512


def kernel(x_prompt, x_sample, state_b_hgrn, state_c_gla, state_d_ssm, state_d_conv, ev_w_in, ev_a_ln_g, ev_a_ln_b, ev_a_ws, ev_a_bs, ev_b_norm_g, ev_w_out, hgrn_lb_logits, od_w_in, od_c_gate_w2, od_c_gate_b, od_c_norm_g, od_d_conv_w, od_d_conv_b, od_d_dt_bias, od_d_a_log, od_d_skip, od_d_norm_g, od_w_out, ln1_g, ln1_b, ln2_g, ln2_b, moe_w_router, moe_router_bias, moe_w_gate, moe_w_up, moe_w_down, moe_ws_gate, moe_ws_up, moe_ws_down):
    p = dict(
        ev_w_in=ev_w_in.astype(BF16), ev_a_ln_g=ev_a_ln_g, ev_a_ln_b=ev_a_ln_b, ev_a_ws=ev_a_ws,
        ev_a_bs=ev_a_bs, ev_b_norm_g=ev_b_norm_g, ev_w_out=ev_w_out.astype(BF16),
        hgrn_lb_logits=hgrn_lb_logits,
        od_w_in=jnp.stack([_odd_w_in(od_w_in[j]) for j in range(od_w_in.shape[0])]).astype(BF16),
        od_c_gate_w2=od_c_gate_w2, od_c_gate_b=od_c_gate_b, od_c_norm_g=od_c_norm_g,
        od_d_conv_w=od_d_conv_w, od_d_conv_b=od_d_conv_b, od_d_dt_bias=od_d_dt_bias,
        od_d_a_log=od_d_a_log, od_d_skip=od_d_skip, od_d_norm_g=od_d_norm_g,
        od_w_out=od_w_out.astype(BF16), ln1_g=ln1_g, ln1_b=ln1_b, ln2_g=ln2_g, ln2_b=ln2_b,
        moe_w_router=moe_w_router, moe_router_bias=moe_router_bias,
        moe_w_gate=moe_w_gate, moe_w_up=moe_w_up, moe_w_down=moe_w_down,
        moe_ws_gate=moe_ws_gate.astype(BF16),
        moe_ws_up=moe_ws_up.astype(BF16), moe_ws_down=moe_ws_down.astype(BF16))
    bp, lp, _ = x_prompt.shape
    n_even = state_b_hgrn.shape[0]
    n_odd = state_c_gla.shape[0]
    zeros = lambda n, *s: jnp.zeros((n, bp) + s, F32)
    bs, ls, _ = x_sample.shape
    groups = [_group(x_prompt, zeros(n_even, H_B, K_B, V_B), zeros(n_odd, H_C, K_C, V_C),
                     zeros(n_odd, H_D, N_D, P_D), zeros(n_odd, D_CONV - 1, CONV_DIM)),
              _group(x_sample, state_b_hgrn, state_c_gla, state_d_ssm, state_d_conv)]
    for l in range(DEPTH):
        mixed = [_mix_layer(grp, l, p) for grp in groups]
        res = _moe_ln([(x, xb, xp, grp["cfg"]["tm"]) for (x, xb, xp), grp in zip(mixed, groups)],
                      p["moe_w_router"][l], p["moe_router_bias"][l], p["moe_w_gate"], p["moe_w_up"],
                      p["moe_w_down"], p["moe_ws_gate"], p["moe_ws_up"], p["moe_ws_down"], l,
                      p["ln2_g"][l], p["ln2_b"][l], MOE_BLOCK_ROWS)
        for grp, (x, xb, _) in zip(groups, res):
            grp["x"], grp["xb"] = x, xb
    y_p = groups[0]["x"].reshape(bp, lp, D_MODEL)
    y_s = groups[1]["x"].reshape(bs, ls, D_MODEL)
    o_p, o_s = groups[0]["outs"], groups[1]["outs"]
    a_v = o_s["v_rows"].reshape(1, bs, ls, HALF_W)
    return (y_p, y_s, a_v, o_p["hgrn"][None], o_s["hgrn"][None], o_p["gla"][None], o_s["gla"][None],
            o_p["ssm"][None], o_s["ssm"][None], o_p["conv"][None], o_s["conv"][None])
```

```python
import functools
import math

import jax
import jax.numpy as jnp
import numpy as np
from jax import lax
from jax.experimental import pallas as pl
from jax.experimental.pallas import tpu as pltpu

F32 = jnp.float32
BF16 = jnp.bfloat16
I32 = jnp.int32

D_MODEL = 2048
DEPTH = 2
CHUNK = 64
SUB = 16
HALF_W = D_MODEL // 2
A_CHUNK = 128
H_A = 4
DA = HALF_W // H_A
H_B = 8
K_B = 128
V_B = HALF_W // H_B
H_C = 4
V_C = HALF_W // H_C
K_C = V_C // 2
GLA_RANK = 16
GLA_TAU = 16.0
P_D = 64
H_D = HALF_W // P_D
G_D = 2
HPG_D = H_D // G_D
N_D = 128
D_CONV = 4
CONV_DIM = HALF_W + 2 * G_D * N_D
N_EXPERTS = 64
N_GROUPS = 8
GROUP_SIZE = N_EXPERTS // N_GROUPS
TOPK_GROUPS = 4
TOP_K = 8
D_EXPERT = 512
ROUTE_SCALE = 2.5
ALPHA = (2 * DEPTH) ** 0.25
EPS = 1e-5
LANE = 128
VMEM_LIMIT = 56 * 1024 * 1024

OD_Q, OD_K, OD_V, OD_R, OD_Z, OD_LRDT, OD_XBC = 0, 512, 1024, 2048, 3072, 4096, 4608
OD_PAD = 6144
DT_OFF = GLA_RANK


def _cparams(*sem):
    return pltpu.CompilerParams(dimension_semantics=sem, vmem_limit_bytes=VMEM_LIMIT)


def _split3(x):
    hi = x.astype(BF16)
    r = x - hi.astype(F32)
    mid = r.astype(BF16)
    lo = (r - mid.astype(F32)).astype(BF16)
    return hi, mid, lo


def _split2(x):
    hi = x.astype(BF16)
    return hi, (x - hi.astype(F32)).astype(BF16)


def _dot_exact_l(a_bf16, x):
    return sum(jnp.dot(a_bf16, p, preferred_element_type=F32) for p in _split3(x))


def _dot_exact_r(x, b_bf16):
    return sum(jnp.dot(p, b_bf16, preferred_element_type=F32) for p in _split3(x))


def _dot_nt(a, b):
    return lax.dot_general(a, b, (((1,), (1,)), ((), ())), preferred_element_type=F32)


def _softplus(x):
    return jnp.maximum(x, 0.0) + jnp.log1p(jnp.exp(-jnp.abs(x)))


def _proj_kernel(x_ref, w_ref, o_ref, *scratch):
    if scratch:
        xb_ref, = scratch

        @pl.when(pl.program_id(1) == 0)
        def _():
            xb_ref[...] = x_ref[...].astype(BF16)
        x = xb_ref[...]
    else:
        x = x_ref[...]
    o_ref[...] = jnp.dot(x, w_ref[...], preferred_element_type=F32)


def _proj(x, w, tm, tn):
    t, k = x.shape
    n = w.shape[1]
    scratch = [] if x.dtype == BF16 else [pltpu.VMEM((tm, k), BF16)]
    return pl.pallas_call(
        _proj_kernel, grid=(t // tm, n // tn),
        in_specs=[pl.BlockSpec((tm, k), lambda i, j: (i, 0)),
                  pl.BlockSpec((k, tn), lambda i, j: (0, j))],
        out_specs=pl.BlockSpec((tm, tn), lambda i, j: (i, j)),
        out_shape=jax.ShapeDtypeStruct((t, n), F32),
        scratch_shapes=scratch, compiler_params=_cparams("parallel", "arbitrary"),
        name="in_proj")(x, w)


def _mixa_kernel(u_ref, v_ref, lng_ref, lnb_ref, w_ref, bs_ref, ya_ref, *vrows, ca):
    gu = jax.nn.gelu(u_ref[...])
    gv = jax.nn.gelu(v_ref[...])
    rows = gu.shape[0]
    for h in range(H_A):
        sl = slice(h * DA, (h + 1) * DA)
        vh = gv[:, sl]
        mu = jnp.mean(vh, -1, keepdims=True)
        d = vh - mu
        var = jnp.mean(d * d, -1, keepdims=True)
        vn = d * lax.rsqrt(var + EPS) * lng_ref[h] + lnb_ref[h]
        if vrows:
            vrows[0][:, sl] = vn
        vnb = vn.astype(BF16)
        for c in range(rows // ca):
            rs = slice(c * ca, (c + 1) * ca)
            s = jnp.dot(w_ref[h], vnb[rs], preferred_element_type=F32) + bs_ref[h]
            ya_ref[rs, sl] = (gu[rs, sl] * s).astype(BF16)


def _mixer_a(proj, lng, lnb, ws, bs, seq_len, rows, emit_v):
    t = proj.shape[0]
    ca = min(A_CHUNK, seq_len)
    pos = np.arange(A_CHUNK)
    mask = (pos[None, :] // CHUNK) <= (pos[:, None] // CHUNK)
    w = (ws * mask)[:, :ca, :ca].astype(BF16)
    bsb = jnp.broadcast_to(bs[:, :ca, None], (H_A, ca, DA)).astype(F32)
    nb = HALF_W // HALF_W
    out_shape = [jax.ShapeDtypeStruct((t, HALF_W), BF16)]
    out_specs = [pl.BlockSpec((rows, HALF_W), lambda i: (i, 0))]
    if emit_v:
        out_shape.append(jax.ShapeDtypeStruct((t, HALF_W), F32))
        out_specs.append(pl.BlockSpec((rows, HALF_W), lambda i: (i, 0)))
    del nb
    res = pl.pallas_call(
        functools.partial(_mixa_kernel, ca=ca), grid=(t // rows,),
        in_specs=[pl.BlockSpec((rows, HALF_W), lambda i: (i, 0)),
                  pl.BlockSpec((rows, HALF_W), lambda i: (i, 1)),
                  pl.BlockSpec((H_A, 1, DA), lambda i: (0, 0, 0)),
                  pl.BlockSpec((H_A, 1, DA), lambda i: (0, 0, 0)),
                  pl.BlockSpec((H_A, ca, ca), lambda i: (0, 0, 0)),
                  pl.BlockSpec((H_A, ca, DA), lambda i: (0, 0, 0))],
        out_specs=out_specs, out_shape=out_shape,
        compiler_params=_cparams("parallel"), name="mixer_a")(
            proj, proj, lng[:, None, :], lnb[:, None, :], w, bsb)
    return res if emit_v else (res[0], None)


def _scan_mats(c):
    i = np.arange(c)[:, None]
    j = np.arange(c)[None, :]
    tril = (j <= i)
    local = tril & ((i // SUB) == (j // SUB))
    ones = np.ones((c, c), bool)
    return jnp.asarray(np.concatenate([local, tril, ones], 0), BF16)


def _cat(parts, axis):
    return parts[0] if len(parts) == 1 else jnp.concatenate(parts, axis)


def _gla_intra(q, k, v, sc, c, nh, kd, vd):
    lc, cum, last = sc[0:c], sc[c:2 * c], sc[2 * c:3 * c]
    ns = c // SUB
    q_hi, q_lo = _split2(q * jnp.exp(lc))
    pre = cum - lc
    vb = v.astype(BF16)
    att = []
    for blk in range(ns):
        n = SUB * (blk + 1)
        r0 = SUB * blk
        k_hi, k_lo = _split2(k[:n] * jnp.exp(pre[r0:r0 + 1] - cum[:n]))
        rq = slice(r0, r0 + SUB)
        for h in range(nh):
            ks = slice(h * kd, (h + 1) * kd)
            att.append(_dot_nt(q_hi[rq, ks], k_hi[:, ks])
                       + (_dot_nt(q_hi[rq, ks], k_lo[:, ks]) + _dot_nt(q_lo[rq, ks], k_hi[:, ks])))
    attb = []
    for blk in range(ns):
        n = SUB * (blk + 1)
        keep = (lax.broadcasted_iota(I32, (SUB, n), 1) - SUB * blk) <= lax.broadcasted_iota(I32, (SUB, n), 0)
        attb += [jnp.where(keep, att[blk * nh + h], 0.0).astype(BF16) for h in range(nh)]
    rows_out = []
    for blk in range(ns):
        n = SUB * (blk + 1)
        rows_out.append(_cat([jnp.dot(attb[blk * nh + h], vb[:n, h * vd:(h + 1) * vd],
                                      preferred_element_type=F32) for h in range(nh)], 1))
    qc = (q * jnp.exp(cum)).astype(BF16)
    kc = (k * jnp.exp(last - cum)).astype(BF16)
    return _cat(rows_out, 0), qc, kc, v.T.astype(BF16), jnp.exp(last[0:1])


def _gla_state(intra, s_ref, nh, kd, vd):
    o_intra, qc, kc, vt, e_last = intra
    sts = [s_ref[h] for h in range(nh)]
    inter = [_dot_nt(qc[:, h * kd:(h + 1) * kd], sts[h].astype(BF16)) for h in range(nh)]
    upd = [jnp.dot(vt[h * vd:(h + 1) * vd], kc[:, h * kd:(h + 1) * kd], preferred_element_type=F32)
           for h in range(nh)]
    for h in range(nh):
        s_ref[h] = sts[h] * e_last[:, h * kd:(h + 1) * kd] + upd[h]
    return o_intra + _cat(inter, 1)


def _hgrn_kernel(q_ref, f_ref, i_ref, g_ref, lbl_ref, ng_ref, s0_ref, cm_ref, y_ref, so_ref, s_ref,
                 *, c, layer):
    step = pl.program_id(1)

    @pl.when(step == 0)
    def _():
        s_ref[...] = s0_ref[0]

    lg = lbl_ref[...]
    ex = jnp.exp(lg - jnp.max(lg, 0, keepdims=True))
    sm = ex / jnp.sum(ex, 0, keepdims=True)
    lb = jnp.sum(sm[:layer + 1], 0, keepdims=True)
    rows = q_ref.shape[0]
    intra = []
    for ci in range(rows // c):
        rs = slice(ci * c, (ci + 1) * c)
        f = lb + (1.0 - lb) * jax.nn.sigmoid(f_ref[rs])
        g = jnp.log(f)
        sc = _dot_exact_l(cm_ref[...], g)
        intra.append(_gla_intra(jax.nn.silu(q_ref[rs]), 1.0 - f, i_ref[rs], sc, c, H_B, K_B, V_B))
    for ci in range(rows // c):
        rs = slice(ci * c, (ci + 1) * c)
        o = _gla_state(intra[ci], s_ref, H_B, K_B, V_B)
        rn = _cat([jnp.broadcast_to(lax.rsqrt(jnp.mean(jnp.square(o[:, h * V_B:(h + 1) * V_B]), -1,
                                                        keepdims=True) + EPS), (c, V_B))
                   for h in range(H_B)], 1)
        y_ref[rs, :] = (o * rn * ng_ref[...] * jax.nn.sigmoid(g_ref[rs])).astype(BF16)

    @pl.when(step == pl.num_programs(1) - 1)
    def _():
        so_ref[0] = s_ref[...]


def _mixer_b(proj, lb_logits, norm_g, s0, seq_len, rows, layer):
    t = proj.shape[0]
    bsz = t // seq_len
    c = CHUNK if seq_len % CHUNK == 0 else seq_len
    spb = seq_len // rows
    s0t = jnp.swapaxes(s0, -1, -2)
    col = lambda j: pl.BlockSpec((rows, HALF_W), lambda b, s, j=j: (b * spb + s, j))
    nl = lb_logits.shape[0]
    y, st = pl.pallas_call(
        functools.partial(_hgrn_kernel, c=c, layer=layer), grid=(bsz, spb),
        in_specs=[col(2), col(3), col(4), col(5),
                  pl.BlockSpec((nl, HALF_W), lambda b, s: (0, 0)),
                  pl.BlockSpec((1, HALF_W), lambda b, s: (0, 0)),
                  pl.BlockSpec((1, H_B, V_B, K_B), lambda b, s: (b, 0, 0, 0)),
                  pl.BlockSpec((3 * c, c), lambda b, s: (0, 0))],
        out_specs=[pl.BlockSpec((rows, HALF_W), lambda b, s: (b * spb + s, 0)),
                   pl.BlockSpec((1, H_B, V_B, K_B), lambda b, s: (b, 0, 0, 0))],
        out_shape=[jax.ShapeDtypeStruct((t, HALF_W), BF16),
                   jax.ShapeDtypeStruct((bsz, H_B, V_B, K_B), F32)],
        scratch_shapes=[pltpu.VMEM((H_B, V_B, K_B), F32)],
        compiler_params=_cparams("parallel", "arbitrary"), name="mixer_b_hgrn")(
            proj, proj, proj, proj, lb_logits, norm_g.reshape(1, HALF_W), s0t, _scan_mats(c))
    return y, jnp.swapaxes(st, -1, -2)


def _glac_kernel(q_ref, k_ref, v_ref, r_ref, lrdt_ref, w2_ref, gb_ref, ng_ref, s0_ref, cm_ref,
                 y_ref, so_ref, s_ref, *, c):
    step = pl.program_id(1)

    @pl.when(step == 0)
    def _():
        s_ref[...] = s0_ref[0]

    rows = q_ref.shape[0]
    intra = []
    for ci in range(rows // c):
        rs = slice(ci * c, (ci + 1) * c)
        z = jnp.dot(lrdt_ref[rs], w2_ref[...], preferred_element_type=F32,
                    precision=lax.Precision.HIGHEST) + gb_ref[...]
        g = -_softplus(-z) / GLA_TAU
        sc = _dot_exact_l(cm_ref[...], g)
        intra.append(_gla_intra(q_ref[rs] * (K_C ** -0.5), k_ref[rs], v_ref[rs], sc, c, H_C, K_C, V_C))
    for ci in range(rows // c):
        rs = slice(ci * c, (ci + 1) * c)
        o = _gla_state(intra[ci], s_ref, H_C, K_C, V_C)
        rn = _cat([jnp.broadcast_to(lax.rsqrt(jnp.mean(jnp.square(o[:, h * V_C:(h + 1) * V_C]), -1,
                                                        keepdims=True) + EPS), (c, V_C))
                   for h in range(H_C)], 1)
        y_ref[rs, :] = (o * rn * ng_ref[...] * jax.nn.silu(r_ref[rs])).astype(BF16)

    @pl.when(step == pl.num_programs(1) - 1)
    def _():
        so_ref[0] = s_ref[...]


def _mixer_c(proj, gate_w2, gate_b, norm_g, s0, seq_len, rows):
    t = proj.shape[0]
    bsz = t // seq_len
    c = CHUNK if seq_len % CHUNK == 0 else seq_len
    spb = seq_len // rows
    s0t = jnp.swapaxes(s0, -1, -2)
    hk = H_C * K_C

    def col(width, off):
        return pl.BlockSpec((rows, width), lambda b, s: (b * spb + s, off // width))

    y, st = pl.pallas_call(
        functools.partial(_glac_kernel, c=c), grid=(bsz, spb),
        in_specs=[col(hk, OD_Q), col(hk, OD_K), col(HALF_W, OD_V), col(HALF_W, OD_R),
                  col(LANE, OD_LRDT),
                  pl.BlockSpec((LANE, hk), lambda b, s: (0, 0)),
                  pl.BlockSpec((1, hk), lambda b, s: (0, 0)),
                  pl.BlockSpec((1, HALF_W), lambda b, s: (0, 0)),
                  pl.BlockSpec((1, H_C, V_C, K_C), lambda b, s: (b, 0, 0, 0)),
                  pl.BlockSpec((3 * c, c), lambda b, s: (0, 0))],
        out_specs=[pl.BlockSpec((rows, HALF_W), lambda b, s: (b * spb + s, 0)),
                   pl.BlockSpec((1, H_C, V_C, K_C), lambda b, s: (b, 0, 0, 0))],
        out_shape=[jax.ShapeDtypeStruct((t, HALF_W), BF16),
                   jax.ShapeDtypeStruct((bsz, H_C, V_C, K_C), F32)],
        scratch_shapes=[pltpu.VMEM((H_C, V_C, K_C), F32)],
        compiler_params=_cparams("parallel", "arbitrary"), name="mixer_c_gla")(
            proj, proj, proj, proj, proj,
            jnp.pad(gate_w2, ((0, LANE - GLA_RANK), (0, 0))),
            gate_b.reshape(1, hk),
            norm_g.reshape(1, HALF_W), s0t, _scan_mats(c))
    return y, jnp.swapaxes(st, -1, -2)


GW = HPG_D * P_D


def _ssd_kernel(z_ref, xbc_ref, lrdt_ref, cw_ref, cb_ref, dtb_ref, alog_ref, skip_ref, ng_ref,
                ex_ref, eye_ref, cm_ref, conv0_ref, s0_ref, y_ref, so_ref, s_ref, tail_ref, *, c):
    step = pl.program_id(1)

    @pl.when(step == 0)
    def _():
        s_ref[...] = s0_ref[0]
        tail_ref[...] = conv0_ref[0]

    x = xbc_ref[...]
    rows = x.shape[0]
    tail = tail_ref[...]
    sub8 = lax.broadcasted_iota(I32, (8, CONV_DIM), 0)
    conv = x * cw_ref[D_CONV - 1:D_CONV]
    for sh in range(1, D_CONV):
        rolled = pltpu.roll(x, sh, 0)
        head = jnp.where(sub8 < sh, pltpu.roll(tail, sh, 0), rolled[0:8])
        xk = jnp.concatenate([head, rolled[8:]], 0) if rows > 8 else head
        conv = conv + xk * cw_ref[D_CONV - 1 - sh:D_CONV - sh]
    tail_ref[...] = x[rows - 8:rows]
    xc = jax.nn.silu(conv + cb_ref[...])
    a = -jnp.exp(alog_ref[...])
    ex = ex_ref[...]
    tri = lax.broadcasted_iota(I32, (c, c), 1) <= lax.broadcasted_iota(I32, (c, c), 0)
    group = lambda arr, g, w: arr[:, g * w:(g + 1) * w]
    intra = []
    for ci in range(rows // c):
        rs = slice(ci * c, (ci + 1) * c)
        xs = xc[rs, 0:HALF_W]
        bm = xc[rs, HALF_W:HALF_W + G_D * N_D]
        cmat = xc[rs, HALF_W + G_D * N_D:CONV_DIM].astype(BF16)
        dt = _softplus(lrdt_ref[rs] + dtb_ref[...])
        la = dt * a
        sc = _dot_exact_l(cm_ref[...], la)
        cum = sc[0:c]
        scx = _dot_exact_r(sc, ex)
        cumx = scx[0:c]
        lastx = scx[c:2 * c]
        xdt = xs * _dot_exact_r(dt, ex)
        cum_t = sum(_dot_nt(eye_ref[...], p) for p in _split3(cum))
        cbs = [_dot_nt(group(cmat, g, N_D), group(bm, g, N_D).astype(BF16)) for g in range(G_D)]
        xdtb = xdt.astype(BF16)
        wmats = []
        for h in range(H_D):
            hl = DT_OFF + h
            diff = cum[:, hl:hl + 1] - cum_t[hl:hl + 1, :]
            dec = jnp.where(tri, jnp.exp(jnp.minimum(diff, 0.0)), 0.0)
            wmats.append((cbs[h // HPG_D] * dec).astype(BF16))
        y_intra = _cat([jnp.dot(wmats[h], xdtb[:, h * P_D:(h + 1) * P_D], preferred_element_type=F32)
                        for h in range(H_D)], 1)
        intra.append((y_intra + xs * skip_ref[...], cmat, jnp.exp(cumx),
                      bm.T.astype(BF16), (xdt * jnp.exp(lastx - cumx)).astype(BF16), jnp.exp(lastx[0:1])))
    for ci in range(rows // c):
        rs = slice(ci * c, (ci + 1) * c)
        y_local, cmat, e_in, bm_t, xw, e_last = intra[ci]
        sgs = [s_ref[g] for g in range(G_D)]
        inter = [jnp.dot(group(cmat, g, N_D), sgs[g].astype(BF16), preferred_element_type=F32)
                 for g in range(G_D)]
        upd = [jnp.dot(bm_t[g * N_D:(g + 1) * N_D], group(xw, g, GW), preferred_element_type=F32)
               for g in range(G_D)]
        for g in range(G_D):
            s_ref[g] = sgs[g] * group(e_last, g, GW) + upd[g]
        yg = (y_local + _cat(inter, 1) * e_in) * jax.nn.silu(z_ref[rs])
        rn = _cat([jnp.broadcast_to(lax.rsqrt(jnp.mean(jnp.square(group(yg, g, GW)), -1, keepdims=True) + EPS),
                                    (c, GW)) for g in range(G_D)], 1)
        y_ref[rs, :] = (yg * rn * ng_ref[...]).astype(BF16)

    @pl.when(step == pl.num_programs(1) - 1)
    def _():
        so_ref[0] = s_ref[...]


def _mixer_d(proj, conv_w, conv_b, dt_bias, a_log, skip, norm_g, s0, conv0, seq_len, rows):
    t = proj.shape[0]
    bsz = t // seq_len
    c = CHUNK if seq_len % CHUNK == 0 else seq_len
    spb = seq_len // rows
    s0g = s0.reshape(bsz, G_D, HPG_D, N_D, P_D).transpose(0, 1, 3, 2, 4).reshape(bsz, G_D, N_D, GW)
    conv0p = jnp.pad(conv0, ((0, 0), (8 - (D_CONV - 1), 0), (0, 0)))
    expand_np = np.zeros((LANE, HALF_W), np.float32)
    expand_np[DT_OFF:DT_OFF + H_D] = np.repeat(np.eye(H_D), P_D, axis=1)
    expand = jnp.asarray(expand_np, BF16)
    eye = jnp.asarray(np.eye(LANE), BF16)
    i = np.arange(c)[:, None]
    j = np.arange(c)[None, :]
    cm = jnp.asarray(np.concatenate([j <= i, np.ones((c, c), bool)], 0), BF16)
    skipx = jnp.repeat(skip, P_D)[None, :]
    lane_pad = lambda v: jnp.pad(v.reshape(1, H_D), ((0, 0), (DT_OFF, LANE - DT_OFF - H_D)))

    def col(width, off):
        return pl.BlockSpec((rows, width), lambda b, s: (b * spb + s, off // width))

    def full(shape):
        return pl.BlockSpec(shape, lambda b, s: (0,) * len(shape))

    y, st = pl.pallas_call(
        functools.partial(_ssd_kernel, c=c), grid=(bsz, spb),
        in_specs=[col(HALF_W, OD_Z),
                  col(CONV_DIM, OD_XBC),
                  col(LANE, OD_LRDT),
                  full((D_CONV, CONV_DIM)), full((1, CONV_DIM)), full((1, LANE)), full((1, LANE)),
                  full((1, HALF_W)), full((1, HALF_W)), full((LANE, HALF_W)), full((LANE, LANE)),
                  full((2 * c, c)),
                  pl.BlockSpec((1, 8, CONV_DIM), lambda b, s: (b, 0, 0)),
                  pl.BlockSpec((1, G_D, N_D, GW), lambda b, s: (b, 0, 0, 0))],
        out_specs=[pl.BlockSpec((rows, HALF_W), lambda b, s: (b * spb + s, 0)),
                   pl.BlockSpec((1, G_D, N_D, GW), lambda b, s: (b, 0, 0, 0))],
        out_shape=[jax.ShapeDtypeStruct((t, HALF_W), BF16),
                   jax.ShapeDtypeStruct((bsz, G_D, N_D, GW), F32)],
        scratch_shapes=[pltpu.VMEM((G_D, N_D, GW), F32), pltpu.VMEM((8, CONV_DIM), F32)],
        compiler_params=_cparams("parallel", "arbitrary"), name="mixer_d_ssd")(
            proj, proj, proj, conv_w, conv_b.reshape(1, CONV_DIM), lane_pad(dt_bias),
            lane_pad(a_log), skipx, norm_g.reshape(1, HALF_W), expand, eye, cm, conv0p, s0g)
    st = st.reshape(bsz, G_D, N_D, HPG_D, P_D).transpose(0, 1, 3, 2, 4).reshape(bsz, H_D, N_D, P_D)
    return y, st


def _layer_norm(hpre, g, b):
    mu = jnp.mean(hpre, -1, keepdims=True)
    d = hpre - mu
    var = jnp.mean(d * d, -1, keepdims=True)
    return d * lax.rsqrt(var + EPS) * g + b


U32 = jnp.uint32
HI_MASK = 0xFFFF0000


def _pack_halves(x):
    half = x.shape[1] // 2
    lo = lax.bitcast_convert_type(x[:, :half].astype(BF16).astype(F32), U32) >> 16
    hi = lax.bitcast_convert_type(x[:, half:].astype(BF16).astype(F32), U32) & U32(HI_MASK)
    return lo | hi


def _unpack_halves(w):
    return (lax.bitcast_convert_type(w << 16, F32), lax.bitcast_convert_type(w & U32(HI_MASK), F32))


SUBLANES = 8
assert HALF_W == SUBLANES * LANE


def _store_row_tiles(ref, lead, w):
    m = w.shape[0]
    for s in range(SUBLANES):
        ref[lead + (pl.ds(s, m, stride=SUBLANES), slice(None))] = w[:, s * LANE:(s + 1) * LANE]


def _load_row_tiles(ref, lead, m):
    return jnp.concatenate([ref[lead + (pl.ds(s, m, stride=SUBLANES), slice(None))] for s in range(SUBLANES)], 1)


def _emit_x(xn, o_ref, ob_ref, op_ref):
    o_ref[...] = xn
    ob_ref[...] = xn.astype(BF16)
    _store_row_tiles(op_ref, (), _pack_halves(xn))


def _x_out(t, tm):
    row = lambda width: pl.BlockSpec((tm, width), lambda i, *_: (i, 0))
    return ([row(D_MODEL), row(D_MODEL), pl.BlockSpec((tm * SUBLANES, LANE), lambda i, *_: (i, 0))],
            [jax.ShapeDtypeStruct((t, D_MODEL), F32), jax.ShapeDtypeStruct((t, D_MODEL), BF16),
             jax.ShapeDtypeStruct((t * SUBLANES, LANE), U32)])


def _outproj_kernel(ya_ref, yb_ref, x_ref, w_ref, g_ref, b_ref, o_ref, ob_ref, op_ref):
    acc = jnp.dot(ya_ref[...], w_ref[0:HALF_W, :], preferred_element_type=F32)
    acc = acc + jnp.dot(yb_ref[...], w_ref[HALF_W:D_MODEL, :], preferred_element_type=F32)
    _emit_x(_layer_norm(ALPHA * x_ref[...] + acc, g_ref[...], b_ref[...]), o_ref, ob_ref, op_ref)


def _out_proj_ln(ya, yb, x, w, g, b, tm):
    t = x.shape[0]
    row = lambda width: pl.BlockSpec((tm, width), lambda i: (i, 0))
    out_specs, out_shape = _x_out(t, tm)
    return pl.pallas_call(
        _outproj_kernel, grid=(t // tm,),
        in_specs=[row(HALF_W), row(HALF_W), row(D_MODEL),
                  pl.BlockSpec((D_MODEL, D_MODEL), lambda i: (0, 0)),
                  pl.BlockSpec((1, D_MODEL), lambda i: (0, 0)),
                  pl.BlockSpec((1, D_MODEL), lambda i: (0, 0))],
        out_specs=out_specs, out_shape=out_shape,
        compiler_params=_cparams("parallel"), name="out_proj_ln")(
            ya, yb, x, w, g.reshape(1, D_MODEL), b.reshape(1, D_MODEL))


RANK_BITS = 20
RANK_MASK = (1 << RANK_BITS) - 1


def _router_kernel(x_ref, wr_ref, bias_ref, tri_ref, cnt0_ref, code_ref, gw_ref, cnt_ref, run_ref):
    step = pl.program_id(0)

    @pl.when(step == 0)
    def _():
        run_ref[...] = cnt0_ref[...]

    tm = x_ref.shape[0]
    logits = _dot_nt(wr_ref[...], x_ref[...])
    scores = jax.nn.sigmoid(logits)
    sel = scores + bias_ref[:, 0:1]
    neg = -jnp.inf
    sub = lax.broadcasted_iota(I32, (GROUP_SIZE, tm), 0).astype(F32)
    gsc = []
    for g in range(N_GROUPS):
        blk = sel[g * GROUP_SIZE:(g + 1) * GROUP_SIZE]
        m1 = jnp.max(blk, 0, keepdims=True)
        i1 = jnp.min(jnp.where(blk == m1, sub, float(GROUP_SIZE)), 0, keepdims=True)
        m2 = jnp.max(jnp.where(sub == i1, neg, blk), 0, keepdims=True)
        gsc.append(m1 + m2)
    cur = jnp.concatenate(gsc, 0)
    gio = lax.broadcasted_iota(I32, (N_GROUPS, tm), 0).astype(F32)
    gmask = jnp.zeros((N_GROUPS, tm), F32)
    for _ in range(TOPK_GROUPS):
        m = jnp.max(cur, 0, keepdims=True)
        i = jnp.min(jnp.where(cur == m, gio, float(N_GROUPS)), 0, keepdims=True)
        pick = gio == i
        gmask = jnp.where(pick, 1.0, gmask)
        cur = jnp.where(pick, neg, cur)
    emask = jnp.concatenate(
        [jnp.broadcast_to(gmask[g:g + 1], (GROUP_SIZE, tm)) for g in range(N_GROUPS)], 0)
    cur = jnp.where(emask > 0.5, sel, neg)
    eio = lax.broadcasted_iota(I32, (N_EXPERTS, tm), 0).astype(F32)
    member = jnp.zeros((N_EXPERTS, tm), F32)
    idxs, scs = [], []
    for _ in range(TOP_K):
        m = jnp.max(cur, 0, keepdims=True)
        i = jnp.min(jnp.where(cur == m, eio, float(N_EXPERTS)), 0, keepdims=True)
        pick = eio == i
        idxs.append(i)
        scs.append(jnp.sum(jnp.where(pick, scores, 0.0), 0, keepdims=True))
        member = jnp.where(pick, 1.0, member)
        cur = jnp.where(pick, neg, cur)
    idx = jnp.concatenate(idxs, 0)
    sc = jnp.concatenate(scs, 0)
    gw_ref[...] = sc / jnp.sum(sc, 0, keepdims=True) * ROUTE_SCALE
    before = jnp.dot(member.astype(BF16), tri_ref[...], preferred_element_type=F32) + run_ref[:, 0:1]
    ranks = [jnp.sum(jnp.where(eio == idxs[k], before, 0.0), 0, keepdims=True) for k in range(TOP_K)]
    rank = jnp.concatenate(ranks, 0).astype(I32)
    code_ref[...] = idx.astype(I32) * (1 << RANK_BITS) + rank
    run_ref[...] = run_ref[...] + jnp.sum(member, 1, keepdims=True)
    cnt_ref[...] = run_ref[...]


def _router(x, w_router, bias, tm, cnt0):
    t = x.shape[0]
    tri = jnp.asarray(np.arange(tm)[:, None] < np.arange(tm)[None, :], BF16)
    kt = lambda dt: jax.ShapeDtypeStruct((TOP_K, t), dt)
    return pl.pallas_call(
        _router_kernel, grid=(t // tm,),
        in_specs=[pl.BlockSpec((tm, D_MODEL), lambda i: (i, 0)),
                  pl.BlockSpec((N_EXPERTS, D_MODEL), lambda i: (0, 0)),
                  pl.BlockSpec((N_EXPERTS, LANE), lambda i: (0, 0)),
                  pl.BlockSpec((tm, tm), lambda i: (0, 0)),
                  pl.BlockSpec((N_EXPERTS, LANE), lambda i: (0, 0))],
        out_specs=[pl.BlockSpec((TOP_K, tm), lambda i: (0, i)),
                   pl.BlockSpec((TOP_K, tm), lambda i: (0, i)),
                   pl.BlockSpec((N_EXPERTS, LANE), lambda i: (0, 0))],
        out_shape=[kt(I32), kt(F32), jax.ShapeDtypeStruct((N_EXPERTS, LANE), F32)],
        scratch_shapes=[pltpu.VMEM((N_EXPERTS, LANE), F32)],
        compiler_params=_cparams("arbitrary"), name="moe_router")(
            x, w_router.T.astype(BF16), jnp.broadcast_to(bias[:, None], (N_EXPERTS, LANE)), tri, cnt0)


def _slots_kernel(start_ref, code_ref, pos_ref):
    code = code_ref[...]
    e = lax.shift_right_logical(code, RANK_BITS)
    r = code & RANK_MASK
    pos = r
    for x in range(N_EXPERTS):
        pos = jnp.where(e == x, r + start_ref[x], pos)
    pos_ref[...] = pos * SUBLANES


def _slots(code, slot_start):
    k, t = code.shape
    tile = min(t, 4096)
    gs = pltpu.PrefetchScalarGridSpec(
        num_scalar_prefetch=1, grid=(t // tile,),
        in_specs=[pl.BlockSpec((k, tile), lambda i, *_: (0, i))],
        out_specs=pl.BlockSpec((k, tile), lambda i, *_: (0, i)))
    return pl.pallas_call(_slots_kernel, grid_spec=gs, out_shape=jax.ShapeDtypeStruct((k, t), I32),
                          compiler_params=_cparams("parallel"), name="moe_slots")(slot_start, code)


def _row_tile(ref, row0, lead=()):
    return ref.at[lead + (pl.ds(pl.multiple_of(row0, SUBLANES), SUBLANES),)]


def _dispatch_kernel(fill_ref, pos_ref, x_ref, xb_ref, wsg_ref, wsu_ref, wsd_ref, *rest, eps, fill_sizes):
    xs_ref, sh_ref, zero_ref, sem, fill_sem = rest[-5:]
    step = pl.program_id(0)
    tm = x_ref.shape[0] // SUBLANES
    zero_ref[...] = jnp.zeros_like(zero_ref)

    def copy(r, k):
        return pltpu.make_async_copy(_row_tile(x_ref, r * SUBLANES), _row_tile(xs_ref, pos_ref[0, 0, k * tm + r]), sem)

    def start(r, carry):
        for k in range(TOP_K):
            copy(r, k).start(priority=k % 2)
        return carry

    def wait(r, carry):
        for k in range(TOP_K):
            copy(r, k).wait()
        return carry

    def fill_expert(e, carry):
        base = fill_ref[0, e]
        n = fill_ref[1, e]
        def zero_copy(p):
            first = base + (n & ~(2 * p - 1))
            return pltpu.make_async_copy(
                zero_ref.at[pl.ds(0, p * SUBLANES)],
                xs_ref.at[pl.ds(pl.multiple_of(first * SUBLANES, SUBLANES), p * SUBLANES)], fill_sem)

        for p in fill_sizes:
            pl.when((n & p) != 0)(zero_copy(p).start)
        for p in fill_sizes:
            pl.when((n & p) != 0)(zero_copy(p).wait)
        return carry

    lax.fori_loop(0, tm, start, 0)
    e0 = jnp.minimum(step * eps, N_EXPERTS)
    e1 = jnp.minimum(e0 + eps, N_EXPERTS)
    lax.fori_loop(e0, e1, fill_expert, 0)
    x = xb_ref[...]
    hg = jnp.dot(x, wsg_ref[...], preferred_element_type=F32)
    hu = jnp.dot(x, wsu_ref[...], preferred_element_type=F32)
    hb = (jax.nn.silu(hg) * hu).astype(BF16)
    sh_ref[...] = jnp.dot(hb, wsd_ref[...], preferred_element_type=F32).astype(BF16)
    lax.fori_loop(0, tm, wait, 0)


def _dispatch_shared(xp, xb, pos_tiles, fill, wsg, wsu, wsd, layer, nslot, tm, bm, into=None):
    t = xb.shape[0]
    steps = t // tm
    de = wsg.shape[-1]
    eps = -(-N_EXPERTS // steps)
    fill_sizes = tuple(1 << i for i in reversed(range((bm - 1).bit_length())))
    in_specs = [pl.BlockSpec((1, 1, TOP_K * tm), lambda i, *_: (i, 0, 0), memory_space=pltpu.SMEM),
                pl.BlockSpec((tm * SUBLANES, LANE), lambda i, *_: (i, 0)),
                pl.BlockSpec((tm, D_MODEL), lambda i, *_: (i, 0)),
                pl.BlockSpec((None, D_MODEL, de), lambda i, *_: (layer, 0, 0)),
                pl.BlockSpec((None, D_MODEL, de), lambda i, *_: (layer, 0, 0)),
                pl.BlockSpec((None, de, D_MODEL), lambda i, *_: (layer, 0, 0))]
    args = [fill, pos_tiles, xp, xb, wsg, wsu, wsd]
    if into is not None:
        in_specs.append(pl.BlockSpec(memory_space=pl.ANY))
        args.append(into)
    gs = pltpu.PrefetchScalarGridSpec(
        num_scalar_prefetch=1, grid=(steps,), in_specs=in_specs,
        out_specs=[pl.BlockSpec(memory_space=pl.ANY), pl.BlockSpec((tm, D_MODEL), lambda i, *_: (i, 0))],
        scratch_shapes=[pltpu.VMEM((fill_sizes[0] * SUBLANES, LANE), xp.dtype),
                        pltpu.SemaphoreType.DMA(()), pltpu.SemaphoreType.DMA(())])
    return pl.pallas_call(
        functools.partial(_dispatch_kernel, eps=eps, fill_sizes=fill_sizes), grid_spec=gs,
        out_shape=[jax.ShapeDtypeStruct((nslot * SUBLANES, LANE), xp.dtype),
                   jax.ShapeDtypeStruct((t, D_MODEL), BF16)],
        input_output_aliases={} if into is None else {len(args) - 1: 0},
        compiler_params=_cparams("arbitrary"), name="moe_dispatch_shared")(*args)


def _expert_kernel(be_ref, nu_ref, first_ref, nxt_ref, slot_ref, x_ref, wg_hbm, wu_hbm, wd_hbm, o_ref,
                   wg_buf, wu_buf, wd_buf, sems, *, layer):
    b = pl.program_id(0)
    used = b < nu_ref[0]
    slot = slot_ref[b]

    def fetch(e, s):
        return (pltpu.make_async_copy(wg_hbm.at[layer, e], wg_buf.at[s], sems.at[s, 0]),
                pltpu.make_async_copy(wu_hbm.at[layer, e], wu_buf.at[s], sems.at[s, 1]),
                pltpu.make_async_copy(wd_hbm.at[layer, e], wd_buf.at[s], sems.at[s, 2]))

    @pl.when(b == 0)
    def _():
        for cp in fetch(be_ref[0], 0):
            cp.start()

    @pl.when(jnp.logical_and(used, first_ref[b] == 1))
    def _():
        for cp in fetch(be_ref[b], slot):
            cp.wait()

        @pl.when(nxt_ref[b] >= 0)
        def _():
            for cp in fetch(nxt_ref[b], 1 - slot):
                cp.start()

    @pl.when(used)
    def _():
        lo, hi = _unpack_halves(_load_row_tiles(x_ref, (), x_ref.shape[0] // SUBLANES))
        lo = lo.astype(BF16)
        hi = hi.astype(BF16)
        w = lambda buf, r0, r1: buf[slot, r0:r1, :].astype(BF16)
        hg = (jnp.dot(lo, w(wg_buf, 0, HALF_W), preferred_element_type=F32)
              + jnp.dot(hi, w(wg_buf, HALF_W, D_MODEL), preferred_element_type=F32))
        hu = (jnp.dot(lo, w(wu_buf, 0, HALF_W), preferred_element_type=F32)
              + jnp.dot(hi, w(wu_buf, HALF_W, D_MODEL), preferred_element_type=F32))
        hb = (jax.nn.silu(hg) * hu).astype(BF16)
        out = jnp.dot(hb, w(wd_buf, 0, wd_buf.shape[1]), preferred_element_type=F32)
        _store_row_tiles(o_ref, (), _pack_halves(out))

    @pl.when(jnp.logical_not(used))
    def _():
        o_ref[...] = jnp.zeros_like(o_ref)


def _expert_ffn(xs, blk_e, nused, wg, wu, wd, layer, bm):
    nrows, width = xs.shape
    brows = bm * SUBLANES
    nb = nrows // brows
    de = wg.shape[-1]
    idx = jnp.arange(nb, dtype=I32)
    first = jnp.concatenate([jnp.ones((1,), bool), blk_e[1:] != blk_e[:-1]]) & (idx < nused[0])
    slot = lax.rem(jnp.cumsum(first.astype(I32)) - 1, 2)
    nxt_first = lax.cummin(jnp.where(first, idx, nb), axis=0, reverse=True)
    nxt_after = jnp.concatenate([nxt_first[1:], jnp.full((1,), nb, I32)])
    nxt_e = jnp.where(nxt_after < nb, blk_e[jnp.minimum(nxt_after, nb - 1)], -1)
    hbm = pl.BlockSpec(memory_space=pl.ANY)
    gs = pltpu.PrefetchScalarGridSpec(
        num_scalar_prefetch=5, grid=(nb,),
        in_specs=[pl.BlockSpec((brows, width), lambda b, be, nu, *_: (jnp.minimum(b, nu[0] - 1), 0)),
                  hbm, hbm, hbm],
        out_specs=pl.BlockSpec((brows, width), lambda b, *_: (b, 0)),
        scratch_shapes=[pltpu.VMEM((2, D_MODEL, de), wg.dtype), pltpu.VMEM((2, D_MODEL, de), wu.dtype),
                        pltpu.VMEM((2, de, D_MODEL), wd.dtype), pltpu.SemaphoreType.DMA((2, 3))])
    return pl.pallas_call(
        functools.partial(_expert_kernel, layer=layer), grid_spec=gs,
        out_shape=jax.ShapeDtypeStruct((nrows, width), xs.dtype),
        compiler_params=_cparams("arbitrary"), name="moe_expert_ffn")(
            blk_e, nused, first.astype(I32), nxt_e.astype(I32), slot.astype(I32), xs, wg, wu, wd)


def _combine_kernel(pos_ref, nxt_ref, gw_ref, x_ref, sh_ref, eo_ref, g_ref, b_ref, o_ref, *rest):
    feeds, (buf, sems) = rest[:-2], rest[-2:]
    step = pl.program_id(0)
    nsteps = pl.num_programs(0)
    tm = x_ref.shape[0]
    cur = lax.rem(step, 2)

    def copy(idx_ref, slot, r, k):
        return pltpu.make_async_copy(_row_tile(eo_ref, idx_ref[0, 0, k * tm + r]),
                                     _row_tile(buf, r * SUBLANES, (slot, k)), sems.at[slot])

    def gather(idx_ref, slot):
        def start(r, carry):
            for k in range(TOP_K):
                copy(idx_ref, slot, r, k).start(priority=k % 2)
            return carry
        lax.fori_loop(0, tm, start, 0)

    @pl.when(step == 0)
    def _():
        gather(pos_ref, 0)

    @pl.when(step + 1 < nsteps)
    def _():
        gather(nxt_ref, 1 - cur)

    def wait(r, carry):
        for k in range(TOP_K):
            copy(pos_ref, cur, r, k).wait()
        return carry

    lax.fori_loop(0, tm, wait, 0)
    sh = sh_ref[...].astype(F32)
    lo_acc = sh[:, :HALF_W]
    hi_acc = sh[:, HALF_W:]
    for k in range(TOP_K):
        lo, hi = _unpack_halves(_load_row_tiles(buf, (cur, k), tm))
        wk = gw_ref[:, k:k + 1]
        lo_acc = lo_acc + lo * wk
        hi_acc = hi_acc + hi * wk
    ffn = jnp.concatenate([lo_acc, hi_acc], 1)
    xn = _layer_norm(ALPHA * x_ref[...] + ffn, g_ref[...], b_ref[...])
    if feeds:
        _emit_x(xn, o_ref, *feeds)
    else:
        o_ref[...] = xn


def _combine_ln(pos_tiles, gw_t, x, sh, eo, g, b, tm, feed_next):
    t = x.shape[0]
    steps = t // tm
    row = lambda width: pl.BlockSpec((tm, width), lambda i: (i, 0))
    const = pl.BlockSpec((1, D_MODEL), lambda i: (0, 0))
    idx = lambda f: pl.BlockSpec((1, 1, TOP_K * tm), f, memory_space=pltpu.SMEM)
    out_specs, out_shape = _x_out(t, tm)
    if not feed_next:
        out_specs, out_shape = out_specs[:1], out_shape[:1]
    return pl.pallas_call(
        _combine_kernel, grid=(steps,),
        in_specs=[idx(lambda i: (i, 0, 0)), idx(lambda i: (jnp.minimum(i + 1, steps - 1), 0, 0)),
                  row(TOP_K), row(D_MODEL), row(D_MODEL), pl.BlockSpec(memory_space=pl.ANY), const, const],
        out_specs=out_specs, out_shape=out_shape,
        scratch_shapes=[pltpu.VMEM((2, TOP_K, tm * SUBLANES, LANE), U32), pltpu.SemaphoreType.DMA((2,))],
        compiler_params=_cparams("arbitrary"), name="moe_combine_ln")(
            pos_tiles, pos_tiles, gw_t, x, sh, eo, g.reshape(1, D_MODEL), b.reshape(1, D_MODEL))


def _moe_ln(groups, w_router, bias, wg, wu, wd, wsg, wsu, wsd, layer, g, b, bm, feed_next):
    cnt = jnp.zeros((N_EXPERTS, LANE), F32)
    routed = []
    for x, xb, xp, tm in groups:
        code, gw, cnt = _router(xb, w_router, bias, tm, cnt)
        routed.append((code, gw))
    t_all = sum(x.shape[0] for x, _, _, _ in groups)
    assert t_all * TOP_K < (1 << RANK_BITS)
    counts = cnt[:, 0].astype(I32)
    padded = (counts + bm - 1) // bm * bm
    pad_end = jnp.cumsum(padded)
    slot_start = pad_end - padded
    fill = jnp.stack([slot_start + counts, padded - counts])
    nb = -(-(t_all * TOP_K + N_EXPERTS * (bm - 1)) // bm)
    blk_first = jnp.arange(nb, dtype=I32) * bm
    blk_e = jnp.minimum(jnp.sum((pad_end[None, :] <= blk_first[:, None]).astype(I32), 1), N_EXPERTS - 1)
    nused = (pad_end[-1] // bm).astype(I32).reshape(1)
    xs = None
    pos_tiles, shared = [], []
    for (x, xb, xp, tm), (code, _) in zip(groups, routed):
        nt = x.shape[0] // tm
        pos = _slots(code, slot_start)
        pos_tiles.append(pos.reshape(TOP_K, nt, tm).transpose(1, 0, 2).reshape(nt, 1, TOP_K * tm))
        xs, sh = _dispatch_shared(xp, xb, pos_tiles[-1], fill if xs is None else jnp.zeros_like(fill),
                                  wsg, wsu, wsd, layer, nb * bm, tm, bm, xs)
        shared.append(sh)
    eo = _expert_ffn(xs, blk_e, nused, wg, wu, wd, layer, bm)
    return [_combine_ln(pt, gw.T, x, sh, eo, g, b, tm, feed_next)
            for (x, _, _, tm), (_, gw), pt, sh in zip(groups, routed, pos_tiles, shared)]


def _odd_w_in(w):
    sizes = (H_C * K_C, H_C * K_C, H_C * V_C, GLA_RANK, H_C * V_C, HALF_W, CONV_DIM, H_D)
    q, k, v, lr, r, z, xbc, dt = jnp.split(w, [int(s) for s in np.cumsum(sizes)[:-1]], axis=-1)
    pad = jnp.zeros((w.shape[0], OD_XBC - (OD_LRDT + GLA_RANK + H_D)), w.dtype)
    return jnp.concatenate([q, k, v, r, z, lr, dt, pad, xbc], -1)


def _mix_layer(grp, l, p):
    cfg, seq_len, bsz = grp["cfg"], grp["seq_len"], grp["bsz"]
    x, xb, outs = grp["x"], grp["xb"], grp["outs"]
    j = l // 2
    if l % 2 == 0:
        proj = _proj(x if xb is None else xb, p["ev_w_in"][j], cfg["ptm"], cfg["ptn"])
        ya, v_rows = _mixer_a(proj, p["ev_a_ln_g"][j], p["ev_a_ln_b"][j], p["ev_a_ws"][j],
                              p["ev_a_bs"][j], seq_len, cfg["arows"], cfg["emit_v"])
        yb, s_b = _mixer_b(proj, p["hgrn_lb_logits"], p["ev_b_norm_g"][j], grp["hgrn0"][j], seq_len,
                           cfg["rows"], l)
        outs["v_rows"] = v_rows
        outs["hgrn"] = s_b
        w_out = p["ev_w_out"][j]
    else:
        proj = _proj(x if xb is None else xb, p["od_w_in"][j], cfg["ptm"], cfg["ptn"])
        ya, s_c = _mixer_c(proj, p["od_c_gate_w2"][j], p["od_c_gate_b"][j], p["od_c_norm_g"][j],
                           grp["gla0"][j], seq_len, cfg["rows"])
        yb, s_d = _mixer_d(proj, p["od_d_conv_w"][j], p["od_d_conv_b"][j], p["od_d_dt_bias"][j],
                           p["od_d_a_log"][j], p["od_d_skip"][j], p["od_d_norm_g"][j],
                           grp["ssm0"][j], grp["conv0"][j], seq_len, cfg["rows"])
        keep = min(D_CONV - 1, seq_len)
        xbc = proj.reshape(bsz, seq_len, OD_PAD)[:, seq_len - keep:, OD_XBC:OD_XBC + CONV_DIM]
        outs["conv"] = jnp.concatenate([grp["conv0"][j], xbc], 1)[:, -(D_CONV - 1):]
        outs["gla"] = s_c
        outs["ssm"] = s_d
        w_out = p["od_w_out"][j]
    return _out_proj_ln(ya, yb, x, w_out, p["ln1_g"][l], p["ln1_b"][l], cfg["tm"])


def _group(x3, hgrn0, gla0, ssm0, conv0):
    bsz, seq_len, _ = x3.shape
    return dict(x=x3.reshape(bsz * seq_len, D_MODEL), xb=None, bsz=bsz, seq_len=seq_len, outs={},
                cfg=_config(bsz, seq_len), hgrn0=hgrn0, gla0=gla0, ssm0=ssm0, conv0=conv0)


def _config(bsz, seq_len):
    t = bsz * seq_len
    if seq_len % CHUNK == 0:
        return dict(tm=256, rows=min(256, seq_len), arows=min(256, seq_len),
                    ptm=min(1024, t), ptn=1024, emit_v=False)
    return dict(tm=t, rows=seq_len, arows=t, ptm=t, ptn=512, emit_v=True)


MOE_BLOCK_ROWS = 512


def kernel(x_prompt, x_sample, state_b_hgrn, state_c_gla, state_d_ssm, state_d_conv, ev_w_in, ev_a_ln_g, ev_a_ln_b, ev_a_ws, ev_a_bs, ev_b_norm_g, ev_w_out, hgrn_lb_logits, od_w_in, od_c_gate_w2, od_c_gate_b, od_c_norm_g, od_d_conv_w, od_d_conv_b, od_d_dt_bias, od_d_a_log, od_d_skip, od_d_norm_g, od_w_out, ln1_g, ln1_b, ln2_g, ln2_b, moe_w_router, moe_router_bias, moe_w_gate, moe_w_up, moe_w_down, moe_ws_gate, moe_ws_up, moe_ws_down):
    p = dict(
        ev_w_in=ev_w_in.astype(BF16), ev_a_ln_g=ev_a_ln_g, ev_a_ln_b=ev_a_ln_b, ev_a_ws=ev_a_ws,
        ev_a_bs=ev_a_bs, ev_b_norm_g=ev_b_norm_g, ev_w_out=ev_w_out.astype(BF16),
        hgrn_lb_logits=hgrn_lb_logits,
        od_w_in=jnp.stack([_odd_w_in(od_w_in[j]) for j in range(od_w_in.shape[0])]).astype(BF16),
        od_c_gate_w2=od_c_gate_w2, od_c_gate_b=od_c_gate_b, od_c_norm_g=od_c_norm_g,
        od_d_conv_w=od_d_conv_w, od_d_conv_b=od_d_conv_b, od_d_dt_bias=od_d_dt_bias,
        od_d_a_log=od_d_a_log, od_d_skip=od_d_skip, od_d_norm_g=od_d_norm_g,
        od_w_out=od_w_out.astype(BF16), ln1_g=ln1_g, ln1_b=ln1_b, ln2_g=ln2_g, ln2_b=ln2_b,
        moe_w_router=moe_w_router, moe_router_bias=moe_router_bias,
        moe_w_gate=moe_w_gate, moe_w_up=moe_w_up, moe_w_down=moe_w_down,
        moe_ws_gate=moe_ws_gate.astype(BF16),
        moe_ws_up=moe_ws_up.astype(BF16), moe_ws_down=moe_ws_down.astype(BF16))
    bp, lp, _ = x_prompt.shape
    n_even = state_b_hgrn.shape[0]
    n_odd = state_c_gla.shape[0]
    zeros = lambda n, *s: jnp.zeros((n, bp) + s, F32)
    bs, ls, _ = x_sample.shape
    groups = [_group(x_prompt, zeros(n_even, H_B, K_B, V_B), zeros(n_odd, H_C, K_C, V_C),
                     zeros(n_odd, H_D, N_D, P_D), zeros(n_odd, D_CONV - 1, CONV_DIM)),
              _group(x_sample, state_b_hgrn, state_c_gla, state_d_ssm, state_d_conv)]
    for l in range(DEPTH):
        mixed = [_mix_layer(grp, l, p) for grp in groups]
        res = _moe_ln([(x, xb, xp, grp["cfg"]["tm"]) for (x, xb, xp), grp in zip(mixed, groups)],
                      p["moe_w_router"][l], p["moe_router_bias"][l], p["moe_w_gate"], p["moe_w_up"],
                      p["moe_w_down"], p["moe_ws_gate"], p["moe_ws_up"], p["moe_ws_down"], l,
                      p["ln2_g"][l], p["ln2_b"][l], MOE_BLOCK_ROWS, l + 1 < DEPTH)
        for grp, out in zip(groups, res):
            grp["x"], grp["xb"] = out[0], (out[1] if len(out) > 1 else None)
    y_p = groups[0]["x"].reshape(bp, lp, D_MODEL)
    y_s = groups[1]["x"].reshape(bs, ls, D_MODEL)
    o_p, o_s = groups[0]["outs"], groups[1]["outs"]
    a_v = o_s["v_rows"].reshape(1, bs, ls, HALF_W)
    return (y_p, y_s, a_v, o_p["hgrn"][None], o_s["hgrn"][None], o_p["gla"][None], o_s["gla"][None],
            o_p["ssm"][None], o_s["ssm"][None], o_p["conv"][None], o_s["conv"][None])
```

```python
import functools
import math

import jax
import jax.numpy as jnp
import numpy as np
from jax import lax
from jax.experimental import pallas as pl
from jax.experimental.pallas import tpu as pltpu

F32 = jnp.float32
BF16 = jnp.bfloat16
I32 = jnp.int32

D_MODEL = 2048
DEPTH = 2
CHUNK = 64
SUB = 16
HALF_W = D_MODEL // 2
A_CHUNK = 128
H_A = 4
DA = HALF_W // H_A
H_B = 8
K_B = 128
V_B = HALF_W // H_B
H_C = 4
V_C = HALF_W // H_C
K_C = V_C // 2
GLA_RANK = 16
GLA_TAU = 16.0
P_D = 64
H_D = HALF_W // P_D
G_D = 2
HPG_D = H_D // G_D
N_D = 128
D_CONV = 4
CONV_DIM = HALF_W + 2 * G_D * N_D
N_EXPERTS = 64
N_GROUPS = 8
GROUP_SIZE = N_EXPERTS // N_GROUPS
TOPK_GROUPS = 4
TOP_K = 8
D_EXPERT = 512
ROUTE_SCALE = 2.5
ALPHA = (2 * DEPTH) ** 0.25
EPS = 1e-5
LANE = 128
VMEM_LIMIT = 56 * 1024 * 1024

OD_Q, OD_K, OD_V, OD_R, OD_Z, OD_LRDT, OD_XBC = 0, 512, 1024, 2048, 3072, 4096, 4608
OD_PAD = 6144
DT_OFF = GLA_RANK


def _cparams(*sem):
    return pltpu.CompilerParams(dimension_semantics=sem, vmem_limit_bytes=VMEM_LIMIT)


def _split3(x):
    hi = x.astype(BF16)
    r = x - hi.astype(F32)
    mid = r.astype(BF16)
    lo = (r - mid.astype(F32)).astype(BF16)
    return hi, mid, lo


def _split2(x):
    hi = x.astype(BF16)
    return hi, (x - hi.astype(F32)).astype(BF16)


def _dot_exact_l(a_bf16, x):
    return sum(jnp.dot(a_bf16, p, preferred_element_type=F32) for p in _split3(x))


def _dot_exact_r(x, b_bf16):
    return sum(jnp.dot(p, b_bf16, preferred_element_type=F32) for p in _split3(x))


def _dot_nt(a, b):
    return lax.dot_general(a, b, (((1,), (1,)), ((), ())), preferred_element_type=F32)


def _softplus(x):
    return jnp.maximum(x, 0.0) + jnp.log1p(jnp.exp(-jnp.abs(x)))


def _proj_kernel(x_ref, w_ref, o_ref, *scratch):
    if scratch:
        xb_ref, = scratch

        @pl.when(pl.program_id(1) == 0)
        def _():
            xb_ref[...] = x_ref[...].astype(BF16)
        x = xb_ref[...]
    else:
        x = x_ref[...]
    o_ref[...] = jnp.dot(x, w_ref[...], preferred_element_type=F32)


def _proj(x, w, tm, tn):
    t, k = x.shape
    n = w.shape[1]
    scratch = [] if x.dtype == BF16 else [pltpu.VMEM((tm, k), BF16)]
    return pl.pallas_call(
        _proj_kernel, grid=(t // tm, n // tn),
        in_specs=[pl.BlockSpec((tm, k), lambda i, j: (i, 0)),
                  pl.BlockSpec((k, tn), lambda i, j: (0, j))],
        out_specs=pl.BlockSpec((tm, tn), lambda i, j: (i, j)),
        out_shape=jax.ShapeDtypeStruct((t, n), F32),
        scratch_shapes=scratch, compiler_params=_cparams("parallel", "arbitrary"),
        name="in_proj")(x, w)


def _mixa_kernel(u_ref, v_ref, lng_ref, lnb_ref, w_ref, bs_ref, ya_ref, *vrows, ca):
    gu = jax.nn.gelu(u_ref[...])
    gv = jax.nn.gelu(v_ref[...])
    rows = gu.shape[0]
    for h in range(H_A):
        sl = slice(h * DA, (h + 1) * DA)
        vh = gv[:, sl]
        mu = jnp.mean(vh, -1, keepdims=True)
        d = vh - mu
        var = jnp.mean(d * d, -1, keepdims=True)
        vn = d * lax.rsqrt(var + EPS) * lng_ref[h] + lnb_ref[h]
        if vrows:
            vrows[0][:, sl] = vn
        vnb = vn.astype(BF16)
        for c in range(rows // ca):
            rs = slice(c * ca, (c + 1) * ca)
            s = jnp.dot(w_ref[h], vnb[rs], preferred_element_type=F32) + bs_ref[h]
            ya_ref[rs, sl] = (gu[rs, sl] * s).astype(BF16)


def _mixer_a(proj, lng, lnb, ws, bs, seq_len, rows, emit_v):
    t = proj.shape[0]
    ca = min(A_CHUNK, seq_len)
    pos = np.arange(A_CHUNK)
    mask = (pos[None, :] // CHUNK) <= (pos[:, None] // CHUNK)
    w = (ws * mask)[:, :ca, :ca].astype(BF16)
    bsb = jnp.broadcast_to(bs[:, :ca, None], (H_A, ca, DA)).astype(F32)
    nb = HALF_W // HALF_W
    out_shape = [jax.ShapeDtypeStruct((t, HALF_W), BF16)]
    out_specs = [pl.BlockSpec((rows, HALF_W), lambda i: (i, 0))]
    if emit_v:
        out_shape.append(jax.ShapeDtypeStruct((t, HALF_W), F32))
        out_specs.append(pl.BlockSpec((rows, HALF_W), lambda i: (i, 0)))
    del nb
    res = pl.pallas_call(
        functools.partial(_mixa_kernel, ca=ca), grid=(t // rows,),
        in_specs=[pl.BlockSpec((rows, HALF_W), lambda i: (i, 0)),
                  pl.BlockSpec((rows, HALF_W), lambda i: (i, 1)),
                  pl.BlockSpec((H_A, 1, DA), lambda i: (0, 0, 0)),
                  pl.BlockSpec((H_A, 1, DA), lambda i: (0, 0, 0)),
                  pl.BlockSpec((H_A, ca, ca), lambda i: (0, 0, 0)),
                  pl.BlockSpec((H_A, ca, DA), lambda i: (0, 0, 0))],
        out_specs=out_specs, out_shape=out_shape,
        compiler_params=_cparams("parallel"), name="mixer_a")(
            proj, proj, lng[:, None, :], lnb[:, None, :], w, bsb)
    return res if emit_v else (res[0], None)


def _scan_mats(c):
    i = np.arange(c)[:, None]
    j = np.arange(c)[None, :]
    tril = (j <= i)
    local = tril & ((i // SUB) == (j // SUB))
    ones = np.ones((c, c), bool)
    return jnp.asarray(np.concatenate([local, tril, ones], 0), BF16)


def _cat(parts, axis):
    return parts[0] if len(parts) == 1 else jnp.concatenate(parts, axis)


def _gla_intra(q, k, v, sc, c, nh, kd, vd):
    lc, cum, last = sc[0:c], sc[c:2 * c], sc[2 * c:3 * c]
    ns = c // SUB
    q_hi, q_lo = _split2(q * jnp.exp(lc))
    pre = cum - lc
    vb = v.astype(BF16)
    att = []
    for blk in range(ns):
        n = SUB * (blk + 1)
        r0 = SUB * blk
        k_hi, k_lo = _split2(k[:n] * jnp.exp(pre[r0:r0 + 1] - cum[:n]))
        rq = slice(r0, r0 + SUB)
        for h in range(nh):
            ks = slice(h * kd, (h + 1) * kd)
            att.append(_dot_nt(q_hi[rq, ks], k_hi[:, ks])
                       + (_dot_nt(q_hi[rq, ks], k_lo[:, ks]) + _dot_nt(q_lo[rq, ks], k_hi[:, ks])))
    attb = []
    for blk in range(ns):
        n = SUB * (blk + 1)
        keep = (lax.broadcasted_iota(I32, (SUB, n), 1) - SUB * blk) <= lax.broadcasted_iota(I32, (SUB, n), 0)
        attb += [jnp.where(keep, att[blk * nh + h], 0.0).astype(BF16) for h in range(nh)]
    rows_out = []
    for blk in range(ns):
        n = SUB * (blk + 1)
        rows_out.append(_cat([jnp.dot(attb[blk * nh + h], vb[:n, h * vd:(h + 1) * vd],
                                      preferred_element_type=F32) for h in range(nh)], 1))
    qc = (q * jnp.exp(cum)).astype(BF16)
    kc = (k * jnp.exp(last - cum)).astype(BF16)
    return _cat(rows_out, 0), qc, kc, v.T.astype(BF16), jnp.exp(last[0:1])


def _gla_state(intra, s_ref, nh, kd, vd):
    o_intra, qc, kc, vt, e_last = intra
    sts = [s_ref[h] for h in range(nh)]
    inter = [_dot_nt(qc[:, h * kd:(h + 1) * kd], sts[h].astype(BF16)) for h in range(nh)]
    upd = [jnp.dot(vt[h * vd:(h + 1) * vd], kc[:, h * kd:(h + 1) * kd], preferred_element_type=F32)
           for h in range(nh)]
    for h in range(nh):
        s_ref[h] = sts[h] * e_last[:, h * kd:(h + 1) * kd] + upd[h]
    return o_intra + _cat(inter, 1)


def _hgrn_kernel(q_ref, f_ref, i_ref, g_ref, lbl_ref, ng_ref, s0_ref, cm_ref, y_ref, so_ref, s_ref,
                 *, c, layer):
    step = pl.program_id(1)

    @pl.when(step == 0)
    def _():
        s_ref[...] = s0_ref[0]

    lg = lbl_ref[...]
    ex = jnp.exp(lg - jnp.max(lg, 0, keepdims=True))
    sm = ex / jnp.sum(ex, 0, keepdims=True)
    lb = jnp.sum(sm[:layer + 1], 0, keepdims=True)
    rows = q_ref.shape[0]
    intra = []
    for ci in range(rows // c):
        rs = slice(ci * c, (ci + 1) * c)
        f = lb + (1.0 - lb) * jax.nn.sigmoid(f_ref[rs])
        g = jnp.log(f)
        sc = _dot_exact_l(cm_ref[...], g)
        intra.append(_gla_intra(jax.nn.silu(q_ref[rs]), 1.0 - f, i_ref[rs], sc, c, H_B, K_B, V_B))
    for ci in range(rows // c):
        rs = slice(ci * c, (ci + 1) * c)
        o = _gla_state(intra[ci], s_ref, H_B, K_B, V_B)
        rn = _cat([jnp.broadcast_to(lax.rsqrt(jnp.mean(jnp.square(o[:, h * V_B:(h + 1) * V_B]), -1,
                                                        keepdims=True) + EPS), (c, V_B))
                   for h in range(H_B)], 1)
        y_ref[rs, :] = (o * rn * ng_ref[...] * jax.nn.sigmoid(g_ref[rs])).astype(BF16)

    @pl.when(step == pl.num_programs(1) - 1)
    def _():
        so_ref[0] = s_ref[...]


def _mixer_b(proj, lb_logits, norm_g, s0, seq_len, rows, layer):
    t = proj.shape[0]
    bsz = t // seq_len
    c = CHUNK if seq_len % CHUNK == 0 else seq_len
    spb = seq_len // rows
    s0t = jnp.swapaxes(s0, -1, -2)
    col = lambda j: pl.BlockSpec((rows, HALF_W), lambda b, s, j=j: (b * spb + s, j))
    nl = lb_logits.shape[0]
    y, st = pl.pallas_call(
        functools.partial(_hgrn_kernel, c=c, layer=layer), grid=(bsz, spb),
        in_specs=[col(2), col(3), col(4), col(5),
                  pl.BlockSpec((nl, HALF_W), lambda b, s: (0, 0)),
                  pl.BlockSpec((1, HALF_W), lambda b, s: (0, 0)),
                  pl.BlockSpec((1, H_B, V_B, K_B), lambda b, s: (b, 0, 0, 0)),
                  pl.BlockSpec((3 * c, c), lambda b, s: (0, 0))],
        out_specs=[pl.BlockSpec((rows, HALF_W), lambda b, s: (b * spb + s, 0)),
                   pl.BlockSpec((1, H_B, V_B, K_B), lambda b, s: (b, 0, 0, 0))],
        out_shape=[jax.ShapeDtypeStruct((t, HALF_W), BF16),
                   jax.ShapeDtypeStruct((bsz, H_B, V_B, K_B), F32)],
        scratch_shapes=[pltpu.VMEM((H_B, V_B, K_B), F32)],
        compiler_params=_cparams("parallel", "arbitrary"), name="mixer_b_hgrn")(
            proj, proj, proj, proj, lb_logits, norm_g.reshape(1, HALF_W), s0t, _scan_mats(c))
    return y, jnp.swapaxes(st, -1, -2)


def _glac_kernel(q_ref, k_ref, v_ref, r_ref, lrdt_ref, w2_ref, gb_ref, ng_ref, s0_ref, cm_ref,
                 y_ref, so_ref, s_ref, *, c):
    step = pl.program_id(1)

    @pl.when(step == 0)
    def _():
        s_ref[...] = s0_ref[0]

    rows = q_ref.shape[0]
    intra = []
    for ci in range(rows // c):
        rs = slice(ci * c, (ci + 1) * c)
        z = jnp.dot(lrdt_ref[rs], w2_ref[...], preferred_element_type=F32,
                    precision=lax.Precision.HIGHEST) + gb_ref[...]
        g = -_softplus(-z) / GLA_TAU
        sc = _dot_exact_l(cm_ref[...], g)
        intra.append(_gla_intra(q_ref[rs] * (K_C ** -0.5), k_ref[rs], v_ref[rs], sc, c, H_C, K_C, V_C))
    for ci in range(rows // c):
        rs = slice(ci * c, (ci + 1) * c)
        o = _gla_state(intra[ci], s_ref, H_C, K_C, V_C)
        rn = _cat([jnp.broadcast_to(lax.rsqrt(jnp.mean(jnp.square(o[:, h * V_C:(h + 1) * V_C]), -1,
                                                        keepdims=True) + EPS), (c, V_C))
                   for h in range(H_C)], 1)
        y_ref[rs, :] = (o * rn * ng_ref[...] * jax.nn.silu(r_ref[rs])).astype(BF16)

    @pl.when(step == pl.num_programs(1) - 1)
    def _():
        so_ref[0] = s_ref[...]


def _mixer_c(proj, gate_w2, gate_b, norm_g, s0, seq_len, rows):
    t = proj.shape[0]
    bsz = t // seq_len
    c = CHUNK if seq_len % CHUNK == 0 else seq_len
    spb = seq_len // rows
    s0t = jnp.swapaxes(s0, -1, -2)
    hk = H_C * K_C

    def col(width, off):
        return pl.BlockSpec((rows, width), lambda b, s: (b * spb + s, off // width))

    y, st = pl.pallas_call(
        functools.partial(_glac_kernel, c=c), grid=(bsz, spb),
        in_specs=[col(hk, OD_Q), col(hk, OD_K), col(HALF_W, OD_V), col(HALF_W, OD_R),
                  col(LANE, OD_LRDT),
                  pl.BlockSpec((LANE, hk), lambda b, s: (0, 0)),
                  pl.BlockSpec((1, hk), lambda b, s: (0, 0)),
                  pl.BlockSpec((1, HALF_W), lambda b, s: (0, 0)),
                  pl.BlockSpec((1, H_C, V_C, K_C), lambda b, s: (b, 0, 0, 0)),
                  pl.BlockSpec((3 * c, c), lambda b, s: (0, 0))],
        out_specs=[pl.BlockSpec((rows, HALF_W), lambda b, s: (b * spb + s, 0)),
                   pl.BlockSpec((1, H_C, V_C, K_C), lambda b, s: (b, 0, 0, 0))],
        out_shape=[jax.ShapeDtypeStruct((t, HALF_W), BF16),
                   jax.ShapeDtypeStruct((bsz, H_C, V_C, K_C), F32)],
        scratch_shapes=[pltpu.VMEM((H_C, V_C, K_C), F32)],
        compiler_params=_cparams("parallel", "arbitrary"), name="mixer_c_gla")(
            proj, proj, proj, proj, proj,
            jnp.pad(gate_w2, ((0, LANE - GLA_RANK), (0, 0))),
            gate_b.reshape(1, hk),
            norm_g.reshape(1, HALF_W), s0t, _scan_mats(c))
    return y, jnp.swapaxes(st, -1, -2)


GW = HPG_D * P_D


def _ssd_kernel(z_ref, xbc_ref, lrdt_ref, cw_ref, cb_ref, dtb_ref, alog_ref, skip_ref, ng_ref,
                ex_ref, eye_ref, cm_ref, conv0_ref, s0_ref, y_ref, so_ref, s_ref, tail_ref, *, c):
    step = pl.program_id(1)

    @pl.when(step == 0)
    def _():
        s_ref[...] = s0_ref[0]
        tail_ref[...] = conv0_ref[0]

    x = xbc_ref[...]
    rows = x.shape[0]
    tail = tail_ref[...]
    sub8 = lax.broadcasted_iota(I32, (8, CONV_DIM), 0)
    conv = x * cw_ref[D_CONV - 1:D_CONV]
    for sh in range(1, D_CONV):
        rolled = pltpu.roll(x, sh, 0)
        head = jnp.where(sub8 < sh, pltpu.roll(tail, sh, 0), rolled[0:8])
        xk = jnp.concatenate([head, rolled[8:]], 0) if rows > 8 else head
        conv = conv + xk * cw_ref[D_CONV - 1 - sh:D_CONV - sh]
    tail_ref[...] = x[rows - 8:rows]
    xc = jax.nn.silu(conv + cb_ref[...])
    a = -jnp.exp(alog_ref[...])
    ex = ex_ref[...]
    tri = lax.broadcasted_iota(I32, (c, c), 1) <= lax.broadcasted_iota(I32, (c, c), 0)
    group = lambda arr, g, w: arr[:, g * w:(g + 1) * w]
    intra = []
    for ci in range(rows // c):
        rs = slice(ci * c, (ci + 1) * c)
        xs = xc[rs, 0:HALF_W]
        bm = xc[rs, HALF_W:HALF_W + G_D * N_D]
        cmat = xc[rs, HALF_W + G_D * N_D:CONV_DIM].astype(BF16)
        dt = _softplus(lrdt_ref[rs] + dtb_ref[...])
        la = dt * a
        sc = _dot_exact_l(cm_ref[...], la)
        cum = sc[0:c]
        scx = _dot_exact_r(sc, ex)
        cumx = scx[0:c]
        lastx = scx[c:2 * c]
        xdt = xs * _dot_exact_r(dt, ex)
        cum_t = sum(_dot_nt(eye_ref[...], p) for p in _split3(cum))
        cbs = [_dot_nt(group(cmat, g, N_D), group(bm, g, N_D).astype(BF16)) for g in range(G_D)]
        xdtb = xdt.astype(BF16)
        wmats = []
        for h in range(H_D):
            hl = DT_OFF + h
            diff = cum[:, hl:hl + 1] - cum_t[hl:hl + 1, :]
            dec = jnp.where(tri, jnp.exp(jnp.minimum(diff, 0.0)), 0.0)
            wmats.append((cbs[h // HPG_D] * dec).astype(BF16))
        y_intra = _cat([jnp.dot(wmats[h], xdtb[:, h * P_D:(h + 1) * P_D], preferred_element_type=F32)
                        for h in range(H_D)], 1)
        intra.append((y_intra + xs * skip_ref[...], cmat, jnp.exp(cumx),
                      bm.T.astype(BF16), (xdt * jnp.exp(lastx - cumx)).astype(BF16), jnp.exp(lastx[0:1])))
    for ci in range(rows // c):
        rs = slice(ci * c, (ci + 1) * c)
        y_local, cmat, e_in, bm_t, xw, e_last = intra[ci]
        sgs = [s_ref[g] for g in range(G_D)]
        inter = [jnp.dot(group(cmat, g, N_D), sgs[g].astype(BF16), preferred_element_type=F32)
                 for g in range(G_D)]
        upd = [jnp.dot(bm_t[g * N_D:(g + 1) * N_D], group(xw, g, GW), preferred_element_type=F32)
               for g in range(G_D)]
        for g in range(G_D):
            s_ref[g] = sgs[g] * group(e_last, g, GW) + upd[g]
        yg = (y_local + _cat(inter, 1) * e_in) * jax.nn.silu(z_ref[rs])
        rn = _cat([jnp.broadcast_to(lax.rsqrt(jnp.mean(jnp.square(group(yg, g, GW)), -1, keepdims=True) + EPS),
                                    (c, GW)) for g in range(G_D)], 1)
        y_ref[rs, :] = (yg * rn * ng_ref[...]).astype(BF16)

    @pl.when(step == pl.num_programs(1) - 1)
    def _():
        so_ref[0] = s_ref[...]


def _mixer_d(proj, conv_w, conv_b, dt_bias, a_log, skip, norm_g, s0, conv0, seq_len, rows):
    t = proj.shape[0]
    bsz = t // seq_len
    c = CHUNK if seq_len % CHUNK == 0 else seq_len
    spb = seq_len // rows
    s0g = s0.reshape(bsz, G_D, HPG_D, N_D, P_D).transpose(0, 1, 3, 2, 4).reshape(bsz, G_D, N_D, GW)
    conv0p = jnp.pad(conv0, ((0, 0), (8 - (D_CONV - 1), 0), (0, 0)))
    expand_np = np.zeros((LANE, HALF_W), np.float32)
    expand_np[DT_OFF:DT_OFF + H_D] = np.repeat(np.eye(H_D), P_D, axis=1)
    expand = jnp.asarray(expand_np, BF16)
    eye = jnp.asarray(np.eye(LANE), BF16)
    i = np.arange(c)[:, None]
    j = np.arange(c)[None, :]
    cm = jnp.asarray(np.concatenate([j <= i, np.ones((c, c), bool)], 0), BF16)
    skipx = jnp.repeat(skip, P_D)[None, :]
    lane_pad = lambda v: jnp.pad(v.reshape(1, H_D), ((0, 0), (DT_OFF, LANE - DT_OFF - H_D)))

    def col(width, off):
        return pl.BlockSpec((rows, width), lambda b, s: (b * spb + s, off // width))

    def full(shape):
        return pl.BlockSpec(shape, lambda b, s: (0,) * len(shape))

    y, st = pl.pallas_call(
        functools.partial(_ssd_kernel, c=c), grid=(bsz, spb),
        in_specs=[col(HALF_W, OD_Z),
                  col(CONV_DIM, OD_XBC),
                  col(LANE, OD_LRDT),
                  full((D_CONV, CONV_DIM)), full((1, CONV_DIM)), full((1, LANE)), full((1, LANE)),
                  full((1, HALF_W)), full((1, HALF_W)), full((LANE, HALF_W)), full((LANE, LANE)),
                  full((2 * c, c)),
                  pl.BlockSpec((1, 8, CONV_DIM), lambda b, s: (b, 0, 0)),
                  pl.BlockSpec((1, G_D, N_D, GW), lambda b, s: (b, 0, 0, 0))],
        out_specs=[pl.BlockSpec((rows, HALF_W), lambda b, s: (b * spb + s, 0)),
                   pl.BlockSpec((1, G_D, N_D, GW), lambda b, s: (b, 0, 0, 0))],
        out_shape=[jax.ShapeDtypeStruct((t, HALF_W), BF16),
                   jax.ShapeDtypeStruct((bsz, G_D, N_D, GW), F32)],
        scratch_shapes=[pltpu.VMEM((G_D, N_D, GW), F32), pltpu.VMEM((8, CONV_DIM), F32)],
        compiler_params=_cparams("parallel", "arbitrary"), name="mixer_d_ssd")(
            proj, proj, proj, conv_w, conv_b.reshape(1, CONV_DIM), lane_pad(dt_bias),
            lane_pad(a_log), skipx, norm_g.reshape(1, HALF_W), expand, eye, cm, conv0p, s0g)
    st = st.reshape(bsz, G_D, N_D, HPG_D, P_D).transpose(0, 1, 3, 2, 4).reshape(bsz, H_D, N_D, P_D)
    return y, st


def _layer_norm(hpre, g, b):
    mu = jnp.mean(hpre, -1, keepdims=True)
    d = hpre - mu
    var = jnp.mean(d * d, -1, keepdims=True)
    return d * lax.rsqrt(var + EPS) * g + b


U32 = jnp.uint32
HI_MASK = 0xFFFF0000


def _pack_halves(x):
    half = x.shape[1] // 2
    lo = lax.bitcast_convert_type(x[:, :half].astype(BF16).astype(F32), U32) >> 16
    hi = lax.bitcast_convert_type(x[:, half:].astype(BF16).astype(F32), U32) & U32(HI_MASK)
    return lo | hi


def _unpack_halves(w):
    return (lax.bitcast_convert_type(w << 16, F32), lax.bitcast_convert_type(w & U32(HI_MASK), F32))


SUBLANES = 8
assert HALF_W == SUBLANES * LANE


def _store_row_tiles(ref, lead, w):
    m = w.shape[0]
    for s in range(SUBLANES):
        ref[lead + (pl.ds(s, m, stride=SUBLANES), slice(None))] = w[:, s * LANE:(s + 1) * LANE]


def _load_row_tiles(ref, lead, m):
    return jnp.concatenate([ref[lead + (pl.ds(s, m, stride=SUBLANES), slice(None))] for s in range(SUBLANES)], 1)


def _emit_x(xn, o_ref, ob_ref, op_ref):
    o_ref[...] = xn
    ob_ref[...] = xn.astype(BF16)
    _store_row_tiles(op_ref, (), _pack_halves(xn))


def _x_out(t, tm):
    row = lambda width: pl.BlockSpec((tm, width), lambda i, *_: (i, 0))
    return ([row(D_MODEL), row(D_MODEL), pl.BlockSpec((tm * SUBLANES, LANE), lambda i, *_: (i, 0))],
            [jax.ShapeDtypeStruct((t, D_MODEL), F32), jax.ShapeDtypeStruct((t, D_MODEL), BF16),
             jax.ShapeDtypeStruct((t * SUBLANES, LANE), U32)])


def _outproj_kernel(ya_ref, yb_ref, x_ref, w_ref, g_ref, b_ref, o_ref, ob_ref, op_ref):
    acc = jnp.dot(ya_ref[...], w_ref[0:HALF_W, :], preferred_element_type=F32)
    acc = acc + jnp.dot(yb_ref[...], w_ref[HALF_W:D_MODEL, :], preferred_element_type=F32)
    _emit_x(_layer_norm(ALPHA * x_ref[...] + acc, g_ref[...], b_ref[...]), o_ref, ob_ref, op_ref)


def _out_proj_ln(ya, yb, x, w, g, b, tm):
    t = x.shape[0]
    row = lambda width: pl.BlockSpec((tm, width), lambda i: (i, 0))
    out_specs, out_shape = _x_out(t, tm)
    return pl.pallas_call(
        _outproj_kernel, grid=(t // tm,),
        in_specs=[row(HALF_W), row(HALF_W), row(D_MODEL),
                  pl.BlockSpec((D_MODEL, D_MODEL), lambda i: (0, 0)),
                  pl.BlockSpec((1, D_MODEL), lambda i: (0, 0)),
                  pl.BlockSpec((1, D_MODEL), lambda i: (0, 0))],
        out_specs=out_specs, out_shape=out_shape,
        compiler_params=_cparams("parallel"), name="out_proj_ln")(
            ya, yb, x, w, g.reshape(1, D_MODEL), b.reshape(1, D_MODEL))


RANK_BITS = 20
RANK_MASK = (1 << RANK_BITS) - 1


def _router_kernel(x_ref, wr_ref, bias_ref, tri_ref, cnt0_ref, code_ref, gw_ref, cnt_ref, run_ref):
    step = pl.program_id(0)

    @pl.when(step == 0)
    def _():
        run_ref[...] = cnt0_ref[...]

    tm = x_ref.shape[0]
    logits = _dot_nt(wr_ref[...], x_ref[...])
    scores = jax.nn.sigmoid(logits)
    sel = scores + bias_ref[:, 0:1]
    neg = -jnp.inf
    sub = lax.broadcasted_iota(I32, (GROUP_SIZE, tm), 0).astype(F32)
    gsc = []
    for g in range(N_GROUPS):
        blk = sel[g * GROUP_SIZE:(g + 1) * GROUP_SIZE]
        m1 = jnp.max(blk, 0, keepdims=True)
        i1 = jnp.min(jnp.where(blk == m1, sub, float(GROUP_SIZE)), 0, keepdims=True)
        m2 = jnp.max(jnp.where(sub == i1, neg, blk), 0, keepdims=True)
        gsc.append(m1 + m2)
    cur = jnp.concatenate(gsc, 0)
    gio = lax.broadcasted_iota(I32, (N_GROUPS, tm), 0).astype(F32)
    gmask = jnp.zeros((N_GROUPS, tm), F32)
    for _ in range(TOPK_GROUPS):
        m = jnp.max(cur, 0, keepdims=True)
        i = jnp.min(jnp.where(cur == m, gio, float(N_GROUPS)), 0, keepdims=True)
        pick = gio == i
        gmask = jnp.where(pick, 1.0, gmask)
        cur = jnp.where(pick, neg, cur)
    emask = jnp.concatenate(
        [jnp.broadcast_to(gmask[g:g + 1], (GROUP_SIZE, tm)) for g in range(N_GROUPS)], 0)
    cur = jnp.where(emask > 0.5, sel, neg)
    eio = lax.broadcasted_iota(I32, (N_EXPERTS, tm), 0).astype(F32)
    member = jnp.zeros((N_EXPERTS, tm), F32)
    idxs, scs = [], []
    for _ in range(TOP_K):
        m = jnp.max(cur, 0, keepdims=True)
        i = jnp.min(jnp.where(cur == m, eio, float(N_EXPERTS)), 0, keepdims=True)
        pick = eio == i
        idxs.append(i)
        scs.append(jnp.sum(jnp.where(pick, scores, 0.0), 0, keepdims=True))
        member = jnp.where(pick, 1.0, member)
        cur = jnp.where(pick, neg, cur)
    idx = jnp.concatenate(idxs, 0)
    sc = jnp.concatenate(scs, 0)
    gw_ref[...] = sc / jnp.sum(sc, 0, keepdims=True) * ROUTE_SCALE
    before = jnp.dot(member.astype(BF16), tri_ref[...], preferred_element_type=F32) + run_ref[:, 0:1]
    ranks = [jnp.sum(jnp.where(eio == idxs[k], before, 0.0), 0, keepdims=True) for k in range(TOP_K)]
    rank = jnp.concatenate(ranks, 0).astype(I32)
    code_ref[...] = idx.astype(I32) * (1 << RANK_BITS) + rank
    run_ref[...] = run_ref[...] + jnp.sum(member, 1, keepdims=True)
    cnt_ref[...] = run_ref[...]


def _router(x, w_router, bias, tm, cnt0):
    t = x.shape[0]
    tri = jnp.asarray(np.arange(tm)[:, None] < np.arange(tm)[None, :], BF16)
    kt = lambda dt: jax.ShapeDtypeStruct((TOP_K, t), dt)
    return pl.pallas_call(
        _router_kernel, grid=(t // tm,),
        in_specs=[pl.BlockSpec((tm, D_MODEL), lambda i: (i, 0)),
                  pl.BlockSpec((N_EXPERTS, D_MODEL), lambda i: (0, 0)),
                  pl.BlockSpec((N_EXPERTS, LANE), lambda i: (0, 0)),
                  pl.BlockSpec((tm, tm), lambda i: (0, 0)),
                  pl.BlockSpec((N_EXPERTS, LANE), lambda i: (0, 0))],
        out_specs=[pl.BlockSpec((TOP_K, tm), lambda i: (0, i)),
                   pl.BlockSpec((TOP_K, tm), lambda i: (0, i)),
                   pl.BlockSpec((N_EXPERTS, LANE), lambda i: (0, 0))],
        out_shape=[kt(I32), kt(F32), jax.ShapeDtypeStruct((N_EXPERTS, LANE), F32)],
        scratch_shapes=[pltpu.VMEM((N_EXPERTS, LANE), F32)],
        compiler_params=_cparams("arbitrary"), name="moe_router")(
            x, w_router.T.astype(BF16), jnp.broadcast_to(bias[:, None], (N_EXPERTS, LANE)), tri, cnt0)


def _slots_kernel(start_ref, code_ref, pos_ref):
    code = code_ref[...]
    e = lax.shift_right_logical(code, RANK_BITS)
    r = code & RANK_MASK
    pos = r
    for x in range(N_EXPERTS):
        pos = jnp.where(e == x, r + start_ref[x], pos)
    pos_ref[...] = pos * SUBLANES


def _slots(code, slot_start):
    k, t = code.shape
    tile = min(t, 4096)
    gs = pltpu.PrefetchScalarGridSpec(
        num_scalar_prefetch=1, grid=(t // tile,),
        in_specs=[pl.BlockSpec((k, tile), lambda i, *_: (0, i))],
        out_specs=pl.BlockSpec((k, tile), lambda i, *_: (0, i)))
    return pl.pallas_call(_slots_kernel, grid_spec=gs, out_shape=jax.ShapeDtypeStruct((k, t), I32),
                          compiler_params=_cparams("parallel"), name="moe_slots")(slot_start, code)


def _row_tile(ref, row0, lead=()):
    return ref.at[lead + (pl.ds(pl.multiple_of(row0, SUBLANES), SUBLANES),)]


def _dispatch_kernel(fill_ref, pos_ref, x_ref, xb_ref, wsg_ref, wsu_ref, wsd_ref, *rest, eps, fill_sizes):
    xs_ref, sh_ref, zero_ref, sem, fill_sem = rest[-5:]
    step = pl.program_id(0)
    tm = x_ref.shape[0] // SUBLANES
    zero_ref[...] = jnp.zeros_like(zero_ref)

    def copy(r, k):
        return pltpu.make_async_copy(_row_tile(x_ref, r * SUBLANES), _row_tile(xs_ref, pos_ref[0, 0, k * tm + r]), sem)

    def start(r, carry):
        for k in range(TOP_K):
            copy(r, k).start(priority=k % 2)
        return carry

    def fill_expert(e, carry):
        base = fill_ref[0, e]
        n = fill_ref[1, e]
        def zero_copy(p):
            first = base + (n & ~(2 * p - 1))
            return pltpu.make_async_copy(
                zero_ref.at[pl.ds(0, p * SUBLANES)],
                xs_ref.at[pl.ds(pl.multiple_of(first * SUBLANES, SUBLANES), p * SUBLANES)], fill_sem)

        for p in fill_sizes:
            pl.when((n & p) != 0)(zero_copy(p).start)
        for p in fill_sizes:
            pl.when((n & p) != 0)(zero_copy(p).wait)
        return carry

    lax.fori_loop(0, tm, start, 0)
    e0 = jnp.minimum(step * eps, N_EXPERTS)
    e1 = jnp.minimum(e0 + eps, N_EXPERTS)
    lax.fori_loop(e0, e1, fill_expert, 0)
    x = xb_ref[...]
    hg = jnp.dot(x, wsg_ref[...], preferred_element_type=F32)
    hu = jnp.dot(x, wsu_ref[...], preferred_element_type=F32)
    hb = (jax.nn.silu(hg) * hu).astype(BF16)
    sh_ref[...] = jnp.dot(hb, wsd_ref[...], preferred_element_type=F32).astype(BF16)
    all_rows = xs_ref.at[pl.ds(0, tm * TOP_K * SUBLANES)]
    pltpu.make_async_copy(all_rows, all_rows, sem).wait()


def _dispatch_shared(xp, xb, pos_tiles, fill, wsg, wsu, wsd, layer, nslot, tm, bm, into=None):
    t = xb.shape[0]
    steps = t // tm
    de = wsg.shape[-1]
    eps = -(-N_EXPERTS // steps)
    fill_sizes = tuple(1 << i for i in reversed(range((bm - 1).bit_length())))
    in_specs = [pl.BlockSpec((1, 1, TOP_K * tm), lambda i, *_: (i, 0, 0), memory_space=pltpu.SMEM),
                pl.BlockSpec((tm * SUBLANES, LANE), lambda i, *_: (i, 0)),
                pl.BlockSpec((tm, D_MODEL), lambda i, *_: (i, 0)),
                pl.BlockSpec((None, D_MODEL, de), lambda i, *_: (layer, 0, 0)),
                pl.BlockSpec((None, D_MODEL, de), lambda i, *_: (layer, 0, 0)),
                pl.BlockSpec((None, de, D_MODEL), lambda i, *_: (layer, 0, 0))]
    args = [fill, pos_tiles, xp, xb, wsg, wsu, wsd]
    if into is not None:
        in_specs.append(pl.BlockSpec(memory_space=pl.ANY))
        args.append(into)
    gs = pltpu.PrefetchScalarGridSpec(
        num_scalar_prefetch=1, grid=(steps,), in_specs=in_specs,
        out_specs=[pl.BlockSpec(memory_space=pl.ANY), pl.BlockSpec((tm, D_MODEL), lambda i, *_: (i, 0))],
        scratch_shapes=[pltpu.VMEM((fill_sizes[0] * SUBLANES, LANE), xp.dtype),
                        pltpu.SemaphoreType.DMA(()), pltpu.SemaphoreType.DMA(())])
    return pl.pallas_call(
        functools.partial(_dispatch_kernel, eps=eps, fill_sizes=fill_sizes), grid_spec=gs,
        out_shape=[jax.ShapeDtypeStruct((nslot * SUBLANES, LANE), xp.dtype),
                   jax.ShapeDtypeStruct((t, D_MODEL), BF16)],
        input_output_aliases={} if into is None else {len(args) - 1: 0},
        compiler_params=_cparams("arbitrary"), name="moe_dispatch_shared")(*args)


def _expert_kernel(be_ref, nu_ref, first_ref, nxt_ref, slot_ref, x_ref, wg_hbm, wu_hbm, wd_hbm, o_ref,
                   wg_buf, wu_buf, wd_buf, sems, *, layer):
    b = pl.program_id(0)
    used = b < nu_ref[0]
    slot = slot_ref[b]

    def fetch(e, s):
        return (pltpu.make_async_copy(wg_hbm.at[layer, e], wg_buf.at[s], sems.at[s, 0]),
                pltpu.make_async_copy(wu_hbm.at[layer, e], wu_buf.at[s], sems.at[s, 1]),
                pltpu.make_async_copy(wd_hbm.at[layer, e], wd_buf.at[s], sems.at[s, 2]))

    @pl.when(b == 0)
    def _():
        for cp in fetch(be_ref[0], 0):
            cp.start()

    @pl.when(jnp.logical_and(used, first_ref[b] == 1))
    def _():
        for cp in fetch(be_ref[b], slot):
            cp.wait()

        @pl.when(nxt_ref[b] >= 0)
        def _():
            for cp in fetch(nxt_ref[b], 1 - slot):
                cp.start()

    @pl.when(used)
    def _():
        lo, hi = _unpack_halves(_load_row_tiles(x_ref, (), x_ref.shape[0] // SUBLANES))
        lo = lo.astype(BF16)
        hi = hi.astype(BF16)
        w = lambda buf, r0, r1: buf[slot, r0:r1, :].astype(BF16)
        hg = (jnp.dot(lo, w(wg_buf, 0, HALF_W), preferred_element_type=F32)
              + jnp.dot(hi, w(wg_buf, HALF_W, D_MODEL), preferred_element_type=F32))
        hu = (jnp.dot(lo, w(wu_buf, 0, HALF_W), preferred_element_type=F32)
              + jnp.dot(hi, w(wu_buf, HALF_W, D_MODEL), preferred_element_type=F32))
        hb = (jax.nn.silu(hg) * hu).astype(BF16)
        out = jnp.dot(hb, w(wd_buf, 0, wd_buf.shape[1]), preferred_element_type=F32)
        _store_row_tiles(o_ref, (), _pack_halves(out))

    @pl.when(jnp.logical_not(used))
    def _():
        o_ref[...] = jnp.zeros_like(o_ref)


def _expert_ffn(xs, blk_e, nused, wg, wu, wd, layer, bm):
    nrows, width = xs.shape
    brows = bm * SUBLANES
    nb = nrows // brows
    de = wg.shape[-1]
    idx = jnp.arange(nb, dtype=I32)
    first = jnp.concatenate([jnp.ones((1,), bool), blk_e[1:] != blk_e[:-1]]) & (idx < nused[0])
    slot = lax.rem(jnp.cumsum(first.astype(I32)) - 1, 2)
    nxt_first = lax.cummin(jnp.where(first, idx, nb), axis=0, reverse=True)
    nxt_after = jnp.concatenate([nxt_first[1:], jnp.full((1,), nb, I32)])
    nxt_e = jnp.where(nxt_after < nb, blk_e[jnp.minimum(nxt_after, nb - 1)], -1)
    hbm = pl.BlockSpec(memory_space=pl.ANY)
    gs = pltpu.PrefetchScalarGridSpec(
        num_scalar_prefetch=5, grid=(nb,),
        in_specs=[pl.BlockSpec((brows, width), lambda b, be, nu, *_: (jnp.minimum(b, nu[0] - 1), 0)),
                  hbm, hbm, hbm],
        out_specs=pl.BlockSpec((brows, width), lambda b, *_: (b, 0)),
        scratch_shapes=[pltpu.VMEM((2, D_MODEL, de), wg.dtype), pltpu.VMEM((2, D_MODEL, de), wu.dtype),
                        pltpu.VMEM((2, de, D_MODEL), wd.dtype), pltpu.SemaphoreType.DMA((2, 3))])
    return pl.pallas_call(
        functools.partial(_expert_kernel, layer=layer), grid_spec=gs,
        out_shape=jax.ShapeDtypeStruct((nrows, width), xs.dtype),
        compiler_params=_cparams("arbitrary"), name="moe_expert_ffn")(
            blk_e, nused, first.astype(I32), nxt_e.astype(I32), slot.astype(I32), xs, wg, wu, wd)


def _combine_kernel(pos_ref, nxt_ref, gw_ref, x_ref, sh_ref, eo_ref, g_ref, b_ref, o_ref, *rest):
    feeds, (buf, sems) = rest[:-2], rest[-2:]
    step = pl.program_id(0)
    nsteps = pl.num_programs(0)
    tm = x_ref.shape[0]
    cur = lax.rem(step, 2)

    def copy(idx_ref, slot, r, k):
        return pltpu.make_async_copy(_row_tile(eo_ref, idx_ref[0, 0, k * tm + r]),
                                     _row_tile(buf, r * SUBLANES, (slot, k)), sems.at[slot])

    def gather(idx_ref, slot):
        def start(r, carry):
            for k in range(TOP_K):
                copy(idx_ref, slot, r, k).start(priority=k % 2)
            return carry
        lax.fori_loop(0, tm, start, 0)

    @pl.when(step == 0)
    def _():
        gather(pos_ref, 0)

    @pl.when(step + 1 < nsteps)
    def _():
        gather(nxt_ref, 1 - cur)

    pltpu.make_async_copy(buf.at[cur], buf.at[cur], sems.at[cur]).wait()
    sh = sh_ref[...].astype(F32)
    lo_acc = sh[:, :HALF_W]
    hi_acc = sh[:, HALF_W:]
    for k in range(TOP_K):
        lo, hi = _unpack_halves(_load_row_tiles(buf, (cur, k), tm))
        wk = gw_ref[:, k:k + 1]
        lo_acc = lo_acc + lo * wk
        hi_acc = hi_acc + hi * wk
    ffn = jnp.concatenate([lo_acc, hi_acc], 1)
    xn = _layer_norm(ALPHA * x_ref[...] + ffn, g_ref[...], b_ref[...])
    if feeds:
        _emit_x(xn, o_ref, *feeds)
    else:
        o_ref[...] = xn


def _combine_ln(pos_tiles, gw_t, x, sh, eo, g, b, tm, feed_next):
    t = x.shape[0]
    steps = t // tm
    row = lambda width: pl.BlockSpec((tm, width), lambda i: (i, 0))
    const = pl.BlockSpec((1, D_MODEL), lambda i: (0, 0))
    idx = lambda f: pl.BlockSpec((1, 1, TOP_K * tm), f, memory_space=pltpu.SMEM)
    out_specs, out_shape = _x_out(t, tm)
    if not feed_next:
        out_specs, out_shape = out_specs[:1], out_shape[:1]
    return pl.pallas_call(
        _combine_kernel, grid=(steps,),
        in_specs=[idx(lambda i: (i, 0, 0)), idx(lambda i: (jnp.minimum(i + 1, steps - 1), 0, 0)),
                  row(TOP_K), row(D_MODEL), row(D_MODEL), pl.BlockSpec(memory_space=pl.ANY), const, const],
        out_specs=out_specs, out_shape=out_shape,
        scratch_shapes=[pltpu.VMEM((2, TOP_K, tm * SUBLANES, LANE), U32), pltpu.SemaphoreType.DMA((2,))],
        compiler_params=_cparams("arbitrary"), name="moe_combine_ln")(
            pos_tiles, pos_tiles, gw_t, x, sh, eo, g.reshape(1, D_MODEL), b.reshape(1, D_MODEL))


def _moe_ln(groups, w_router, bias, wg, wu, wd, wsg, wsu, wsd, layer, g, b, bm, feed_next):
    cnt = jnp.zeros((N_EXPERTS, LANE), F32)
    routed = []
    for x, xb, xp, tm in groups:
        code, gw, cnt = _router(xb, w_router, bias, tm, cnt)
        routed.append((code, gw))
    t_all = sum(x.shape[0] for x, _, _, _ in groups)
    assert t_all * TOP_K < (1 << RANK_BITS)
    counts = cnt[:, 0].astype(I32)
    padded = (counts + bm - 1) // bm * bm
    pad_end = jnp.cumsum(padded)
    slot_start = pad_end - padded
    fill = jnp.stack([slot_start + counts, padded - counts])
    nb = -(-(t_all * TOP_K + N_EXPERTS * (bm - 1)) // bm)
    blk_first = jnp.arange(nb, dtype=I32) * bm
    blk_e = jnp.minimum(jnp.sum((pad_end[None, :] <= blk_first[:, None]).astype(I32), 1), N_EXPERTS - 1)
    nused = (pad_end[-1] // bm).astype(I32).reshape(1)
    xs = None
    pos_tiles, shared = [], []
    for (x, xb, xp, tm), (code, _) in zip(groups, routed):
        nt = x.shape[0] // tm
        pos = _slots(code, slot_start)
        pos_tiles.append(pos.reshape(TOP_K, nt, tm).transpose(1, 0, 2).reshape(nt, 1, TOP_K * tm))
        xs, sh = _dispatch_shared(xp, xb, pos_tiles[-1], fill if xs is None else jnp.zeros_like(fill),
                                  wsg, wsu, wsd, layer, nb * bm, tm, bm, xs)
        shared.append(sh)
    eo = _expert_ffn(xs, blk_e, nused, wg, wu, wd, layer, bm)
    return [_combine_ln(pt, gw.T, x, sh, eo, g, b, tm, feed_next)
            for (x, _, _, tm), (_, gw), pt, sh in zip(groups, routed, pos_tiles, shared)]


def _odd_w_in(w):
    sizes = (H_C * K_C, H_C * K_C, H_C * V_C, GLA_RANK, H_C * V_C, HALF_W, CONV_DIM, H_D)
    q, k, v, lr, r, z, xbc, dt = jnp.split(w, [int(s) for s in np.cumsum(sizes)[:-1]], axis=-1)
    pad = jnp.zeros((w.shape[0], OD_XBC - (OD_LRDT + GLA_RANK + H_D)), w.dtype)
    return jnp.concatenate([q, k, v, r, z, lr, dt, pad, xbc], -1)


def _mix_layer(grp, l, p):
    cfg, seq_len, bsz = grp["cfg"], grp["seq_len"], grp["bsz"]
    x, xb, outs = grp["x"], grp["xb"], grp["outs"]
    j = l // 2
    if l % 2 == 0:
        proj = _proj(x if xb is None else xb, p["ev_w_in"][j], cfg["ptm"], cfg["ptn"])
        ya, v_rows = _mixer_a(proj, p["ev_a_ln_g"][j], p["ev_a_ln_b"][j], p["ev_a_ws"][j],
                              p["ev_a_bs"][j], seq_len, cfg["arows"], cfg["emit_v"])
        yb, s_b = _mixer_b(proj, p["hgrn_lb_logits"], p["ev_b_norm_g"][j], grp["hgrn0"][j], seq_len,
                           cfg["rows"], l)
        outs["v_rows"] = v_rows
        outs["hgrn"] = s_b
        w_out = p["ev_w_out"][j]
    else:
        proj = _proj(x if xb is None else xb, p["od_w_in"][j], cfg["ptm"], cfg["ptn"])
        ya, s_c = _mixer_c(proj, p["od_c_gate_w2"][j], p["od_c_gate_b"][j], p["od_c_norm_g"][j],
                           grp["gla0"][j], seq_len, cfg["rows"])
        yb, s_d = _mixer_d(proj, p["od_d_conv_w"][j], p["od_d_conv_b"][j], p["od_d_dt_bias"][j],
                           p["od_d_a_log"][j], p["od_d_skip"][j], p["od_d_norm_g"][j],
                           grp["ssm0"][j], grp["conv0"][j], seq_len, cfg["rows"])
        keep = min(D_CONV - 1, seq_len)
        xbc = proj.reshape(bsz, seq_len, OD_PAD)[:, seq_len - keep:, OD_XBC:OD_XBC + CONV_DIM]
        outs["conv"] = jnp.concatenate([grp["conv0"][j], xbc], 1)[:, -(D_CONV - 1):]
        outs["gla"] = s_c
        outs["ssm"] = s_d
        w_out = p["od_w_out"][j]
    return _out_proj_ln(ya, yb, x, w_out, p["ln1_g"][l], p["ln1_b"][l], cfg["tm"])


def _group(x3, hgrn0, gla0, ssm0, conv0):
    bsz, seq_len, _ = x3.shape
    return dict(x=x3.reshape(bsz * seq_len, D_MODEL), xb=None, bsz=bsz, seq_len=seq_len, outs={},
                cfg=_config(bsz, seq_len), hgrn0=hgrn0, gla0=gla0, ssm0=ssm0, conv0=conv0)


def _config(bsz, seq_len):
    t = bsz * seq_len
    if seq_len % CHUNK == 0:
        return dict(tm=256, rows=min(256, seq_len), arows=min(256, seq_len),
                    ptm=min(1024, t), ptn=1024, emit_v=False)
    return dict(tm=t, rows=seq_len, arows=t, ptm=t, ptn=512, emit_v=True)


MOE_BLOCK_ROWS = 512


def kernel(x_prompt, x_sample, state_b_hgrn, state_c_gla, state_d_ssm, state_d_conv, ev_w_in, ev_a_ln_g, ev_a_ln_b, ev_a_ws, ev_a_bs, ev_b_norm_g, ev_w_out, hgrn_lb_logits, od_w_in, od_c_gate_w2, od_c_gate_b, od_c_norm_g, od_d_conv_w, od_d_conv_b, od_d_dt_bias, od_d_a_log, od_d_skip, od_d_norm_g, od_w_out, ln1_g, ln1_b, ln2_g, ln2_b, moe_w_router, moe_router_bias, moe_w_gate, moe_w_up, moe_w_down, moe_ws_gate, moe_ws_up, moe_ws_down):
    p = dict(
        ev_w_in=ev_w_in.astype(BF16), ev_a_ln_g=ev_a_ln_g, ev_a_ln_b=ev_a_ln_b, ev_a_ws=ev_a_ws,
        ev_a_bs=ev_a_bs, ev_b_norm_g=ev_b_norm_g, ev_w_out=ev_w_out.astype(BF16),
        hgrn_lb_logits=hgrn_lb_logits,
        od_w_in=jnp.stack([_odd_w_in(od_w_in[j]) for j in range(od_w_in.shape[0])]).astype(BF16),
        od_c_gate_w2=od_c_gate_w2, od_c_gate_b=od_c_gate_b, od_c_norm_g=od_c_norm_g,
        od_d_conv_w=od_d_conv_w, od_d_conv_b=od_d_conv_b, od_d_dt_bias=od_d_dt_bias,
        od_d_a_log=od_d_a_log, od_d_skip=od_d_skip, od_d_norm_g=od_d_norm_g,
        od_w_out=od_w_out.astype(BF16), ln1_g=ln1_g, ln1_b=ln1_b, ln2_g=ln2_g, ln2_b=ln2_b,
        moe_w_router=moe_w_router, moe_router_bias=moe_router_bias,
        moe_w_gate=moe_w_gate, moe_w_up=moe_w_up, moe_w_down=moe_w_down,
        moe_ws_gate=moe_ws_gate.astype(BF16),
        moe_ws_up=moe_ws_up.astype(BF16), moe_ws_down=moe_ws_down.astype(BF16))
    bp, lp, _ = x_prompt.shape
    n_even = state_b_hgrn.shape[0]
    n_odd = state_c_gla.shape[0]
    zeros = lambda n, *s: jnp.zeros((n, bp) + s, F32)
    bs, ls, _ = x_sample.shape
    groups = [_group(x_prompt, zeros(n_even, H_B, K_B, V_B), zeros(n_odd, H_C, K_C, V_C),
                     zeros(n_odd, H_D, N_D, P_D), zeros(n_odd, D_CONV - 1, CONV_DIM)),
              _group(x_sample, state_b_hgrn, state_c_gla, state_d_ssm, state_d_conv)]
    for l in range(DEPTH):
        mixed = [_mix_layer(grp, l, p) for grp in groups]
        res = _moe_ln([(x, xb, xp, grp["cfg"]["tm"]) for (x, xb, xp), grp in zip(mixed, groups)],
                      p["moe_w_router"][l], p["moe_router_bias"][l], p["moe_w_gate"], p["moe_w_up"],
                      p["moe_w_down"], p["moe_ws_gate"], p["moe_ws_up"], p["moe_ws_down"], l,
                      p["ln2_g"][l], p["ln2_b"][l], MOE_BLOCK_ROWS, l + 1 < DEPTH)
        for grp, out in zip(groups, res):
            grp["x"], grp["xb"] = out[0], (out[1] if len(out) > 1 else None)
    y_p = groups[0]["x"].reshape(bp, lp, D_MODEL)
    y_s = groups[1]["x"].reshape(bs, ls, D_MODEL)
    o_p, o_s = groups[0]["outs"], groups[1]["outs"]
    a_v = o_s["v_rows"].reshape(1, bs, ls, HALF_W)
    return (y_p, y_s, a_v, o_p["hgrn"][None], o_s["hgrn"][None], o_p["gla"][None], o_s["gla"][None],
            o_p["ssm"][None], o_s["ssm"][None], o_p["conv"][None], o_s["conv"][None])
```
